```python
import jax, jax.numpy as jnp
from jax import lax
import numpy as np

D_MODEL = 1024
BATCH = 4
SEQ = 4096
DEPTH = 2
DEC_BATCH = 16
DEC_SEQ = 16
PAST_LEN = 2048

CHUNK = 64
HEAD_DIM = 64
H_A = 8
H_B = 8
H_C = 8
H_D = 8
D_A = H_A * HEAD_DIM
D_B = H_B * HEAD_DIM
D_C = H_C * HEAD_DIM
D_D = H_D * HEAD_DIM
RWKV_W_LORA = 64
RWKV_A_LORA = 64
RWKV_G_LORA = 128
A_COLS = 3 * D_A + RWKV_W_LORA + RWKV_A_LORA + RWKV_G_LORA
B_COLS = 3 * D_B + H_B
AB_COLS = A_COLS + B_COLS
C_COLS = 3 * D_C
D_COLS = 4 * D_D
CD_COLS = C_COLS + D_COLS
FOX_Q_BLOCK = 128
BAND_CHUNKS = 8
BAND = BAND_CHUNKS * CHUNK
REL_CLIP = 128
D_FF = 2816
CONV_W = 3
N_AB = (DEPTH + 1) // 2
N_CD = DEPTH // 2
RMS_EPS = 1e-6
GN_EPS = 64e-5
ATTN_SCALE = HEAD_DIM ** -0.5

kernel_name = 'hybrid_streaming_encoder_step'

STATE_NAMES = ('fox_k', 'fox_v', 'fox_logf', 'rwkv', 'rwkv_shift', 'chunk_k', 'chunk_v', 'hgrn', 'ffn_conv')


def rmsnorm(x, g):
    xf = x.astype(jnp.float32)
    y = xf * lax.rsqrt(jnp.mean(xf * xf, axis=-1, keepdims=True) + RMS_EPS)
    return (y * g.astype(jnp.float32)).astype(x.dtype)


def adaln(cs, w, b):
    shift, scale, gate = jnp.split(cs @ w + b, 3, axis=-1)
    return shift[:, None], scale[:, None], gate[:, None]


def wkv7_scan(r, w, k, v, a, b, s0):
    seq = tuple(t.swapaxes(0, 1) for t in (r, w, k, v, a, b))
    def step(s, inp):
        r_t, w_t, k_t, v_t, a_t, b_t = inp
        sa = jnp.einsum('bhvk,bhk->bhv', s, a_t)
        s = s * w_t[:, :, None, :] + sa[..., None] * b_t[:, :, None, :] + v_t[..., None] * k_t[:, :, None, :]
        return s, jnp.einsum('bhvk,bhk->bhv', s, r_t)
    s_t, y = lax.scan(step, s0.astype(jnp.float32), seq)
    return y.swapaxes(0, 1), s_t


def rwkv7_mix(z, shift0, s0, mu, w0, w2, a0, a2, g2, k_k, k_a, r_k, lnx_g, lnx_b):
    B, T, _ = z.shape
    dt = z.dtype
    z_prev = jnp.concatenate([shift0[:, None].astype(dt), z[:, :-1]], axis=1)
    zs = z + mu * (z_prev - z)
    r, k, v = (zs[..., n * D_A:(n + 1) * D_A] for n in range(3))
    o = 3 * D_A
    w_lr = zs[..., o:o + RWKV_W_LORA]
    a_lr = zs[..., o + RWKV_W_LORA:o + RWKV_W_LORA + RWKV_A_LORA]
    g_lr = zs[..., o + RWKV_W_LORA + RWKV_A_LORA:]
    w = -jax.nn.softplus(-(w0 + jnp.tanh(w_lr) @ w2).astype(jnp.float32)) - 0.5
    decay = jnp.exp(-jnp.exp(w))
    a = jax.nn.sigmoid((a0 + a_lr @ a2).astype(jnp.float32))
    g = jax.nn.sigmoid(g_lr) @ g2
    heads = lambda t: t.reshape(B, T, H_A, HEAD_DIM).astype(jnp.float32)
    r, k, v, decay, a = heads(r), heads(k), heads(v), heads(decay), heads(a)
    kk = k * k_k.reshape(H_A, HEAD_DIM).astype(jnp.float32)
    kk = kk / jnp.maximum(jnp.sqrt(jnp.sum(kk * kk, axis=-1, keepdims=True)), 1e-12)
    k = k * (1.0 + (a - 1.0) * k_a.reshape(H_A, HEAD_DIM).astype(jnp.float32))
    y, s_t = wkv7_scan(r, decay, k, v, -kk, kk * a, s0)
    mean = jnp.mean(y, axis=-1, keepdims=True)
    var = jnp.mean(jnp.square(y - mean), axis=-1, keepdims=True)
    y = (y - mean) * lax.rsqrt(var + GN_EPS) * lnx_g.reshape(H_A, HEAD_DIM).astype(jnp.float32) \
        + lnx_b.reshape(H_A, HEAD_DIM).astype(jnp.float32)
    y = y + jnp.sum(r * k * r_k.astype(jnp.float32), axis=-1, keepdims=True) * v
    return y.reshape(B, T, D_A).astype(dt) * g, s_t, z[:, -1]


def fox_prompt(q, k, v, logf):
    B, S, H, dh = q.shape
    nb = S // FOX_Q_BLOCK
    cum = jnp.cumsum(logf.astype(jnp.float32), axis=1)
    pos = jnp.arange(S)
    q_blocks = q.reshape(B, nb, FOX_Q_BLOCK, H, dh).swapaxes(0, 1)
    f_blocks = cum.reshape(B, nb, FOX_Q_BLOCK, H).swapaxes(0, 1)
    p_blocks = pos.reshape(nb, FOX_Q_BLOCK)
    f_keys = cum.transpose(0, 2, 1)[:, :, None, :]
    def block(args):
        qi, fi, pi = args
        s = jnp.einsum('bqhd,bkhd->bhqk', qi, k).astype(jnp.float32) * ATTN_SCALE \
            + fi.transpose(0, 2, 1)[..., None] - f_keys
        s = jnp.where(pi[:, None] >= pos[None, :], s, -jnp.inf)
        p = jax.nn.softmax(s, axis=-1).astype(v.dtype)
        return jnp.einsum('bhqk,bkhd->bqhd', p, v)
    out = lax.map(block, (q_blocks, f_blocks, p_blocks))
    return out.swapaxes(0, 1).reshape(B, S, H, dh)


def fox_sample(q, k, v, logf, ck, cv, clogf):
    T = q.shape[1]
    P = ck.shape[1]
    keys = jnp.concatenate([ck.astype(k.dtype), k], axis=1)
    vals = jnp.concatenate([cv.astype(v.dtype), v], axis=1)
    cum = jnp.cumsum(jnp.concatenate([clogf.astype(jnp.float32), logf], axis=1), axis=1)
    s = jnp.einsum('bqhd,bkhd->bhqk', q, keys).astype(jnp.float32) * ATTN_SCALE \
        + cum[:, P:].transpose(0, 2, 1)[..., None] - cum.transpose(0, 2, 1)[:, :, None, :]
    mask = jnp.arange(P + T)[None, :] <= (P + jnp.arange(T))[:, None]
    s = jnp.where(mask, s, -jnp.inf)
    p = jax.nn.softmax(s, axis=-1).astype(vals.dtype)
    return jnp.einsum('bhqk,bkhd->bqhd', p, vals)


def chunk_attn_prompt(q, k, v, rel_bias):
    B, S, H, dh = q.shape
    nc = S // CHUNK
    nk = BAND + CHUNK
    pad = ((0, 0), (BAND, 0), (0, 0), (0, 0))
    kp, vp = jnp.pad(k, pad), jnp.pad(v, pad)
    idx = (jnp.arange(nc) * CHUNK)[:, None] + jnp.arange(nk)[None, :]
    kb, vb = kp[:, idx], vp[:, idx]
    qc = q.reshape(B, nc, CHUNK, H, dh)
    rel = jnp.arange(CHUNK)[:, None] + BAND - jnp.arange(nk)[None, :]
    bias = rel_bias[:, jnp.clip(rel, -REL_CLIP, REL_CLIP) + REL_CLIP].astype(jnp.float32)
    s = jnp.einsum('bcqhd,bckhd->bchqk', qc, kb).astype(jnp.float32) * ATTN_SCALE + bias[None, None]
    valid = idx >= BAND
    s = jnp.where(valid[None, :, None, None, :], s, -jnp.inf)
    p = jax.nn.softmax(s, axis=-1).astype(vb.dtype)
    return jnp.einsum('bchqk,bckhd->bcqhd', p, vb).reshape(B, S, H, dh)


def chunk_attn_sample(q, k, v, ck, cv, rel_bias):
    T = q.shape[1]
    P = ck.shape[1]
    keys = jnp.concatenate([ck.astype(k.dtype), k], axis=1)
    vals = jnp.concatenate([cv.astype(v.dtype), v], axis=1)
    rel = jnp.arange(T)[:, None] + P - jnp.arange(P + T)[None, :]
    bias = rel_bias[:, jnp.clip(rel, -REL_CLIP, REL_CLIP) + REL_CLIP].astype(jnp.float32)
    s = jnp.einsum('bqhd,bkhd->bhqk', q, keys).astype(jnp.float32) * ATTN_SCALE + bias[None]
    p = jax.nn.softmax(s, axis=-1).astype(vals.dtype)
    return jnp.einsum('bhqk,bkhd->bqhd', p, vals)


def gla_chunkwise(q, g, k, v, s0, L):
    B, T, H, K = q.shape
    n = T // L
    to_chunks = lambda t: t.reshape(B, n, L, H, t.shape[-1]).swapaxes(0, 1)
    causal = jnp.tril(jnp.ones((L, L), bool))
    def body(s, inp):
        qc, gc, kc, vc = inp
        b = jnp.cumsum(gc, axis=1)
        diff = jnp.where(causal[None, :, :, None, None], b[:, :, None] - b[:, None, :], -jnp.inf)
        att = jnp.einsum('bthk,bshk,btshk->bhts', qc, kc, jnp.exp(diff))
        o = jnp.einsum('bhts,bshv->bthv', att, vc) + jnp.einsum('bthk,bhkv->bthv', qc * jnp.exp(b), s)
        b_last = b[:, -1]
        s = jnp.exp(b_last)[..., None] * s + jnp.einsum('bshk,bshv->bhkv', kc * jnp.exp(b_last[:, None] - b), vc)
        return s, o
    s_t, o = lax.scan(body, s0, tuple(to_chunks(t) for t in (q, g, k, v)))
    return o.swapaxes(0, 1).reshape(B, T, H, v.shape[-1]), s_t


def hgrn2_mix(zq, zf, zi, zg, s0, lb, norm_g):
    B, T, _ = zq.shape
    dt = zq.dtype
    heads = lambda t: t.reshape(B, T, H_D, HEAD_DIM).astype(jnp.float32)
    q = jax.nn.silu(heads(zq))
    xf = heads(zf)
    v = heads(zi)
    lb = lb.reshape(H_D, HEAD_DIM)
    logf = jnp.logaddexp(jnp.log(lb), jnp.log1p(-lb) + jax.nn.log_sigmoid(xf))
    k_in = (1.0 - lb) * jax.nn.sigmoid(-xf)
    o, s_t = gla_chunkwise(q, logf, k_in, v, s0.astype(jnp.float32), min(CHUNK, T))
    o = o * lax.rsqrt(jnp.mean(o * o, axis=-1, keepdims=True) + RMS_EPS) * norm_g.reshape(H_D, HEAD_DIM).astype(jnp.float32)
    return o.reshape(B, T, D_D).astype(dt) * jax.nn.silu(zg), s_t


def ab_mixer(h, i, P, cache):
    B, T, _ = h.shape
    dt = h.dtype
    z = h @ P['ab_w_in'][i]
    za, zb = z[..., :A_COLS], z[..., A_COLS:]
    if cache is None:
        shift0 = jnp.zeros((B, A_COLS), dt)
        s0 = jnp.zeros((B, H_A, HEAD_DIM, HEAD_DIM), jnp.float32)
    else:
        shift0 = cache['rwkv_shift'][i]
        s0 = cache['rwkv'][i]
    ya, s_t, shift_t = rwkv7_mix(za, shift0, s0, P['rwkv_mu'][i], P['rwkv_w0'][i], P['rwkv_w2'][i],
                                 P['rwkv_a0'][i], P['rwkv_a2'][i], P['rwkv_g2'][i], P['rwkv_k_k'][i],
                                 P['rwkv_k_a'][i], P['rwkv_r_k'][i], P['rwkv_lnx_g'][i], P['rwkv_lnx_b'][i])
    q, k, v = (zb[..., n * D_B:(n + 1) * D_B].reshape(B, T, H_B, HEAD_DIM) for n in range(3))
    logf = jax.nn.log_sigmoid((zb[..., 3 * D_B:] + P['fox_b_f'][i]).astype(jnp.float32))
    if cache is None:
        yb = fox_prompt(q, k, v, logf)
    else:
        yb = fox_sample(q, k, v, logf, cache['fox_k'][i], cache['fox_v'][i], cache['fox_logf'][i])
    y = jnp.concatenate([ya, yb.reshape(B, T, D_B).astype(dt)], axis=-1) @ P['ab_w_out'][i]
    st = {'fox_k': k, 'fox_v': v, 'fox_logf': logf.astype(dt), 'rwkv': s_t.astype(dt), 'rwkv_shift': shift_t}
    return y, st


def cd_mixer(h, i, layer, P, cache):
    B, T, _ = h.shape
    dt = h.dtype
    z = h @ P['cd_w_in'][i]
    q, k, v = (z[..., n * D_C:(n + 1) * D_C].reshape(B, T, H_C, HEAD_DIM) for n in range(3))
    rel_bias = P['chunk_rel_bias'][i]
    if cache is None:
        yc = chunk_attn_prompt(q, k, v, rel_bias)
        keep = min(BAND, T)
        k_new, v_new = k[:, T - keep:], v[:, T - keep:]
        s0 = jnp.zeros((B, H_D, HEAD_DIM, HEAD_DIM), jnp.float32)
    else:
        yc = chunk_attn_sample(q, k, v, cache['chunk_k'][i], cache['chunk_v'][i], rel_bias)
        k_new, v_new = k, v
        s0 = cache['hgrn'][i]
    zq, zf, zi, zg = (z[..., C_COLS + n * D_D:C_COLS + (n + 1) * D_D] for n in range(4))
    sm = jax.nn.softmax(P['hgrn_lb_table'].astype(jnp.float32), axis=0)
    lb = (jnp.cumsum(sm, axis=0) - sm[0])[layer]
    yd, s_t = hgrn2_mix(zq, zf, zi, zg, s0, lb, P['hgrn_norm_g'][i])
    y = jnp.concatenate([yc.reshape(B, T, D_C).astype(dt), yd], axis=-1) @ P['cd_w_out'][i]
    st = {'chunk_k': k_new, 'chunk_v': v_new, 'hgrn': s_t.astype(dt)}
    return y, st


def conv_ffn(h, buf, w_up, conv_w, conv_b, w_down):
    T = h.shape[1]
    u = h @ w_up
    ext = jnp.concatenate([buf.astype(u.dtype), u], axis=1)
    acc = conv_b
    for j in range(CONV_W):
        acc = acc + conv_w[j] * ext[:, j:j + T]
    a, b = jnp.split(acc, 2, axis=-1)
    return (jax.nn.silu(a) * b) @ w_down, ext[:, T:]


def trunk(x, c, P, cache):
    B = x.shape[0]
    dt = x.dtype
    cs = jax.nn.silu(c)
    new = {name: [] for name in STATE_NAMES}
    for layer in range(DEPTH):
        shift, scale, gate = adaln(cs, P['ada_w'][layer, 0], P['ada_b'][layer, 0])
        h = rmsnorm(x, P['norm_mix_g'][layer]) * (1.0 + scale) + shift
        if layer % 2 == 0:
            y, st = ab_mixer(h, layer // 2, P, cache)
        else:
            y, st = cd_mixer(h, layer // 2, layer, P, cache)
        for name, val in st.items():
            new[name].append(val)
        x = x + gate * y
        shift, scale, gate = adaln(cs, P['ada_w'][layer, 1], P['ada_b'][layer, 1])
        h = rmsnorm(x, P['norm_ffn_g'][layer]) * (1.0 + scale) + shift
        buf = jnp.zeros((B, CONV_W - 1, 2 * D_FF), dt) if cache is None else cache['ffn_conv'][layer]
        y, buf_new = conv_ffn(h, buf, P['ffn_w_up'][layer], P['ffn_conv_w'][layer], P['ffn_conv_b'][layer], P['ffn_w_down'][layer])
        new['ffn_conv'].append(buf_new)
        x = x + gate * y
    return rmsnorm(x, P['final_norm_g']), {name: jnp.stack(vals) for name, vals in new.items()}


def setup_inputs(seed: int = 0) -> dict:
    key = jax.random.key(seed)
    ks = iter(jax.random.split(key, 64))
    nrm = lambda shape, s=1.0: s * jax.random.normal(next(ks), shape, jnp.float32)
    uni = lambda shape, lo, hi: jax.random.uniform(next(ks), shape, jnp.float32, lo, hi)
    c_cache = min(BAND, PAST_LEN)
    return {
        'x_prompt': nrm((BATCH, SEQ, D_MODEL)),
        'x_sample': nrm((DEC_BATCH, DEC_SEQ, D_MODEL)),
        'c_prompt': nrm((BATCH, D_MODEL)),
        'c_sample': nrm((DEC_BATCH, D_MODEL)),
        'cache_fox_k': nrm((N_AB, DEC_BATCH, PAST_LEN, H_B, HEAD_DIM)),
        'cache_fox_v': nrm((N_AB, DEC_BATCH, PAST_LEN, H_B, HEAD_DIM)),
        'cache_fox_logf': jax.nn.log_sigmoid(4.0 + nrm((N_AB, DEC_BATCH, PAST_LEN, H_B))),
        'state_rwkv': nrm((N_AB, DEC_BATCH, H_A, HEAD_DIM, HEAD_DIM), 0.5),
        'state_rwkv_shift': nrm((N_AB, DEC_BATCH, A_COLS)),
        'cache_chunk_k': nrm((N_CD, DEC_BATCH, c_cache, H_C, HEAD_DIM)),
        'cache_chunk_v': nrm((N_CD, DEC_BATCH, c_cache, H_C, HEAD_DIM)),
        'state_hgrn': nrm((N_CD, DEC_BATCH, H_D, HEAD_DIM, HEAD_DIM), 0.5),
        'state_ffn_conv': nrm((DEPTH, DEC_BATCH, CONV_W - 1, 2 * D_FF)),
        'ada_w': nrm((DEPTH, 2, D_MODEL, 3 * D_MODEL), 0.5 * D_MODEL ** -0.5),
        'ada_b': nrm((DEPTH, 2, 3 * D_MODEL), 0.02),
        'norm_mix_g': 1.0 + nrm((DEPTH, D_MODEL), 0.02),
        'norm_ffn_g': 1.0 + nrm((DEPTH, D_MODEL), 0.02),
        'ab_w_in': nrm((N_AB, D_MODEL, AB_COLS), D_MODEL ** -0.5),
        'rwkv_mu': uni((N_AB, A_COLS), 0.0, 1.0),
        'rwkv_w0': uni((N_AB, D_A), -6.0, -1.0),
        'rwkv_w2': nrm((N_AB, RWKV_W_LORA, D_A), 0.5 * RWKV_W_LORA ** -0.5),
        'rwkv_a0': nrm((N_AB, D_A), 0.1),
        'rwkv_a2': nrm((N_AB, RWKV_A_LORA, D_A), RWKV_A_LORA ** -0.5),
        'rwkv_g2': nrm((N_AB, RWKV_G_LORA, D_A), RWKV_G_LORA ** -0.5),
        'rwkv_k_k': 0.85 + nrm((N_AB, D_A), 0.05),
        'rwkv_k_a': 1.0 + nrm((N_AB, D_A), 0.05),
        'rwkv_r_k': nrm((N_AB, H_A, HEAD_DIM), 0.1),
        'rwkv_lnx_g': 1.0 + nrm((N_AB, D_A), 0.02),
        'rwkv_lnx_b': nrm((N_AB, D_A), 0.02),
        'fox_b_f': 4.0 + nrm((N_AB, H_B), 0.5),
        'ab_w_out': nrm((N_AB, D_A + D_B, D_MODEL), (D_A + D_B) ** -0.5),
        'cd_w_in': nrm((N_CD, D_MODEL, CD_COLS), D_MODEL ** -0.5),
        'chunk_rel_bias': nrm((N_CD, H_C, 2 * REL_CLIP + 1), 0.5),
        'hgrn_lb_table': nrm((DEPTH, D_D)),
        'hgrn_norm_g': 1.0 + nrm((N_CD, D_D), 0.02),
        'cd_w_out': nrm((N_CD, D_C + D_D, D_MODEL), (D_C + D_D) ** -0.5),
        'ffn_w_up': nrm((DEPTH, D_MODEL, 2 * D_FF), D_MODEL ** -0.5),
        'ffn_conv_w': nrm((DEPTH, CONV_W, 2 * D_FF), CONV_W ** -0.5),
        'ffn_conv_b': nrm((DEPTH, 2 * D_FF), 0.02),
        'ffn_w_down': nrm((DEPTH, D_FF, D_MODEL), D_FF ** -0.5),
        'final_norm_g': 1.0 + nrm((D_MODEL,), 0.02),
    }


def reference(x_prompt, x_sample, c_prompt, c_sample,
              cache_fox_k, cache_fox_v, cache_fox_logf, state_rwkv, state_rwkv_shift,
              cache_chunk_k, cache_chunk_v, state_hgrn, state_ffn_conv,
              ada_w, ada_b, norm_mix_g, norm_ffn_g,
              ab_w_in, rwkv_mu, rwkv_w0, rwkv_w2, rwkv_a0, rwkv_a2, rwkv_g2, rwkv_k_k, rwkv_k_a,
              rwkv_r_k, rwkv_lnx_g, rwkv_lnx_b, fox_b_f, ab_w_out,
              cd_w_in, chunk_rel_bias, hgrn_lb_table, hgrn_norm_g, cd_w_out,
              ffn_w_up, ffn_conv_w, ffn_conv_b, ffn_w_down, final_norm_g):
    P = {'ada_w': ada_w, 'ada_b': ada_b, 'norm_mix_g': norm_mix_g, 'norm_ffn_g': norm_ffn_g,
         'ab_w_in': ab_w_in, 'rwkv_mu': rwkv_mu, 'rwkv_w0': rwkv_w0, 'rwkv_w2': rwkv_w2,
         'rwkv_a0': rwkv_a0, 'rwkv_a2': rwkv_a2, 'rwkv_g2': rwkv_g2, 'rwkv_k_k': rwkv_k_k,
         'rwkv_k_a': rwkv_k_a, 'rwkv_r_k': rwkv_r_k, 'rwkv_lnx_g': rwkv_lnx_g, 'rwkv_lnx_b': rwkv_lnx_b,
         'fox_b_f': fox_b_f, 'ab_w_out': ab_w_out, 'cd_w_in': cd_w_in, 'chunk_rel_bias': chunk_rel_bias,
         'hgrn_lb_table': hgrn_lb_table, 'hgrn_norm_g': hgrn_norm_g, 'cd_w_out': cd_w_out,
         'ffn_w_up': ffn_w_up, 'ffn_conv_w': ffn_conv_w, 'ffn_conv_b': ffn_conv_b,
         'ffn_w_down': ffn_w_down, 'final_norm_g': final_norm_g}
    cache = {'fox_k': cache_fox_k, 'fox_v': cache_fox_v, 'fox_logf': cache_fox_logf,
             'rwkv': state_rwkv, 'rwkv_shift': state_rwkv_shift, 'chunk_k': cache_chunk_k,
             'chunk_v': cache_chunk_v, 'hgrn': state_hgrn, 'ffn_conv': state_ffn_conv}
    y_prompt, sp = trunk(x_prompt, c_prompt, P, None)
    y_sample, ss = trunk(x_sample, c_sample, P, cache)
    return (y_prompt, y_sample,
            sp['fox_k'], sp['fox_v'], sp['fox_logf'], sp['rwkv'], sp['rwkv_shift'],
            sp['chunk_k'], sp['chunk_v'], sp['hgrn'], sp['ffn_conv'],
            ss['fox_k'], ss['fox_v'], ss['fox_logf'], ss['rwkv'], ss['rwkv_shift'],
            ss['chunk_k'], ss['chunk_v'], ss['hgrn'], ss['ffn_conv'])
```

```python
import functools
import math

import jax
import jax.numpy as jnp
from jax import lax
from jax.experimental import pallas as pl
from jax.experimental.pallas import tpu as pltpu

F32 = jnp.float32
BF16 = jnp.bfloat16
HI = lax.Precision.HIGHEST

HEAD_DIM = 64
N_HEADS = 8
D_MIX = N_HEADS * HEAD_DIM
GROUP = 256
N_GROUPS = D_MIX // GROUP
HEADS_PER_GROUP = GROUP // HEAD_DIM
SUB = 16
CHUNK = 64
BAND = 512
REL_CLIP = 128
RMS_EPS = 1e-6
GN_EPS = 64e-5
ATTN_SCALE = HEAD_DIM ** -0.5
NEG = -1e30
VMEM_LIMIT = 56 * 1024 * 1024


def _dot(a, b, prec=None):
    return jnp.dot(a, b, preferred_element_type=F32, precision=prec)


def _dot_nt(a, b, prec=None):
    return lax.dot_general(a, b, (((1,), (1,)), ((), ())), preferred_element_type=F32, precision=prec)


def _dot_tn(a, b, prec=None):
    return lax.dot_general(a, b, (((0,), (0,)), ((), ())), preferred_element_type=F32, precision=prec)


def _dot_sel(x, w_bf16):
    x0 = x.astype(BF16)
    r1 = x - x0.astype(F32)
    x1 = r1.astype(BF16)
    x2 = (r1 - x1.astype(F32)).astype(BF16)
    return _dot(x0, w_bf16) + _dot(x1, w_bf16) + _dot(x2, w_bf16)


def _sigmoid(x):
    return jax.nn.sigmoid(x)


def _silu(x):
    return x * jax.nn.sigmoid(x)


def _softplus(x):
    return jnp.maximum(x, 0.0) + jnp.log1p(jnp.exp(-jnp.abs(x)))


def _log_sigmoid(x):
    return -_softplus(-x)


def _iota(shape, dim):
    return lax.broadcasted_iota(jnp.int32, shape, dim)


def _const_spec(shape):
    nd = len(shape)
    return pl.BlockSpec(shape, lambda *_: (0,) * nd, pipeline_mode=pl.Buffered(1))


def _params(sem):
    return pltpu.CompilerParams(dimension_semantics=sem, vmem_limit_bytes=VMEM_LIMIT)


def _norm_mod(x, g, shift, scale):
    xn = x * lax.rsqrt(jnp.mean(x * x, axis=-1, keepdims=True) + RMS_EPS) * g
    return xn * (1.0 + scale) + shift


def _block_ones(n, blk):
    i = jnp.arange(n) // blk
    return (i[:, None] == i[None, :]).astype(BF16)


def _bd(x, rows_per_head, cols_per_head):
    n = HEADS_PER_GROUP
    t = jnp.concatenate([x] * n, axis=0)
    r = _iota(t.shape, 0) // rows_per_head
    c = _iota(t.shape, 1) // cols_per_head
    return jnp.where(r == c, t, 0.0)


def _adaln_kernel(c_ref, w_ref, b_ref, o_ref):
    o_ref[0] = _dot(_silu(c_ref[...]), w_ref[0], HI) + b_ref[0]


def _adaln(c_all, ada_w, ada_b):
    n, d, d3 = ada_w.shape
    bt = c_all.shape[0]
    nt = d3 // d
    return pl.pallas_call(
        _adaln_kernel,
        grid=(n, nt),
        in_specs=[pl.BlockSpec((bt, d), lambda i, j: (0, 0)),
                  pl.BlockSpec((1, d, d), lambda i, j: (i, 0, j)),
                  pl.BlockSpec((1, 1, d), lambda i, j: (i, 0, j))],
        out_specs=pl.BlockSpec((1, bt, d), lambda i, j: (i, 0, j)),
        out_shape=jax.ShapeDtypeStruct((n, bt, d3), F32),
        compiler_params=_params(("parallel", "parallel")),
        name="adaln",
    )(c_all, ada_w, ada_b.reshape(n, 1, d3))


def _proj_kernel(*refs, splits, fox):
    x_ref, mod_ref, g_ref, w_ref = refs[:4]
    pos = 4
    if fox:
        wf_ref, bf_ref = refs[4:6]
        pos = 6
    outs = refs[pos:]
    d = x_ref.shape[-1]
    mod = mod_ref[0]
    h = _norm_mod(x_ref[0], g_ref[...], mod[:, :d], mod[:, d:2 * d]).astype(BF16)
    off = 0
    for o_ref, n in zip(outs, splits):
        o_ref[0] = _dot(h, w_ref[:, off:off + n])
        off += n
    if fox:
        zf = _dot(h, wf_ref[...]) + bf_ref[...]
        outs[len(splits)][0] = _log_sigmoid(zf)[:, :N_HEADS]


def _proj(x, mod, g, w_bf, splits, tm, wf=None, bf=None):
    b, t, d = x.shape
    r = mod.shape[1]
    fox = wf is not None
    mod_spec = (pl.BlockSpec((1, 1, 3 * d), lambda i, j: (i, 0, 0)) if r == 1
                else pl.BlockSpec((1, tm, 3 * d), lambda i, j: (i, j, 0)))
    in_specs = [pl.BlockSpec((1, tm, d), lambda i, j: (i, j, 0)), mod_spec,
                _const_spec((1, d)), _const_spec(w_bf.shape)]
    args = [x, mod, g.reshape(1, d), w_bf]
    widths = list(splits)
    if fox:
        in_specs += [_const_spec(wf.shape), _const_spec(bf.shape)]
        args += [wf, bf]
        widths.append(N_HEADS)
    return pl.pallas_call(
        functools.partial(_proj_kernel, splits=tuple(splits), fox=fox),
        grid=(b, t // tm),
        in_specs=in_specs,
        out_specs=[pl.BlockSpec((1, tm, n), lambda i, j: (i, j, 0)) for n in widths],
        out_shape=[jax.ShapeDtypeStruct((b, t, n), F32) for n in widths],
        compiler_params=_params(("parallel", "parallel")),
        name="proj",
    )(*args)


def _tail_kernel(*refs, sample, final, period, cb):
    (x_ref, ya_ref, yb_ref, m1_ref, m2_ref, wo_ref, g_ref, wup_ref, cw_ref, cbias_ref,
     wdn_ref, fg_ref) = refs[:12]
    if sample:
        fix1_ref, fix2_ref, o_ref, u_ref = refs[12:]
    else:
        o_ref, cs_ref, tail_sc = refs[12:]
    tm, d = x_ref.shape[1], x_ref.shape[2]
    ff = wdn_ref.shape[0]
    dm = ya_ref.shape[2]
    m1 = m1_ref[0]
    m2 = m2_ref[0]
    ymix = (_dot(ya_ref[0].astype(BF16), wo_ref[0:dm, :])
            + _dot(yb_ref[0].astype(BF16), wo_ref[dm:2 * dm, :]))
    x1 = x_ref[0] + m1[:, 2 * d:3 * d] * ymix
    h = _norm_mod(x1, g_ref[...], m2[:, :d], m2[:, d:2 * d]).astype(BF16)
    rows = _iota((tm, 1), 0)
    if sample:
        tpos = rows & (period - 1)
    else:
        t = pl.program_id(1)

        @pl.when(t == 0)
        def _():
            tail_sc[...] = jnp.zeros_like(tail_sc)

    acc = jnp.zeros((tm, d), F32)
    for j in range(ff // cb):
        halves = []
        for half in range(2):
            c0 = half * ff + j * cb
            u = _dot(h, wup_ref[:, c0:c0 + cb])
            r1 = pltpu.roll(u, 1, 0)
            r2 = pltpu.roll(u, 2, 0)
            if sample:
                u1 = jnp.where(tpos == 0, fix1_ref[0, :, c0:c0 + cb], r1)
                u2 = jnp.where(tpos < 2, fix2_ref[0, :, c0:c0 + cb], r2)
                u_ref[0, :, c0:c0 + cb] = u
            else:
                t6 = tail_sc[6:7, c0:c0 + cb]
                t7 = tail_sc[7:8, c0:c0 + cb]
                u1 = jnp.where(rows == 0, t7, r1)
                u2 = jnp.where(rows == 0, t6, jnp.where(rows == 1, t7, r2))
                tail_sc[:, c0:c0 + cb] = u[tm - 8:tm, :]
            cw = cw_ref[:, c0:c0 + cb]
            halves.append(cbias_ref[:, c0:c0 + cb] + cw[0:1] * u2 + cw[1:2] * u1 + cw[2:3] * u)
        gact = (_silu(halves[0]) * halves[1]).astype(BF16)
        acc = acc + _dot(gact, wdn_ref[j * cb:(j + 1) * cb, :])
    out = x1 + m2[:, 2 * d:3 * d] * acc
    if final:
        out = out * lax.rsqrt(jnp.mean(out * out, axis=-1, keepdims=True) + RMS_EPS) * fg_ref[...]
    o_ref[0] = out
    if not sample:
        @pl.when(t == pl.num_programs(1) - 1)
        def _():
            cs_ref[0] = tail_sc[6:8, :]


def _tail(x, ya, yb, mod1, mod2, wo_bf, g, wup_bf, conv_w, conv_b, wdn_bf, fg, final, tm,
          fix=None, period=None):
    b, t, d = x.shape
    ff = wdn_bf.shape[0]
    dm = ya.shape[2]
    sample = fix is not None
    r = mod1.shape[1]
    mod_spec = (pl.BlockSpec((1, 1, 3 * d), lambda i, j: (i, 0, 0)) if r == 1
                else pl.BlockSpec((1, tm, 3 * d), lambda i, j: (i, j, 0)))
    row_spec = lambda n: pl.BlockSpec((1, tm, n), lambda i, j: (i, j, 0))
    in_specs = [row_spec(d), row_spec(dm), row_spec(dm), mod_spec, mod_spec,
                _const_spec(wo_bf.shape), _const_spec((1, d)), _const_spec(wup_bf.shape),
                _const_spec(conv_w.shape), _const_spec((1, 2 * ff)), _const_spec(wdn_bf.shape),
                _const_spec((1, d))]
    args = [x, ya, yb, mod1, mod2, wo_bf, g.reshape(1, d), wup_bf, conv_w, conv_b.reshape(1, 2 * ff),
            wdn_bf, fg.reshape(1, d)]
    if sample:
        in_specs += [row_spec(2 * ff), row_spec(2 * ff)]
        args += list(fix)
        out_specs = [row_spec(d), row_spec(2 * ff)]
        out_shape = [jax.ShapeDtypeStruct((b, t, d), F32), jax.ShapeDtypeStruct((b, t, 2 * ff), F32)]
        scratch = []
    else:
        out_specs = [row_spec(d), pl.BlockSpec((1, 2, 2 * ff), lambda i, j: (i, 0, 0))]
        out_shape = [jax.ShapeDtypeStruct((b, t, d), F32), jax.ShapeDtypeStruct((b, 2, 2 * ff), F32)]
        scratch = [pltpu.VMEM((8, 2 * ff), F32)]
    return pl.pallas_call(
        functools.partial(_tail_kernel, sample=sample, final=final, period=period, cb=256),
        grid=(b, t // tm),
        in_specs=in_specs,
        out_specs=out_specs,
        out_shape=out_shape,
        scratch_shapes=scratch,
        compiler_params=_params(("parallel", "arbitrary")),
        name="tail",
    )(*args)


def _rwkv_kernel(z_ref, shift0_ref, s0_ref, mu_ref, w0_ref, a0_ref, wl_ref, kk_ref, ka_ref, rk_ref,
                 lng_ref, lnb_ref, bo_ref, y_ref, st_ref, prev_sc, st_sc, *, L):
    t = pl.program_id(1)

    @pl.when(t == 0)
    def _():
        prev_sc[...] = shift0_ref[0]
        st_sc[...] = s0_ref[0]

    z = z_ref[0]
    rows = _iota((L, 1), 0)
    z_prev = jnp.where(rows == 0, prev_sc[...], pltpu.roll(z, 1, 0))
    prev_sc[...] = z[L - 1:L, :]
    zs = z + mu_ref[...] * (z_prev - z)
    dm = D_MIX
    r = zs[:, 0:dm]
    k = zs[:, dm:2 * dm]
    v = zs[:, 2 * dm:3 * dm]
    lr = zs[:, 3 * dm:]
    nl = lr.shape[1]
    lane = _iota((1, nl), 1)
    act = jnp.where(lane < 64, jnp.tanh(lr), jnp.where(lane < 128, lr, _sigmoid(lr)))
    lo = _dot(act, wl_ref[...], HI)
    w = -_softplus(-(w0_ref[...] + lo[:, 0:dm])) - 0.5
    lw = -jnp.exp(w)
    a = _sigmoid(a0_ref[...] + lo[:, dm:2 * dm])
    g = lo[:, 2 * dm:3 * dm]
    bo = bo_ref[...]
    kk = k * kk_ref[...]
    kk = kk / jnp.maximum(jnp.sqrt(_dot_sel(kk * kk, bo)), 1e-12)
    k2 = k * (1.0 + (a - 1.0) * ka_ref[...])
    am = -kk
    bm = kk * a

    tri = (_iota((L, L), 1) <= _iota((L, L), 0)).astype(F32)
    c = _dot(tri, lw, HI)
    c_last = c[L - 1:L, :]
    a_t = am * jnp.exp(c - lw)
    r_t = r * jnp.exp(c)
    g_inv = jnp.exp(-c)
    k_t = k2 * g_inv
    b_t = bm * g_inv
    g_end = jnp.exp(c_last - c)
    k_end = k2 * g_end
    b_end = bm * g_end

    cw = HEADS_PER_GROUP * L
    e_tile = ((_iota((L, cw), 1) % L) == _iota((L, cw), 0)).astype(F32)
    tmask = (_iota((GROUP, cw), 0) // HEAD_DIM) == (_iota((GROUP, cw), 1) // L)
    col_s = _iota((L, cw), 1) % L
    row_t = _iota((L, cw), 0)
    strict = col_s < row_t
    incl = col_s <= row_t
    bdmask = (_iota((GROUP, GROUP), 0) // HEAD_DIM) == (_iota((GROUP, GROUP), 1) // HEAD_DIM)
    ones_l = jnp.ones((L, GROUP), F32)
    n_dbl = int(math.log2(L))

    ys = []
    for gi in range(N_GROUPS):
        sl = slice(gi * GROUP, (gi + 1) * GROUP)
        s = st_sc[gi]
        bdt_b = jnp.where(tmask, _dot_tn(b_t[:, sl], e_tile, HI), 0.0)
        bdt_k = jnp.where(tmask, _dot_tn(k_t[:, sl], e_tile, HI), 0.0)
        ar = jnp.concatenate([a_t[:, sl], r_t[:, sl]], axis=0)
        arb = _dot(ar, bdt_b, HI)
        ark = _dot(ar, bdt_k, HI)
        ars = _dot(ar, s, HI)
        n_ab = jnp.where(strict, arb[:L], 0.0)
        m_rb = jnp.where(incl, arb[L:], 0.0)
        n_ak = jnp.where(strict, ark[:L], 0.0)
        m_rk = jnp.where(incl, ark[L:], 0.0)
        vg = v[:, sl]
        bd_v = _bd(vg, L, HEAD_DIM)
        u = ars[:L] + _dot(n_ak, bd_v, HI)
        p = n_ab
        for i in range(n_dbl):
            u = u + _dot(p, _bd(u, L, HEAD_DIM), HI)
            if i + 1 < n_dbl:
                p = _dot(p, _bd(p, L, L), HI)
        ys.append(ars[L:] + _dot(m_rk, bd_v, HI) + _dot(m_rb, _bd(u, L, HEAD_DIM), HI))
        kb = jnp.concatenate([k_end[:, sl], b_end[:, sl]], axis=0)
        vu = jnp.concatenate([vg, u], axis=0)
        decay_col = jnp.exp(_dot_tn(lw[:, sl], ones_l, HI))
        st_sc[gi] = decay_col * s + jnp.where(bdmask, _dot_tn(kb, vu, HI), 0.0)

    y = jnp.concatenate(ys, axis=1)
    inv = 1.0 / HEAD_DIM
    mean = _dot_sel(y, bo) * inv
    dlt = y - mean
    var = _dot_sel(dlt * dlt, bo) * inv
    yn = dlt * lax.rsqrt(var + GN_EPS) * lng_ref[...] + lnb_ref[...]
    yn = yn + _dot_sel(r * k2 * rk_ref[...], bo) * v
    y_ref[0] = yn * g

    @pl.when(t == pl.num_programs(1) - 1)
    def _():
        st_ref[0] = st_sc[...]


def _rwkv(za, shift0, s0_bd, pr, L):
    b, t, ac = za.shape
    dm = D_MIX
    row = lambda a: a.reshape(1, -1)
    consts = [row(pr["mu"]), row(pr["w0"]), row(pr["a0"]), pr["wl"], row(pr["k_k"]), row(pr["k_a"]),
              row(pr["r_k"]), row(pr["lnx_g"]), row(pr["lnx_b"]), _block_ones(dm, HEAD_DIM)]
    return pl.pallas_call(
        functools.partial(_rwkv_kernel, L=L),
        grid=(b, t // L),
        in_specs=[pl.BlockSpec((1, L, ac), lambda i, j: (i, j, 0)),
                  pl.BlockSpec((1, 1, ac), lambda i, j: (i, 0, 0)),
                  pl.BlockSpec((1, N_GROUPS, GROUP, GROUP), lambda i, j: (i, 0, 0, 0))]
                 + [_const_spec(c.shape) for c in consts],
        out_specs=[pl.BlockSpec((1, L, dm), lambda i, j: (i, j, 0)),
                   pl.BlockSpec((1, N_GROUPS, GROUP, GROUP), lambda i, j: (i, 0, 0, 0))],
        out_shape=[jax.ShapeDtypeStruct((b, t, dm), F32),
                   jax.ShapeDtypeStruct((b, N_GROUPS, GROUP, GROUP), F32)],
        scratch_shapes=[pltpu.VMEM((1, ac), F32), pltpu.VMEM((N_GROUPS, GROUP, GROUP), F32)],
        compiler_params=_params(("parallel", "arbitrary")),
        name="rwkv7",
    )(za, shift0.reshape(b, 1, ac), s0_bd, *consts)


def _hgrn_kernel(zq_ref, zf_ref, zi_ref, zg_ref, s0_ref, lb_ref, ng_ref, bo_ref, y_ref, st_ref, st_sc, *, L):
    t = pl.program_id(1)

    @pl.when(t == 0)
    def _():
        st_sc[...] = s0_ref[0]

    dm = D_MIX
    q = _silu(zq_ref[0])
    xf = zf_ref[0]
    v = zi_ref[0]
    lb = lb_ref[...]
    la = jnp.log(lb)
    lc = jnp.log1p(-lb) + _log_sigmoid(xf)
    logf = jnp.maximum(la, lc) + jnp.log1p(jnp.exp(-jnp.abs(la - lc)))
    k = (1.0 - lb) * _sigmoid(-xf)
    bo = bo_ref[...]

    tri = (_iota((L, L), 1) <= _iota((L, L), 0)).astype(F32)
    b = _dot(tri, logf, HI)
    b_last = b[L - 1:L, :]
    qe = q * jnp.exp(b)
    k_end = k * jnp.exp(b_last - b)

    o = jnp.concatenate([_dot(qe[:, gi * GROUP:(gi + 1) * GROUP], st_sc[gi], HI)
                         for gi in range(N_GROUPS)], axis=1)

    n_sub = L // SUB
    nst = N_HEADS * SUB
    stackmask = (_iota((nst, dm), 0) // SUB) == (_iota((nst, dm), 1) // HEAD_DIM)
    sub_rows = _iota((SUB, 1), 0)
    o_subs = []
    for i in range(n_sub):
        lo_, hi_ = i * SUB, (i + 1) * SUB
        bi = b[lo_:hi_]
        qi = q[lo_:hi_]
        ki = k[lo_:hi_]
        vi = v[lo_:hi_]
        xs = []
        for s in range(SUB):
            e = jnp.exp(jnp.where(sub_rows >= s, bi - bi[s:s + 1], NEG))
            xs.append(qi * e * ki[s:s + 1])
        att = _dot_sel(jnp.concatenate(xs, axis=0), bo)
        oi = att[0:SUB] * vi[0:1]
        for s in range(1, SUB):
            oi = oi + att[s * SUB:(s + 1) * SUB] * vi[s:s + 1]
        if i > 0:
            bref = b[lo_ - 1:lo_]
            qh = qi * jnp.exp(bi - bref)
            kh = k[0:lo_] * jnp.exp(bref - b[0:lo_])
            qst = jnp.where(stackmask, jnp.concatenate([qh] * N_HEADS, axis=0), 0.0)
            att2 = _dot_nt(qst, kh, HI)
            ov = jnp.where(stackmask, _dot(att2, v[0:lo_], HI), 0.0)
            for hh in range(N_HEADS):
                oi = oi + ov[hh * SUB:(hh + 1) * SUB]
        o_subs.append(oi)
    o = o + (jnp.concatenate(o_subs, axis=0) if n_sub > 1 else o_subs[0])

    bdmask = (_iota((GROUP, GROUP), 0) // HEAD_DIM) == (_iota((GROUP, GROUP), 1) // HEAD_DIM)
    ones_l = jnp.ones((L, GROUP), F32)
    for gi in range(N_GROUPS):
        sl = slice(gi * GROUP, (gi + 1) * GROUP)
        decay_col = jnp.exp(_dot_tn(logf[:, sl], ones_l, HI))
        st_sc[gi] = decay_col * st_sc[gi] + jnp.where(bdmask, _dot_tn(k_end[:, sl], v[:, sl], HI), 0.0)

    ms = _dot_sel(o * o, bo) * (1.0 / HEAD_DIM)
    y_ref[0] = o * lax.rsqrt(ms + RMS_EPS) * ng_ref[...] * _silu(zg_ref[0])

    @pl.when(t == pl.num_programs(1) - 1)
    def _():
        st_ref[0] = st_sc[...]


def _hgrn(zq, zf, zi, zg, s0_bd, lb, norm_g, L):
    b, t, dm = zq.shape
    consts = [lb.reshape(1, dm), norm_g.reshape(1, dm), _block_ones(dm, HEAD_DIM)]
    row_spec = pl.BlockSpec((1, L, dm), lambda i, j: (i, j, 0))
    st_spec = pl.BlockSpec((1, N_GROUPS, GROUP, GROUP), lambda i, j: (i, 0, 0, 0))
    return pl.pallas_call(
        functools.partial(_hgrn_kernel, L=L),
        grid=(b, t // L),
        in_specs=[row_spec] * 4 + [st_spec] + [_const_spec(c.shape) for c in consts],
        out_specs=[row_spec, st_spec],
        out_shape=[jax.ShapeDtypeStruct((b, t, dm), F32),
                   jax.ShapeDtypeStruct((b, N_GROUPS, GROUP, GROUP), F32)],
        scratch_shapes=[pltpu.VMEM((N_GROUPS, GROUP, GROUP), F32)],
        compiler_params=_params(("parallel", "arbitrary")),
        name="hgrn2",
    )(zq, zf, zi, zg, s0_bd, *consts)


def _cumsum_kernel(x_ref, o_ref):
    x = x_ref[0]
    n, w = x.shape
    nblk = n // N_HEADS
    upper = (_iota((w, w), 0) <= _iota((w, w), 1)).astype(F32)
    c = _dot(x, upper, HI)
    tot = jnp.broadcast_to(c[:, w - 1:w], (n, w))
    ri = _iota((n, n), 0)
    ci = _iota((n, n), 1)
    prior = ((ri // nblk == ci // nblk) & (ci < ri)).astype(F32)
    o_ref[0] = c + _dot(prior, tot, HI)


def _cumsum_time(logf_bht):
    b, h, t = logf_bht.shape
    w = 128
    n = h * (t // w)
    x = logf_bht.reshape(b, n, w)
    out = pl.pallas_call(
        _cumsum_kernel,
        grid=(b,),
        in_specs=[pl.BlockSpec((1, n, w), lambda i: (i, 0, 0))],
        out_specs=pl.BlockSpec((1, n, w), lambda i: (i, 0, 0)),
        out_shape=jax.ShapeDtypeStruct((b, n, w), F32),
        compiler_params=_params(("parallel",)),
        name="cumsum",
    )(x)
    return out.reshape(b, h, t)


def _fox_kernel(q_ref, k_ref, v_ref, f_ref, o_ref, *, tq):
    i = pl.program_id(2)
    q = q_ref[0] * ATTN_SCALE
    lane = _iota((1, 2 * HEAD_DIM), 1)
    causal = _iota((tq, tq), 1) <= _iota((tq, tq), 0)
    outs = []
    for hh in range(2):
        qm = jnp.where((lane // HEAD_DIM) == hh, q, 0.0).astype(BF16)

        def scores(j):
            kb = k_ref[0, pl.ds(j * tq, tq), :].astype(BF16)
            return _dot_nt(qm, kb) - f_ref[0, 0, j, hh:hh + 1, :]

        def update(j, s, carry):
            m, l, acc = carry
            m_new = jnp.maximum(m, jnp.max(s, axis=-1, keepdims=True))
            alpha = jnp.exp(m - m_new)
            p = jnp.exp(s - m_new)
            vb = v_ref[0, pl.ds(j * tq, tq), :].astype(BF16)
            return (m_new, alpha * l + jnp.sum(p, axis=-1, keepdims=True),
                    alpha * acc + _dot(p.astype(BF16), vb))

        init = (jnp.full((tq, 1), NEG, F32), jnp.zeros((tq, 1), F32), jnp.zeros((tq, 2 * HEAD_DIM), F32))
        carry = lax.fori_loop(0, i, lambda j, cr: update(j, scores(j), cr), init)
        m, l, acc = update(i, jnp.where(causal, scores(i), NEG), carry)
        outs.append(acc / l)
    o_ref[0] = jnp.where((lane // HEAD_DIM) == 0, outs[0], outs[1])


def _fox_prompt(q, k, v, cum_bht, tq):
    b, t, dm = q.shape
    hp = dm // (2 * HEAD_DIM)
    nk = t // tq
    f = cum_bht.reshape(b, hp, 2, nk, tq).transpose(0, 1, 3, 2, 4)
    kv_spec = pl.BlockSpec((1, t, 2 * HEAD_DIM), lambda i, h, j: (i, 0, h))
    return pl.pallas_call(
        functools.partial(_fox_kernel, tq=tq),
        grid=(b, hp, nk),
        in_specs=[pl.BlockSpec((1, tq, 2 * HEAD_DIM), lambda i, h, j: (i, j, h)), kv_spec, kv_spec,
                  pl.BlockSpec((1, 1, nk, 2, tq), lambda i, h, j: (i, h, 0, 0, 0))],
        out_specs=pl.BlockSpec((1, tq, 2 * HEAD_DIM), lambda i, h, j: (i, j, h)),
        out_shape=jax.ShapeDtypeStruct((b, t, dm), F32),
        compiler_params=_params(("parallel", "parallel", "arbitrary")),
        name="fox_prompt",
    )(q, k, v, f)


def _band_kernel(q_ref, k_ref, v_ref, bias_ref, o_ref, *, tq):
    i = pl.program_id(2)
    q = q_ref[0] * ATTN_SCALE
    lane = _iota((1, 2 * HEAD_DIM), 1)
    n_piece = BAND // tq + 1
    outs = []
    for hh in range(2):
        qm = jnp.where((lane // HEAD_DIM) == hh, q, 0.0).astype(BF16)
        ss, vs = [], []
        for p in range(n_piece):
            blk = i - (n_piece - 1) + p
            start = jnp.maximum(blk, 0) * tq
            kb = k_ref[0, pl.ds(start, tq), :].astype(BF16)
            s = _dot_nt(qm, kb) + bias_ref[0, hh, :, p * tq:(p + 1) * tq]
            ss.append(jnp.where(blk >= 0, s, NEG))
            vs.append(v_ref[0, pl.ds(start, tq), :].astype(BF16))
        m = jnp.max(ss[0], axis=-1, keepdims=True)
        for s in ss[1:]:
            m = jnp.maximum(m, jnp.max(s, axis=-1, keepdims=True))
        l = jnp.zeros((tq, 1), F32)
        acc = jnp.zeros((tq, 2 * HEAD_DIM), F32)
        for s, vb in zip(ss, vs):
            p_ = jnp.exp(s - m)
            l = l + jnp.sum(p_, axis=-1, keepdims=True)
            acc = acc + _dot(p_.astype(BF16), vb)
        outs.append(acc / l)
    o_ref[0] = jnp.where((lane // HEAD_DIM) == 0, outs[0], outs[1])


def _band_bias_prompt(rel_bias, tq):
    nk = BAND + tq
    qi = jnp.arange(tq)[:, None]
    kj = jnp.arange(nk)[None, :]
    rel = qi + BAND - kj
    bias = rel_bias[:, jnp.clip(rel, -REL_CLIP, REL_CLIP) + REL_CLIP].astype(F32)
    qc = qi // CHUNK
    kc = kj // CHUNK
    valid = (kc >= qc) & (kc <= qc + BAND // CHUNK)
    return jnp.where(valid[None], bias, NEG)


def _band_prompt(q, k, v, rel_bias, tq):
    b, t, dm = q.shape
    hp = dm // (2 * HEAD_DIM)
    bias = _band_bias_prompt(rel_bias, tq).reshape(hp, 2, tq, BAND + tq)
    kv_spec = pl.BlockSpec((1, t, 2 * HEAD_DIM), lambda i, h, j: (i, 0, h))
    return pl.pallas_call(
        functools.partial(_band_kernel, tq=tq),
        grid=(b, hp, t // tq),
        in_specs=[pl.BlockSpec((1, tq, 2 * HEAD_DIM), lambda i, h, j: (i, j, h)), kv_spec, kv_spec,
                  pl.BlockSpec((1, 2, tq, BAND + tq), lambda i, h, j: (h, 0, 0, 0))],
        out_specs=pl.BlockSpec((1, tq, 2 * HEAD_DIM), lambda i, h, j: (i, j, h)),
        out_shape=jax.ShapeDtypeStruct((b, t, dm), F32),
        compiler_params=_params(("parallel", "parallel", "arbitrary")),
        name="band_prompt",
    )(q, k, v, bias)


def _cached_attn_kernel(*refs, fox):
    if fox:
        q_ref, kn_ref, vn_ref, ck_ref, cv_ref, fc_ref, fn_ref, o_ref = refs
    else:
        q_ref, kn_ref, vn_ref, ck_ref, cv_ref, bc_ref, bn_ref, o_ref = refs
    tn, dm = q_ref.shape[1], q_ref.shape[2]
    nst = N_HEADS * tn
    stackmask = (_iota((nst, dm), 0) // tn) == (_iota((nst, dm), 1) // HEAD_DIM)
    q = q_ref[0] * ATTN_SCALE
    qst = jnp.where(stackmask, jnp.concatenate([q] * N_HEADS, axis=0), 0.0).astype(BF16)
    s_c = _dot_nt(qst, ck_ref[0].astype(BF16))
    s_n = _dot_nt(qst, kn_ref[0].astype(BF16))
    if fox:
        expand = ((_iota((nst, N_HEADS), 0) // tn) == _iota((nst, N_HEADS), 1)).astype(F32)
        fc = fc_ref[0]
        upper = (_iota((tn, tn), 0) <= _iota((tn, tn), 1)).astype(F32)
        fnew = fc[:, fc.shape[1] - 1:] + _dot(fn_ref[0], upper, HI)
        s_c = s_c - _dot(expand, fc, HI)
        s_n = s_n - _dot(expand, fnew, HI)
        tq = _iota((nst, tn), 0) % tn
        s_n = jnp.where(_iota((nst, tn), 1) <= tq, s_n, NEG)
    else:
        s_c = s_c + bc_ref[...]
        s_n = s_n + bn_ref[...]
    m = jnp.maximum(jnp.max(s_c, axis=-1, keepdims=True), jnp.max(s_n, axis=-1, keepdims=True))
    p_c = jnp.exp(s_c - m)
    p_n = jnp.exp(s_n - m)
    l = jnp.sum(p_c, axis=-1, keepdims=True) + jnp.sum(p_n, axis=-1, keepdims=True)
    ov = _dot(p_c.astype(BF16), cv_ref[0].astype(BF16)) + _dot(p_n.astype(BF16), vn_ref[0].astype(BF16))
    ov = jnp.where(stackmask, ov / l, 0.0)
    out = ov[0:tn]
    for hh in range(1, N_HEADS):
        out = out + ov[hh * tn:(hh + 1) * tn]
    o_ref[0] = out


def _cached_attn(q, kn, vn, ck, cv, extra_c, extra_n, fox):
    b, tn, dm = q.shape
    p = ck.shape[1]
    new_spec = pl.BlockSpec((1, tn, dm), lambda i: (i, 0, 0))
    cache_spec = pl.BlockSpec((1, p, dm), lambda i: (i, 0, 0))
    if fox:
        ex_specs = [pl.BlockSpec((1, N_HEADS, p), lambda i: (i, 0, 0)),
                    pl.BlockSpec((1, N_HEADS, tn), lambda i: (i, 0, 0))]
    else:
        ex_specs = [_const_spec(extra_c.shape), _const_spec(extra_n.shape)]
    return pl.pallas_call(
        functools.partial(_cached_attn_kernel, fox=fox),
        grid=(b,),
        in_specs=[new_spec, new_spec, new_spec, cache_spec, cache_spec] + ex_specs,
        out_specs=new_spec,
        out_shape=jax.ShapeDtypeStruct((b, tn, dm), F32),
        compiler_params=_params(("parallel",)),
        name="fox_sample" if fox else "band_sample",
    )(q, kn, vn, ck, cv, extra_c, extra_n)


def _to_bd(s):
    b = s.shape[0]
    s5 = s.reshape(b, N_GROUPS, HEADS_PER_GROUP, HEAD_DIM, HEAD_DIM)
    eye = jnp.eye(HEADS_PER_GROUP, dtype=s.dtype)
    bd = s5[:, :, :, :, None, :] * eye[None, None, :, None, :, None]
    return bd.reshape(b, N_GROUPS, GROUP, GROUP)


def _from_bd(bd):
    b = bd.shape[0]
    s6 = bd.reshape(b, N_GROUPS, HEADS_PER_GROUP, HEAD_DIM, HEADS_PER_GROUP, HEAD_DIM)
    d = jnp.diagonal(s6, axis1=2, axis2=4)
    return jnp.moveaxis(d, -1, 2).reshape(b, N_HEADS, HEAD_DIM, HEAD_DIM)


def _trunk(x, mods, P, cache, tm, tq, L):
    b, t, d = x.shape
    dm = D_MIX
    sample = cache is not None
    new = {}
    za, q, k, v, logf = _proj(x, mods[0], P["norm_mix_g"][0], P["ab_w"], (P["a_cols"], dm, dm, dm), tm,
                              wf=P["ab_wf"], bf=P["fox_bf"])
    ac = P["a_cols"]
    if sample:
        nb, tn = cache["nb"], cache["tn"]
        za_b = za.reshape(nb, tn, ac)
        shift0 = cache["rwkv_shift"][0]
        s0 = _to_bd(jnp.swapaxes(cache["rwkv"][0], -1, -2))
    else:
        nb, tn = b, t
        za_b = za
        shift0 = jnp.zeros((nb, ac), F32)
        s0 = jnp.zeros((nb, N_GROUPS, GROUP, GROUP), F32)
    ya, st = _rwkv(za_b, shift0, s0, P["rwkv"], L)
    new["rwkv"] = jnp.swapaxes(_from_bd(st), -1, -2)[None]
    new["rwkv_shift"] = za_b[:, -1][None]
    qb, kb, vb = (a.reshape(nb, tn, dm) for a in (q, k, v))
    logf_b = logf.reshape(nb, tn, N_HEADS)
    logf_t = jnp.swapaxes(logf_b, 1, 2)
    if sample:
        ck = cache["fox_k"][0].reshape(nb, -1, dm)
        cv = cache["fox_v"][0].reshape(nb, -1, dm)
        fc = _cumsum_time(jnp.swapaxes(cache["fox_logf"][0], 1, 2))
        yb = _cached_attn(qb, kb, vb, ck, cv, fc, logf_t, fox=True)
    else:
        yb = _fox_prompt(qb, kb, vb, _cumsum_time(logf_t), tq)
    new["fox_k"] = kb.reshape(1, nb, tn, N_HEADS, HEAD_DIM)
    new["fox_v"] = vb.reshape(1, nb, tn, N_HEADS, HEAD_DIM)
    new["fox_logf"] = logf_b[None]
    fix = None
    if sample:
        buf = cache["ffn_conv"][0]
        zero = jnp.zeros((nb, tn - 2, buf.shape[-1]), F32)
        fix = (jnp.concatenate([buf[:, 1:2], buf[:, 0:1] * 0, zero], axis=1).reshape(1, nb * tn, -1),
               jnp.concatenate([buf, zero], axis=1).reshape(1, nb * tn, -1))
    res = _tail(x, ya.reshape(b, t, dm), yb.reshape(b, t, dm), mods[0], mods[1], P["ab_wo"],
                P["norm_ffn_g"][0], P["wup"][0], P["conv_w"][0], P["conv_b"][0], P["wdn"][0],
                P["final_g"], False, tm, fix=fix, period=tn)
    x = res[0]
    conv0 = res[1].reshape(nb, tn, -1)[:, tn - 2:] if sample else res[1]
    q, k, v, zq, zf, zi, zg = _proj(x, mods[2], P["norm_mix_g"][1], P["cd_w"], (dm,) * 7, tm)
    qb, kb, vb, zq, zf, zi, zg = (a.reshape(nb, tn, dm) for a in (q, k, v, zq, zf, zi, zg))
    if sample:
        ck = cache["chunk_k"][0].reshape(nb, -1, dm)
        cv = cache["chunk_v"][0].reshape(nb, -1, dm)
        yc = _cached_attn(qb, kb, vb, ck, cv, P["band_bias_c"], P["band_bias_n"], fox=False)
        new["chunk_k"] = kb.reshape(1, nb, tn, N_HEADS, HEAD_DIM)
        new["chunk_v"] = vb.reshape(1, nb, tn, N_HEADS, HEAD_DIM)
        s0 = _to_bd(cache["hgrn"][0])
    else:
        yc = _band_prompt(qb, kb, vb, P["rel_bias"], 4 * CHUNK)
        keep = min(BAND, tn)
        new["chunk_k"] = kb[:, tn - keep:].reshape(1, nb, keep, N_HEADS, HEAD_DIM)
        new["chunk_v"] = vb[:, tn - keep:].reshape(1, nb, keep, N_HEADS, HEAD_DIM)
        s0 = jnp.zeros((nb, N_GROUPS, GROUP, GROUP), F32)
    yd, st = _hgrn(zq, zf, zi, zg, s0, P["hgrn_lb"], P["hgrn_norm_g"], L)
    new["hgrn"] = _from_bd(st)[None]
    if sample:
        buf = cache["ffn_conv"][1]
        fix = (jnp.concatenate([buf[:, 1:2], buf[:, 0:1] * 0, zero], axis=1).reshape(1, nb * tn, -1),
               jnp.concatenate([buf, zero], axis=1).reshape(1, nb * tn, -1))
    res = _tail(x, yc.reshape(b, t, dm), yd.reshape(b, t, dm), mods[2], mods[3], P["cd_wo"],
                P["norm_ffn_g"][1], P["wup"][1], P["conv_w"][1], P["conv_b"][1], P["wdn"][1],
                P["final_g"], True, tm, fix=fix, period=tn)
    conv1 = res[1].reshape(nb, tn, -1)[:, tn - 2:] if sample else res[1]
    new["ffn_conv"] = jnp.stack([conv0, conv1])
    y = res[0].reshape(nb, tn, d)
    return y, new


def kernel(x_prompt, x_sample, c_prompt, c_sample, cache_fox_k, cache_fox_v, cache_fox_logf, state_rwkv,
           state_rwkv_shift, cache_chunk_k, cache_chunk_v, state_hgrn, state_ffn_conv, ada_w, ada_b,
           norm_mix_g, norm_ffn_g, ab_w_in, rwkv_mu, rwkv_w0, rwkv_w2, rwkv_a0, rwkv_a2, rwkv_g2,
           rwkv_k_k, rwkv_k_a, rwkv_r_k, rwkv_lnx_g, rwkv_lnx_b, fox_b_f, ab_w_out, cd_w_in,
           chunk_rel_bias, hgrn_lb_table, hgrn_norm_g, cd_w_out, ffn_w_up, ffn_conv_w, ffn_conv_b,
           ffn_w_down, final_norm_g):
    bp, tp, d = x_prompt.shape
    bs, ts, _ = x_sample.shape
    depth = ada_w.shape[0]
    dm = D_MIX
    a_cols = rwkv_mu.shape[1]
    n_lw, n_la, n_lg = rwkv_w2.shape[1], rwkv_a2.shape[1], rwkv_g2.shape[1]

    c_all = jnp.concatenate([c_prompt, c_sample], axis=0)
    mods = _adaln(c_all, ada_w.reshape(depth * 2, d, 3 * d), ada_b.reshape(depth * 2, 3 * d))
    mods_p = [mods[i, :bp].reshape(bp, 1, 3 * d) for i in range(depth * 2)]
    mods_s = [jnp.repeat(mods[i, bp:], ts, axis=0).reshape(1, bs * ts, 3 * d) for i in range(depth * 2)]

    wl = jnp.zeros((n_lw + n_la + n_lg, 3 * dm), F32)
    wl = wl.at[:n_lw, 0:dm].set(rwkv_w2[0])
    wl = wl.at[n_lw:n_lw + n_la, dm:2 * dm].set(rwkv_a2[0])
    wl = wl.at[n_lw + n_la:, 2 * dm:].set(rwkv_g2[0])
    sm = jax.nn.softmax(hgrn_lb_table.astype(F32), axis=0)
    lb = (jnp.cumsum(sm, axis=0) - sm[0])[1]
    wf = jnp.zeros((d, 128), F32).at[:, :N_HEADS].set(ab_w_in[0][:, a_cols + 3 * dm:]).astype(BF16)
    bf = jnp.zeros((1, 128), F32).at[0, :N_HEADS].set(fox_b_f[0])
    p_c = cache_chunk_k.shape[2]
    rel = jnp.arange(ts)[:, None] + p_c - jnp.arange(p_c + ts)[None, :]
    bias_s = chunk_rel_bias[0][:, jnp.clip(rel, -REL_CLIP, REL_CLIP) + REL_CLIP].astype(F32)
    bias_s = bias_s.reshape(N_HEADS * ts, p_c + ts)

    P = {
        "a_cols": a_cols,
        "norm_mix_g": norm_mix_g, "norm_ffn_g": norm_ffn_g, "final_g": final_norm_g,
        "ab_w": ab_w_in[0][:, :a_cols + 3 * dm].astype(BF16), "ab_wf": wf, "fox_bf": bf,
        "ab_wo": ab_w_out[0].astype(BF16),
        "cd_w": cd_w_in[0].astype(BF16), "cd_wo": cd_w_out[0].astype(BF16),
        "wup": ffn_w_up.astype(BF16), "wdn": ffn_w_down.astype(BF16),
        "conv_w": ffn_conv_w, "conv_b": ffn_conv_b,
        "rwkv": {"mu": rwkv_mu[0], "w0": rwkv_w0[0], "a0": rwkv_a0[0], "wl": wl, "k_k": rwkv_k_k[0],
                 "k_a": rwkv_k_a[0], "r_k": rwkv_r_k[0], "lnx_g": rwkv_lnx_g[0], "lnx_b": rwkv_lnx_b[0]},
        "rel_bias": chunk_rel_bias[0], "band_bias_c": bias_s[:, :p_c], "band_bias_n": bias_s[:, p_c:],
        "hgrn_lb": lb, "hgrn_norm_g": hgrn_norm_g[0],
    }
    cache = {"nb": bs, "tn": ts, "fox_k": cache_fox_k, "fox_v": cache_fox_v, "fox_logf": cache_fox_logf,
             "rwkv": state_rwkv, "rwkv_shift": state_rwkv_shift, "chunk_k": cache_chunk_k,
             "chunk_v": cache_chunk_v, "hgrn": state_hgrn, "ffn_conv": state_ffn_conv}

    y_p, sp = _trunk(x_prompt, mods_p, P, None, tm=512, tq=512, L=CHUNK)
    y_s, ss = _trunk(x_sample.reshape(1, bs * ts, d), mods_s, P, cache, tm=bs * ts, tq=None, L=ts)
    names = ("fox_k", "fox_v", "fox_logf", "rwkv", "rwkv_shift", "chunk_k", "chunk_v", "hgrn", "ffn_conv")
    return (y_p, y_s) + tuple(sp[n] for n in names) + tuple(ss[n] for n in names)
```

```python
import functools
import math

import jax
import jax.numpy as jnp
from jax import lax
from jax.experimental import pallas as pl
from jax.experimental.pallas import tpu as pltpu

F32 = jnp.float32
BF16 = jnp.bfloat16
HI = lax.Precision.HIGHEST

HEAD_DIM = 64
N_HEADS = 8
D_MIX = N_HEADS * HEAD_DIM
GROUP = 256
N_GROUPS = D_MIX // GROUP
HEADS_PER_GROUP = GROUP // HEAD_DIM
SUB = 16
CHUNK = 64
BAND = 512
REL_CLIP = 128
RMS_EPS = 1e-6
GN_EPS = 64e-5
ATTN_SCALE = HEAD_DIM ** -0.5
NEG = -1e30
RW_SCORE = 1
RW_STATE = 1
RW_VALUE = 1
RW_SOLVE_P = 1
RW_SOLVE_U = 2
RW_SOLVE_SQ = 1
RW_UPDATE = 1
HG_SCORE = 1
HG_STATE = 1
HG_VALUE = 1
HG_UPDATE = 1
VMEM_LIMIT = 56 * 1024 * 1024


def _dot(a, b, prec=None):
    return jnp.dot(a, b, preferred_element_type=F32, precision=prec)


def _dot_nt(a, b, prec=None):
    return lax.dot_general(a, b, (((1,), (1,)), ((), ())), preferred_element_type=F32, precision=prec)


def _dot_tn(a, b, prec=None):
    return lax.dot_general(a, b, (((0,), (0,)), ((), ())), preferred_element_type=F32, precision=prec)


_NN = ((1,), (0,))
_NT = ((1,), (1,))
_TN = ((0,), (0,))


def _sp(x, n):
    hi = x.astype(BF16)
    if n == 1:
        return (hi,)
    r = x - hi.astype(F32)
    mid = r.astype(BF16)
    if n == 2:
        return (hi, mid)
    return (hi, mid, (r - mid.astype(F32)).astype(BF16))


def _mm(a, b, dims=_NN):
    n = max(len(a), len(b))
    out = None
    for i, ai in enumerate(a):
        for j, bj in enumerate(b):
            if i + j < n:
                d = lax.dot_general(ai, bj, (dims, ((), ())), preferred_element_type=F32)
                out = d if out is None else out + d
    return out


def _head_sum(x, bo_group):
    return jnp.concatenate(
        [_mm(_sp(x[:, gi * GROUP:(gi + 1) * GROUP], 2), (bo_group,)) for gi in range(N_GROUPS)], axis=1)


def _sigmoid(x):
    return jax.nn.sigmoid(x)


def _silu(x):
    return x * jax.nn.sigmoid(x)


def _softplus(x):
    return jnp.maximum(x, 0.0) + jnp.log1p(jnp.exp(-jnp.abs(x)))


def _log_sigmoid(x):
    return -_softplus(-x)


def _iota(shape, dim):
    return lax.broadcasted_iota(jnp.int32, shape, dim)


def _const_spec(shape):
    nd = len(shape)
    return pl.BlockSpec(shape, lambda *_: (0,) * nd, pipeline_mode=pl.Buffered(1))


def _params(sem):
    return pltpu.CompilerParams(dimension_semantics=sem, vmem_limit_bytes=VMEM_LIMIT)


def _norm_mod(x, g, shift, scale):
    xn = x * lax.rsqrt(jnp.mean(x * x, axis=-1, keepdims=True) + RMS_EPS) * g
    return xn * (1.0 + scale) + shift


def _block_ones(n, blk):
    i = jnp.arange(n) // blk
    return (i[:, None] == i[None, :]).astype(BF16)


def _bd(x, rows_per_head, cols_per_head):
    n = HEADS_PER_GROUP
    t = jnp.concatenate([x] * n, axis=0)
    r = _iota(t.shape, 0) // rows_per_head
    c = _iota(t.shape, 1) // cols_per_head
    return jnp.where(r == c, t, 0.0)


def _adaln_kernel(c_ref, w_ref, b_ref, o_ref):
    o_ref[0] = _dot(_silu(c_ref[...]), w_ref[0], HI) + b_ref[0]


def _adaln(c_all, ada_w, ada_b):
    n, d, d3 = ada_w.shape
    bt = c_all.shape[0]
    nt = d3 // d
    return pl.pallas_call(
        _adaln_kernel,
        grid=(n, nt),
        in_specs=[pl.BlockSpec((bt, d), lambda i, j: (0, 0)),
                  pl.BlockSpec((1, d, d), lambda i, j: (i, 0, j)),
                  pl.BlockSpec((1, 1, d), lambda i, j: (i, 0, j))],
        out_specs=pl.BlockSpec((1, bt, d), lambda i, j: (i, 0, j)),
        out_shape=jax.ShapeDtypeStruct((n, bt, d3), F32),
        compiler_params=_params(("parallel", "parallel")),
        name="adaln",
    )(c_all, ada_w, ada_b.reshape(n, 1, d3))


def _proj_kernel(*refs, splits, fox):
    x_ref, mod_ref, g_ref, w_ref = refs[:4]
    pos = 4
    if fox:
        wf_ref, bf_ref = refs[4:6]
        pos = 6
    outs = refs[pos:]
    d = x_ref.shape[-1]
    mod = mod_ref[0]
    h = _norm_mod(x_ref[0], g_ref[...], mod[:, :d], mod[:, d:2 * d]).astype(BF16)
    off = 0
    for o_ref, n in zip(outs, splits):
        o_ref[0] = _dot(h, w_ref[:, off:off + n])
        off += n
    if fox:
        zf = _dot(h, wf_ref[...]) + bf_ref[...]
        outs[len(splits)][0] = _log_sigmoid(zf)[:, :N_HEADS]


def _proj(x, mod, g, w_bf, splits, tm, wf=None, bf=None):
    b, t, d = x.shape
    r = mod.shape[1]
    fox = wf is not None
    mod_spec = (pl.BlockSpec((1, 1, 3 * d), lambda i, j: (i, 0, 0)) if r == 1
                else pl.BlockSpec((1, tm, 3 * d), lambda i, j: (i, j, 0)))
    in_specs = [pl.BlockSpec((1, tm, d), lambda i, j: (i, j, 0)), mod_spec,
                _const_spec((1, d)), _const_spec(w_bf.shape)]
    args = [x, mod, g.reshape(1, d), w_bf]
    widths = list(splits)
    if fox:
        in_specs += [_const_spec(wf.shape), _const_spec(bf.shape)]
        args += [wf, bf]
        widths.append(N_HEADS)
    return pl.pallas_call(
        functools.partial(_proj_kernel, splits=tuple(splits), fox=fox),
        grid=(b, t // tm),
        in_specs=in_specs,
        out_specs=[pl.BlockSpec((1, tm, n), lambda i, j: (i, j, 0)) for n in widths],
        out_shape=[jax.ShapeDtypeStruct((b, t, n), F32) for n in widths],
        compiler_params=_params(("parallel", "parallel")),
        name="proj",
    )(*args)


def _tail_kernel(*refs, sample, final, period, cb):
    (x_ref, ya_ref, yb_ref, m1_ref, m2_ref, wo_ref, g_ref, wup_ref, cw_ref, cbias_ref,
     wdn_ref, fg_ref) = refs[:12]
    if sample:
        fix1_ref, fix2_ref, o_ref, u_ref = refs[12:]
    else:
        o_ref, cs_ref, tail_sc = refs[12:]
    tm, d = x_ref.shape[1], x_ref.shape[2]
    ff = wdn_ref.shape[0]
    dm = ya_ref.shape[2]
    m1 = m1_ref[0]
    m2 = m2_ref[0]
    ymix = (_dot(ya_ref[0].astype(BF16), wo_ref[0:dm, :])
            + _dot(yb_ref[0].astype(BF16), wo_ref[dm:2 * dm, :]))
    x1 = x_ref[0] + m1[:, 2 * d:3 * d] * ymix
    h = _norm_mod(x1, g_ref[...], m2[:, :d], m2[:, d:2 * d]).astype(BF16)
    rows = _iota((tm, 1), 0)
    if sample:
        tpos = rows & (period - 1)
    else:
        t = pl.program_id(1)

        @pl.when(t == 0)
        def _():
            tail_sc[...] = jnp.zeros_like(tail_sc)

    acc = jnp.zeros((tm, d), F32)
    for j in range(ff // cb):
        halves = []
        for half in range(2):
            c0 = half * ff + j * cb
            u = _dot(h, wup_ref[:, c0:c0 + cb])
            r1 = pltpu.roll(u, 1, 0)
            r2 = pltpu.roll(u, 2, 0)
            if sample:
                u1 = jnp.where(tpos == 0, fix1_ref[0, :, c0:c0 + cb], r1)
                u2 = jnp.where(tpos < 2, fix2_ref[0, :, c0:c0 + cb], r2)
                u_ref[0, :, c0:c0 + cb] = u
            else:
                t6 = tail_sc[6:7, c0:c0 + cb]
                t7 = tail_sc[7:8, c0:c0 + cb]
                u1 = jnp.where(rows == 0, t7, r1)
                u2 = jnp.where(rows == 0, t6, jnp.where(rows == 1, t7, r2))
                tail_sc[:, c0:c0 + cb] = u[tm - 8:tm, :]
            cw = cw_ref[:, c0:c0 + cb]
            halves.append(cbias_ref[:, c0:c0 + cb] + cw[0:1] * u2 + cw[1:2] * u1 + cw[2:3] * u)
        gact = (_silu(halves[0]) * halves[1]).astype(BF16)
        acc = acc + _dot(gact, wdn_ref[j * cb:(j + 1) * cb, :])
    out = x1 + m2[:, 2 * d:3 * d] * acc
    if final:
        out = out * lax.rsqrt(jnp.mean(out * out, axis=-1, keepdims=True) + RMS_EPS) * fg_ref[...]
    o_ref[0] = out
    if not sample:
        @pl.when(t == pl.num_programs(1) - 1)
        def _():
            cs_ref[0] = tail_sc[6:8, :]


def _tail(x, ya, yb, mod1, mod2, wo_bf, g, wup_bf, conv_w, conv_b, wdn_bf, fg, final, tm,
          fix=None, period=None):
    b, t, d = x.shape
    ff = wdn_bf.shape[0]
    dm = ya.shape[2]
    sample = fix is not None
    r = mod1.shape[1]
    mod_spec = (pl.BlockSpec((1, 1, 3 * d), lambda i, j: (i, 0, 0)) if r == 1
                else pl.BlockSpec((1, tm, 3 * d), lambda i, j: (i, j, 0)))
    row_spec = lambda n: pl.BlockSpec((1, tm, n), lambda i, j: (i, j, 0))
    in_specs = [row_spec(d), row_spec(dm), row_spec(dm), mod_spec, mod_spec,
                _const_spec(wo_bf.shape), _const_spec((1, d)), _const_spec(wup_bf.shape),
                _const_spec(conv_w.shape), _const_spec((1, 2 * ff)), _const_spec(wdn_bf.shape),
                _const_spec((1, d))]
    args = [x, ya, yb, mod1, mod2, wo_bf, g.reshape(1, d), wup_bf, conv_w, conv_b.reshape(1, 2 * ff),
            wdn_bf, fg.reshape(1, d)]
    if sample:
        in_specs += [row_spec(2 * ff), row_spec(2 * ff)]
        args += list(fix)
        out_specs = [row_spec(d), row_spec(2 * ff)]
        out_shape = [jax.ShapeDtypeStruct((b, t, d), F32), jax.ShapeDtypeStruct((b, t, 2 * ff), F32)]
        scratch = []
    else:
        out_specs = [row_spec(d), pl.BlockSpec((1, 2, 2 * ff), lambda i, j: (i, 0, 0))]
        out_shape = [jax.ShapeDtypeStruct((b, t, d), F32), jax.ShapeDtypeStruct((b, 2, 2 * ff), F32)]
        scratch = [pltpu.VMEM((8, 2 * ff), F32)]
    return pl.pallas_call(
        functools.partial(_tail_kernel, sample=sample, final=final, period=period, cb=256),
        grid=(b, t // tm),
        in_specs=in_specs,
        out_specs=out_specs,
        out_shape=out_shape,
        scratch_shapes=scratch,
        compiler_params=_params(("parallel", "arbitrary")),
        name="tail",
    )(*args)


def _rwkv_kernel(z_ref, shift0_ref, s0_ref, mu_ref, w0_ref, a0_ref, wl_ref, kk_ref, ka_ref, rk_ref,
                 lng_ref, lnb_ref, bo_ref, y_ref, st_ref, prev_sc, st_sc, *, L):
    t = pl.program_id(1)

    @pl.when(t == 0)
    def _():
        prev_sc[...] = shift0_ref[0]
        st_sc[...] = s0_ref[0]

    z = z_ref[0]
    rows = _iota((L, 1), 0)
    z_prev = jnp.where(rows == 0, prev_sc[...], pltpu.roll(z, 1, 0))
    prev_sc[...] = z[L - 1:L, :]
    zs = z + mu_ref[...] * (z_prev - z)
    dm = D_MIX
    r = zs[:, 0:dm]
    k = zs[:, dm:2 * dm]
    v = zs[:, 2 * dm:3 * dm]
    lr = zs[:, 3 * dm:]
    nl = lr.shape[1]
    lane = _iota((1, nl), 1)
    act = jnp.where(lane < 64, jnp.tanh(lr), jnp.where(lane < 128, lr, _sigmoid(lr)))
    lo = _mm(_sp(act, 2), (wl_ref[0], wl_ref[1]))
    w = -_softplus(-(w0_ref[...] + lo[:, 0:dm])) - 0.5
    lw = -jnp.exp(w)
    a = _sigmoid(a0_ref[...] + lo[:, dm:2 * dm])
    g = lo[:, 2 * dm:3 * dm]
    bo = bo_ref[...]
    kk = k * kk_ref[...]
    k2 = k * (1.0 + (a - 1.0) * ka_ref[...])
    sums = _head_sum(jnp.concatenate([kk * kk, r * k2 * rk_ref[...]], axis=0), bo)
    kk = kk / jnp.maximum(jnp.sqrt(sums[:L]), 1e-12)
    bonus = sums[L:]
    am = -kk
    bm = kk * a

    tri = (_iota((L, L), 1) <= _iota((L, L), 0)).astype(BF16)
    c = _mm((tri,), _sp(lw, 3))
    c_last = c[L - 1:L, :]
    a_t = am * jnp.exp(c - lw)
    r_t = r * jnp.exp(c)
    g_inv = jnp.exp(-c)
    k_t = k2 * g_inv
    b_t = bm * g_inv
    g_end = jnp.exp(c_last - c)
    k_end = k2 * g_end
    b_end = bm * g_end
    g_last = jnp.exp(c_last)

    cw = HEADS_PER_GROUP * L
    e_tile = ((_iota((L, cw), 1) % L) == _iota((L, cw), 0)).astype(BF16)
    tmask = (_iota((GROUP, cw), 0) // HEAD_DIM) == (_iota((GROUP, cw), 1) // L)
    col_s = _iota((L, cw), 1) % L
    row_t = _iota((L, cw), 0)
    strict = col_s < row_t
    incl = col_s <= row_t
    bdmask = (_iota((GROUP, GROUP), 0) // HEAD_DIM) == (_iota((GROUP, GROUP), 1) // HEAD_DIM)
    n_dbl = int(math.log2(L))

    def bdt(x):
        return tuple(jnp.where(tmask, _mm((p,), (e_tile,), _TN), 0.0).astype(BF16) for p in _sp(x, RW_SCORE))

    def bd(x, n, cols_per_head=HEAD_DIM):
        return tuple(_bd(p, L, cols_per_head) for p in _sp(x, n))

    ys = []
    for gi in range(N_GROUPS):
        sl = slice(gi * GROUP, (gi + 1) * GROUP)
        s = st_sc[gi]
        ar = _sp(jnp.concatenate([a_t[:, sl], r_t[:, sl]], axis=0), RW_SCORE)
        arb = _mm(ar, bdt(b_t[:, sl]))
        ark = _mm(ar, bdt(k_t[:, sl]))
        ars = _mm(ar, _sp(s, RW_STATE), _NT)
        n_ab = jnp.where(strict, arb[:L], 0.0)
        m_rb = jnp.where(incl, arb[L:], 0.0)
        n_ak = jnp.where(strict, ark[:L], 0.0)
        m_rk = jnp.where(incl, ark[L:], 0.0)
        vg = v[:, sl]
        bd_v = bd(vg, RW_VALUE)
        u = ars[:L] + _mm(_sp(n_ak, RW_VALUE), bd_v)
        p = n_ab
        for i in range(n_dbl):
            u = u + _mm(_sp(p, RW_SOLVE_P), bd(u, RW_SOLVE_U))
            if i + 1 < n_dbl:
                p = _mm(_sp(p, RW_SOLVE_SQ), bd(p, RW_SOLVE_SQ, L))
        ys.append(ars[L:] + _mm(_sp(m_rk, RW_VALUE), bd_v) + _mm(_sp(m_rb, RW_VALUE), bd(u, RW_VALUE)))
        kb = _sp(jnp.concatenate([k_end[:, sl], b_end[:, sl]], axis=0), RW_UPDATE)
        vu = _sp(jnp.concatenate([vg, u], axis=0), RW_UPDATE)
        st_sc[gi] = s * g_last[:, sl] + jnp.where(bdmask, _mm(vu, kb, _TN), 0.0)

    y = jnp.concatenate(ys, axis=1)
    inv = 1.0 / HEAD_DIM
    mean = _head_sum(y, bo) * inv
    dlt = y - mean
    var = _head_sum(dlt * dlt, bo) * inv
    yn = dlt * lax.rsqrt(var + GN_EPS) * lng_ref[...] + lnb_ref[...]
    y_ref[0] = (yn + bonus * v) * g

    @pl.when(t == pl.num_programs(1) - 1)
    def _():
        st_ref[0] = st_sc[...]


def _rwkv(za, shift0, s0_bd, pr, L):
    b, t, ac = za.shape
    dm = D_MIX
    row = lambda a: a.reshape(1, -1)
    consts = [row(pr["mu"]), row(pr["w0"]), row(pr["a0"]), pr["wl"], row(pr["k_k"]), row(pr["k_a"]),
              row(pr["r_k"]), row(pr["lnx_g"]), row(pr["lnx_b"]), _block_ones(GROUP, HEAD_DIM)]
    return pl.pallas_call(
        functools.partial(_rwkv_kernel, L=L),
        grid=(b, t // L),
        in_specs=[pl.BlockSpec((1, L, ac), lambda i, j: (i, j, 0)),
                  pl.BlockSpec((1, 1, ac), lambda i, j: (i, 0, 0)),
                  pl.BlockSpec((1, N_GROUPS, GROUP, GROUP), lambda i, j: (i, 0, 0, 0))]
                 + [_const_spec(c.shape) for c in consts],
        out_specs=[pl.BlockSpec((1, L, dm), lambda i, j: (i, j, 0)),
                   pl.BlockSpec((1, N_GROUPS, GROUP, GROUP), lambda i, j: (i, 0, 0, 0))],
        out_shape=[jax.ShapeDtypeStruct((b, t, dm), F32),
                   jax.ShapeDtypeStruct((b, N_GROUPS, GROUP, GROUP), F32)],
        scratch_shapes=[pltpu.VMEM((1, ac), F32), pltpu.VMEM((N_GROUPS, GROUP, GROUP), F32)],
        compiler_params=_params(("parallel", "arbitrary")),
        name="rwkv7",
    )(za, shift0.reshape(b, 1, ac), s0_bd, *consts)


def _hgrn_kernel(zq_ref, zf_ref, zi_ref, zg_ref, s0_ref, lb_ref, ng_ref, bo_ref, y_ref, st_ref, st_sc, *, L):
    t = pl.program_id(1)

    @pl.when(t == 0)
    def _():
        st_sc[...] = s0_ref[0]

    dm = D_MIX
    q = _silu(zq_ref[0])
    xf = zf_ref[0]
    v = zi_ref[0]
    lb = lb_ref[...]
    la = jnp.log(lb)
    lc = jnp.log1p(-lb) + _log_sigmoid(xf)
    logf = jnp.maximum(la, lc) + jnp.log1p(jnp.exp(-jnp.abs(la - lc)))
    k = (1.0 - lb) * _sigmoid(-xf)
    bo = bo_ref[...]

    tri = (_iota((L, L), 1) <= _iota((L, L), 0)).astype(BF16)
    b = _mm((tri,), _sp(logf, 3))
    b_last = b[L - 1:L, :]
    qe = q * jnp.exp(b)
    k_end = k * jnp.exp(b_last - b)
    g_last = jnp.exp(b_last)

    o = jnp.concatenate([_mm(_sp(qe[:, gi * GROUP:(gi + 1) * GROUP], HG_STATE), _sp(st_sc[gi], HG_STATE), _NT)
                         for gi in range(N_GROUPS)], axis=1)

    n_sub = L // SUB
    nst = N_HEADS * SUB
    stackmask = (_iota((nst, dm), 0) // SUB) == (_iota((nst, dm), 1) // HEAD_DIM)
    sub_rows = _iota((SUB, 1), 0)
    o_subs = []
    for i in range(n_sub):
        lo_, hi_ = i * SUB, (i + 1) * SUB
        bi = b[lo_:hi_]
        qi = q[lo_:hi_]
        ki = k[lo_:hi_]
        vi = v[lo_:hi_]
        xs = []
        for s in range(SUB):
            e = jnp.exp(jnp.where(sub_rows >= s, bi - bi[s:s + 1], NEG))
            xs.append(qi * e * ki[s:s + 1])
        att = _head_sum(jnp.concatenate(xs, axis=0), bo)
        oi = att[0:SUB] * vi[0:1]
        for s in range(1, SUB):
            oi = oi + att[s * SUB:(s + 1) * SUB] * vi[s:s + 1]
        if i > 0:
            bref = b[lo_ - 1:lo_]
            qh = qi * jnp.exp(bi - bref)
            kh = k[0:lo_] * jnp.exp(bref - b[0:lo_])
            qst = jnp.where(stackmask, jnp.concatenate([qh] * N_HEADS, axis=0), 0.0)
            att2 = _mm(_sp(qst, HG_SCORE), _sp(kh, HG_SCORE), _NT)
            ov = jnp.where(stackmask, _mm(_sp(att2, HG_VALUE), _sp(v[0:lo_], HG_VALUE)), 0.0)
            for hh in range(N_HEADS):
                oi = oi + ov[hh * SUB:(hh + 1) * SUB]
        o_subs.append(oi)
    o = o + (jnp.concatenate(o_subs, axis=0) if n_sub > 1 else o_subs[0])

    bdmask = (_iota((GROUP, GROUP), 0) // HEAD_DIM) == (_iota((GROUP, GROUP), 1) // HEAD_DIM)
    for gi in range(N_GROUPS):
        sl = slice(gi * GROUP, (gi + 1) * GROUP)
        upd = _mm(_sp(v[:, sl], HG_UPDATE), _sp(k_end[:, sl], HG_UPDATE), _TN)
        st_sc[gi] = st_sc[gi] * g_last[:, sl] + jnp.where(bdmask, upd, 0.0)

    ms = _head_sum(o * o, bo) * (1.0 / HEAD_DIM)
    y_ref[0] = o * lax.rsqrt(ms + RMS_EPS) * ng_ref[...] * _silu(zg_ref[0])

    @pl.when(t == pl.num_programs(1) - 1)
    def _():
        st_ref[0] = st_sc[...]


def _hgrn(zq, zf, zi, zg, s0_bd, lb, norm_g, L):
    b, t, dm = zq.shape
    consts = [lb.reshape(1, dm), norm_g.reshape(1, dm), _block_ones(GROUP, HEAD_DIM)]
    row_spec = pl.BlockSpec((1, L, dm), lambda i, j: (i, j, 0))
    st_spec = pl.BlockSpec((1, N_GROUPS, GROUP, GROUP), lambda i, j: (i, 0, 0, 0))
    return pl.pallas_call(
        functools.partial(_hgrn_kernel, L=L),
        grid=(b, t // L),
        in_specs=[row_spec] * 4 + [st_spec] + [_const_spec(c.shape) for c in consts],
        out_specs=[row_spec, st_spec],
        out_shape=[jax.ShapeDtypeStruct((b, t, dm), F32),
                   jax.ShapeDtypeStruct((b, N_GROUPS, GROUP, GROUP), F32)],
        scratch_shapes=[pltpu.VMEM((N_GROUPS, GROUP, GROUP), F32)],
        compiler_params=_params(("parallel", "arbitrary")),
        name="hgrn2",
    )(zq, zf, zi, zg, s0_bd, *consts)


def _cumsum_kernel(x_ref, o_ref):
    x = x_ref[0]
    n, w = x.shape
    nblk = n // N_HEADS
    upper = (_iota((w, w), 0) <= _iota((w, w), 1)).astype(F32)
    c = _dot(x, upper, HI)
    tot = jnp.broadcast_to(c[:, w - 1:w], (n, w))
    ri = _iota((n, n), 0)
    ci = _iota((n, n), 1)
    prior = ((ri // nblk == ci // nblk) & (ci < ri)).astype(F32)
    o_ref[0] = c + _dot(prior, tot, HI)


def _cumsum_time(logf_bht):
    b, h, t = logf_bht.shape
    w = 128
    n = h * (t // w)
    x = logf_bht.reshape(b, n, w)
    out = pl.pallas_call(
        _cumsum_kernel,
        grid=(b,),
        in_specs=[pl.BlockSpec((1, n, w), lambda i: (i, 0, 0))],
        out_specs=pl.BlockSpec((1, n, w), lambda i: (i, 0, 0)),
        out_shape=jax.ShapeDtypeStruct((b, n, w), F32),
        compiler_params=_params(("parallel",)),
        name="cumsum",
    )(x)
    return out.reshape(b, h, t)


def _fox_kernel(q_ref, k_ref, v_ref, f_ref, o_ref, *, tq):
    i = pl.program_id(2)
    q = q_ref[0] * ATTN_SCALE
    lane = _iota((1, 2 * HEAD_DIM), 1)
    causal = _iota((tq, tq), 1) <= _iota((tq, tq), 0)
    outs = []
    for hh in range(2):
        qm = jnp.where((lane // HEAD_DIM) == hh, q, 0.0).astype(BF16)

        def scores(j):
            kb = k_ref[0, pl.ds(j * tq, tq), :].astype(BF16)
            return _dot_nt(qm, kb) - f_ref[0, 0, j, hh:hh + 1, :]

        def update(j, s, carry):
            m, l, acc = carry
            m_new = jnp.maximum(m, jnp.max(s, axis=-1, keepdims=True))
            alpha = jnp.exp(m - m_new)
            p = jnp.exp(s - m_new)
            vb = v_ref[0, pl.ds(j * tq, tq), :].astype(BF16)
            return (m_new, alpha * l + jnp.sum(p, axis=-1, keepdims=True),
                    alpha * acc + _dot(p.astype(BF16), vb))

        init = (jnp.full((tq, 1), NEG, F32), jnp.zeros((tq, 1), F32), jnp.zeros((tq, 2 * HEAD_DIM), F32))
        carry = lax.fori_loop(0, i, lambda j, cr: update(j, scores(j), cr), init)
        m, l, acc = update(i, jnp.where(causal, scores(i), NEG), carry)
        outs.append(acc / l)
    o_ref[0] = jnp.where((lane // HEAD_DIM) == 0, outs[0], outs[1])


def _fox_prompt(q, k, v, cum_bht, tq):
    b, t, dm = q.shape
    hp = dm // (2 * HEAD_DIM)
    nk = t // tq
    f = cum_bht.reshape(b, hp, 2, nk, tq).transpose(0, 1, 3, 2, 4)
    kv_spec = pl.BlockSpec((1, t, 2 * HEAD_DIM), lambda i, h, j: (i, 0, h))
    return pl.pallas_call(
        functools.partial(_fox_kernel, tq=tq),
        grid=(b, hp, nk),
        in_specs=[pl.BlockSpec((1, tq, 2 * HEAD_DIM), lambda i, h, j: (i, j, h)), kv_spec, kv_spec,
                  pl.BlockSpec((1, 1, nk, 2, tq), lambda i, h, j: (i, h, 0, 0, 0))],
        out_specs=pl.BlockSpec((1, tq, 2 * HEAD_DIM), lambda i, h, j: (i, j, h)),
        out_shape=jax.ShapeDtypeStruct((b, t, dm), F32),
        compiler_params=_params(("parallel", "parallel", "arbitrary")),
        name="fox_prompt",
    )(q, k, v, f)


def _band_kernel(q_ref, k_ref, v_ref, bias_ref, o_ref, *, tq):
    i = pl.program_id(2)
    q = q_ref[0] * ATTN_SCALE
    lane = _iota((1, 2 * HEAD_DIM), 1)
    n_piece = BAND // tq + 1
    outs = []
    for hh in range(2):
        qm = jnp.where((lane // HEAD_DIM) == hh, q, 0.0).astype(BF16)
        ss, vs = [], []
        for p in range(n_piece):
            blk = i - (n_piece - 1) + p
            start = jnp.maximum(blk, 0) * tq
            kb = k_ref[0, pl.ds(start, tq), :].astype(BF16)
            s = _dot_nt(qm, kb) + bias_ref[0, hh, :, p * tq:(p + 1) * tq]
            ss.append(jnp.where(blk >= 0, s, NEG))
            vs.append(v_ref[0, pl.ds(start, tq), :].astype(BF16))
        m = jnp.max(ss[0], axis=-1, keepdims=True)
        for s in ss[1:]:
            m = jnp.maximum(m, jnp.max(s, axis=-1, keepdims=True))
        l = jnp.zeros((tq, 1), F32)
        acc = jnp.zeros((tq, 2 * HEAD_DIM), F32)
        for s, vb in zip(ss, vs):
            p_ = jnp.exp(s - m)
            l = l + jnp.sum(p_, axis=-1, keepdims=True)
            acc = acc + _dot(p_.astype(BF16), vb)
        outs.append(acc / l)
    o_ref[0] = jnp.where((lane // HEAD_DIM) == 0, outs[0], outs[1])


def _toeplitz_bias(rel_bias, nq, nk, offset):
    h = rel_bias.shape[0]
    m = jnp.arange(nq + nk - 1)
    u = rel_bias[:, jnp.clip(offset + nq - 1 - m, -REL_CLIP, REL_CLIP) + REL_CLIP].astype(F32)
    period = nq + nk
    up = jnp.pad(u, ((0, 0), (0, 1)))
    skew = jnp.tile(up, (1, nq))[:, :nq * (period - 1)].reshape(h, nq, period - 1)
    return skew[:, :, nq - 1:]


def _band_bias_prompt(rel_bias, tq):
    nk = BAND + tq
    bias = _toeplitz_bias(rel_bias, tq, nk, BAND)
    qc = jnp.arange(tq)[:, None] // CHUNK
    kc = jnp.arange(nk)[None, :] // CHUNK
    valid = (kc >= qc) & (kc <= qc + BAND // CHUNK)
    return jnp.where(valid[None], bias, NEG)


def _band_prompt(q, k, v, rel_bias, tq):
    b, t, dm = q.shape
    hp = dm // (2 * HEAD_DIM)
    bias = _band_bias_prompt(rel_bias, tq).reshape(hp, 2, tq, BAND + tq)
    kv_spec = pl.BlockSpec((1, t, 2 * HEAD_DIM), lambda i, h, j: (i, 0, h))
    return pl.pallas_call(
        functools.partial(_band_kernel, tq=tq),
        grid=(b, hp, t // tq),
        in_specs=[pl.BlockSpec((1, tq, 2 * HEAD_DIM), lambda i, h, j: (i, j, h)), kv_spec, kv_spec,
                  pl.BlockSpec((1, 2, tq, BAND + tq), lambda i, h, j: (h, 0, 0, 0))],
        out_specs=pl.BlockSpec((1, tq, 2 * HEAD_DIM), lambda i, h, j: (i, j, h)),
        out_shape=jax.ShapeDtypeStruct((b, t, dm), F32),
        compiler_params=_params(("parallel", "parallel", "arbitrary")),
        name="band_prompt",
    )(q, k, v, bias)


def _cached_attn_kernel(*refs, fox):
    if fox:
        q_ref, kn_ref, vn_ref, ck_ref, cv_ref, fc_ref, fn_ref, o_ref = refs
    else:
        q_ref, kn_ref, vn_ref, ck_ref, cv_ref, bc_ref, bn_ref, o_ref = refs
    tn, dm = q_ref.shape[1], q_ref.shape[2]
    nst = N_HEADS * tn
    stackmask = (_iota((nst, dm), 0) // tn) == (_iota((nst, dm), 1) // HEAD_DIM)
    q = q_ref[0] * ATTN_SCALE
    qst = jnp.where(stackmask, jnp.concatenate([q] * N_HEADS, axis=0), 0.0).astype(BF16)
    s_c = _dot_nt(qst, ck_ref[0].astype(BF16))
    s_n = _dot_nt(qst, kn_ref[0].astype(BF16))
    if fox:
        expand = ((_iota((nst, N_HEADS), 0) // tn) == _iota((nst, N_HEADS), 1)).astype(F32)
        fc = fc_ref[0]
        upper = (_iota((tn, tn), 0) <= _iota((tn, tn), 1)).astype(F32)
        fnew = fc[:, fc.shape[1] - 1:] + _dot(fn_ref[0], upper, HI)
        s_c = s_c - _dot(expand, fc, HI)
        s_n = s_n - _dot(expand, fnew, HI)
        tq = _iota((nst, tn), 0) % tn
        s_n = jnp.where(_iota((nst, tn), 1) <= tq, s_n, NEG)
    else:
        s_c = s_c + bc_ref[...]
        s_n = s_n + bn_ref[...]
    m = jnp.maximum(jnp.max(s_c, axis=-1, keepdims=True), jnp.max(s_n, axis=-1, keepdims=True))
    p_c = jnp.exp(s_c - m)
    p_n = jnp.exp(s_n - m)
    l = jnp.sum(p_c, axis=-1, keepdims=True) + jnp.sum(p_n, axis=-1, keepdims=True)
    ov = _dot(p_c.astype(BF16), cv_ref[0].astype(BF16)) + _dot(p_n.astype(BF16), vn_ref[0].astype(BF16))
    ov = jnp.where(stackmask, ov / l, 0.0)
    out = ov[0:tn]
    for hh in range(1, N_HEADS):
        out = out + ov[hh * tn:(hh + 1) * tn]
    o_ref[0] = out


def _cached_attn(q, kn, vn, ck, cv, extra_c, extra_n, fox):
    b, tn, dm = q.shape
    p = ck.shape[1]
    new_spec = pl.BlockSpec((1, tn, dm), lambda i: (i, 0, 0))
    cache_spec = pl.BlockSpec((1, p, dm), lambda i: (i, 0, 0))
    if fox:
        ex_specs = [pl.BlockSpec((1, N_HEADS, p), lambda i: (i, 0, 0)),
                    pl.BlockSpec((1, N_HEADS, tn), lambda i: (i, 0, 0))]
    else:
        ex_specs = [_const_spec(extra_c.shape), _const_spec(extra_n.shape)]
    return pl.pallas_call(
        functools.partial(_cached_attn_kernel, fox=fox),
        grid=(b,),
        in_specs=[new_spec, new_spec, new_spec, cache_spec, cache_spec] + ex_specs,
        out_specs=new_spec,
        out_shape=jax.ShapeDtypeStruct((b, tn, dm), F32),
        compiler_params=_params(("parallel",)),
        name="fox_sample" if fox else "band_sample",
    )(q, kn, vn, ck, cv, extra_c, extra_n)


def _to_bd(s):
    b = s.shape[0]
    s5 = s.reshape(b, N_GROUPS, HEADS_PER_GROUP, HEAD_DIM, HEAD_DIM)
    eye = jnp.eye(HEADS_PER_GROUP, dtype=s.dtype)
    bd = s5[:, :, :, :, None, :] * eye[None, None, :, None, :, None]
    return bd.reshape(b, N_GROUPS, GROUP, GROUP)


def _from_bd(bd):
    b = bd.shape[0]
    s6 = bd.reshape(b, N_GROUPS, HEADS_PER_GROUP, HEAD_DIM, HEADS_PER_GROUP, HEAD_DIM)
    d = jnp.diagonal(s6, axis1=2, axis2=4)
    return jnp.moveaxis(d, -1, 2).reshape(b, N_HEADS, HEAD_DIM, HEAD_DIM)


def _trunk(x, mods, P, cache, tm, tq, L):
    b, t, d = x.shape
    dm = D_MIX
    sample = cache is not None
    new = {}
    za, q, k, v, logf = _proj(x, mods[0], P["norm_mix_g"][0], P["ab_w"], (P["a_cols"], dm, dm, dm), tm,
                              wf=P["ab_wf"], bf=P["fox_bf"])
    ac = P["a_cols"]
    if sample:
        nb, tn = cache["nb"], cache["tn"]
        za_b = za.reshape(nb, tn, ac)
        shift0 = cache["rwkv_shift"][0]
        s0 = _to_bd(cache["rwkv"][0])
    else:
        nb, tn = b, t
        za_b = za
        shift0 = jnp.zeros((nb, ac), F32)
        s0 = jnp.zeros((nb, N_GROUPS, GROUP, GROUP), F32)
    ya, st = _rwkv(za_b, shift0, s0, P["rwkv"], L)
    new["rwkv"] = _from_bd(st)[None]
    new["rwkv_shift"] = za_b[:, -1][None]
    qb, kb, vb = (a.reshape(nb, tn, dm) for a in (q, k, v))
    logf_b = logf.reshape(nb, tn, N_HEADS)
    logf_t = jnp.swapaxes(logf_b, 1, 2)
    if sample:
        ck = cache["fox_k"][0].reshape(nb, -1, dm)
        cv = cache["fox_v"][0].reshape(nb, -1, dm)
        fc = _cumsum_time(jnp.swapaxes(cache["fox_logf"][0], 1, 2))
        yb = _cached_attn(qb, kb, vb, ck, cv, fc, logf_t, fox=True)
    else:
        yb = _fox_prompt(qb, kb, vb, _cumsum_time(logf_t), tq)
    new["fox_k"] = kb.reshape(1, nb, tn, N_HEADS, HEAD_DIM)
    new["fox_v"] = vb.reshape(1, nb, tn, N_HEADS, HEAD_DIM)
    new["fox_logf"] = logf_b[None]
    fix = None
    if sample:
        buf = cache["ffn_conv"][0]
        zero = jnp.zeros((nb, tn - 2, buf.shape[-1]), F32)
        fix = (jnp.concatenate([buf[:, 1:2], buf[:, 0:1] * 0, zero], axis=1).reshape(1, nb * tn, -1),
               jnp.concatenate([buf, zero], axis=1).reshape(1, nb * tn, -1))
    res = _tail(x, ya.reshape(b, t, dm), yb.reshape(b, t, dm), mods[0], mods[1], P["ab_wo"],
                P["norm_ffn_g"][0], P["wup"][0], P["conv_w"][0], P["conv_b"][0], P["wdn"][0],
                P["final_g"], False, tm, fix=fix, period=tn)
    x = res[0]
    conv0 = res[1].reshape(nb, tn, -1)[:, tn - 2:] if sample else res[1]
    q, k, v, zq, zf, zi, zg = _proj(x, mods[2], P["norm_mix_g"][1], P["cd_w"], (dm,) * 7, tm)
    qb, kb, vb, zq, zf, zi, zg = (a.reshape(nb, tn, dm) for a in (q, k, v, zq, zf, zi, zg))
    if sample:
        ck = cache["chunk_k"][0].reshape(nb, -1, dm)
        cv = cache["chunk_v"][0].reshape(nb, -1, dm)
        yc = _cached_attn(qb, kb, vb, ck, cv, P["band_bias_c"], P["band_bias_n"], fox=False)
        new["chunk_k"] = kb.reshape(1, nb, tn, N_HEADS, HEAD_DIM)
        new["chunk_v"] = vb.reshape(1, nb, tn, N_HEADS, HEAD_DIM)
        s0 = _to_bd(jnp.swapaxes(cache["hgrn"][0], -1, -2))
    else:
        yc = _band_prompt(qb, kb, vb, P["rel_bias"], 4 * CHUNK)
        keep = min(BAND, tn)
        new["chunk_k"] = kb[:, tn - keep:].reshape(1, nb, keep, N_HEADS, HEAD_DIM)
        new["chunk_v"] = vb[:, tn - keep:].reshape(1, nb, keep, N_HEADS, HEAD_DIM)
        s0 = jnp.zeros((nb, N_GROUPS, GROUP, GROUP), F32)
    yd, st = _hgrn(zq, zf, zi, zg, s0, P["hgrn_lb"], P["hgrn_norm_g"], L)
    new["hgrn"] = jnp.swapaxes(_from_bd(st), -1, -2)[None]
    if sample:
        buf = cache["ffn_conv"][1]
        fix = (jnp.concatenate([buf[:, 1:2], buf[:, 0:1] * 0, zero], axis=1).reshape(1, nb * tn, -1),
               jnp.concatenate([buf, zero], axis=1).reshape(1, nb * tn, -1))
    res = _tail(x, yc.reshape(b, t, dm), yd.reshape(b, t, dm), mods[2], mods[3], P["cd_wo"],
                P["norm_ffn_g"][1], P["wup"][1], P["conv_w"][1], P["conv_b"][1], P["wdn"][1],
                P["final_g"], True, tm, fix=fix, period=tn)
    conv1 = res[1].reshape(nb, tn, -1)[:, tn - 2:] if sample else res[1]
    new["ffn_conv"] = jnp.stack([conv0, conv1])
    y = res[0].reshape(nb, tn, d)
    return y, new


def kernel(x_prompt, x_sample, c_prompt, c_sample, cache_fox_k, cache_fox_v, cache_fox_logf, state_rwkv,
           state_rwkv_shift, cache_chunk_k, cache_chunk_v, state_hgrn, state_ffn_conv, ada_w, ada_b,
           norm_mix_g, norm_ffn_g, ab_w_in, rwkv_mu, rwkv_w0, rwkv_w2, rwkv_a0, rwkv_a2, rwkv_g2,
           rwkv_k_k, rwkv_k_a, rwkv_r_k, rwkv_lnx_g, rwkv_lnx_b, fox_b_f, ab_w_out, cd_w_in,
           chunk_rel_bias, hgrn_lb_table, hgrn_norm_g, cd_w_out, ffn_w_up, ffn_conv_w, ffn_conv_b,
           ffn_w_down, final_norm_g):
    bp, tp, d = x_prompt.shape
    bs, ts, _ = x_sample.shape
    depth = ada_w.shape[0]
    dm = D_MIX
    a_cols = rwkv_mu.shape[1]
    n_lw, n_la, n_lg = rwkv_w2.shape[1], rwkv_a2.shape[1], rwkv_g2.shape[1]

    c_all = jnp.concatenate([c_prompt, c_sample], axis=0)
    mods = _adaln(c_all, ada_w.reshape(depth * 2, d, 3 * d), ada_b.reshape(depth * 2, 3 * d))
    mods_p = [mods[i, :bp].reshape(bp, 1, 3 * d) for i in range(depth * 2)]
    mods_s = [jnp.repeat(mods[i, bp:], ts, axis=0).reshape(1, bs * ts, 3 * d) for i in range(depth * 2)]

    wl = jnp.zeros((n_lw + n_la + n_lg, 3 * dm), F32)
    wl = wl.at[:n_lw, 0:dm].set(rwkv_w2[0])
    wl = wl.at[n_lw:n_lw + n_la, dm:2 * dm].set(rwkv_a2[0])
    wl = wl.at[n_lw + n_la:, 2 * dm:].set(rwkv_g2[0])
    sm = jax.nn.softmax(hgrn_lb_table.astype(F32), axis=0)
    lb = (jnp.cumsum(sm, axis=0) - sm[0])[1]
    wf = jnp.zeros((d, 128), F32).at[:, :N_HEADS].set(ab_w_in[0][:, a_cols + 3 * dm:]).astype(BF16)
    bf = jnp.zeros((1, 128), F32).at[0, :N_HEADS].set(fox_b_f[0])
    p_c = cache_chunk_k.shape[2]
    bias_s = _toeplitz_bias(chunk_rel_bias[0], ts, p_c + ts, p_c).reshape(N_HEADS * ts, p_c + ts)

    P = {
        "a_cols": a_cols,
        "norm_mix_g": norm_mix_g, "norm_ffn_g": norm_ffn_g, "final_g": final_norm_g,
        "ab_w": ab_w_in[0][:, :a_cols + 3 * dm].astype(BF16), "ab_wf": wf, "fox_bf": bf,
        "ab_wo": ab_w_out[0].astype(BF16),
        "cd_w": cd_w_in[0].astype(BF16), "cd_wo": cd_w_out[0].astype(BF16),
        "wup": ffn_w_up.astype(BF16), "wdn": ffn_w_down.astype(BF16),
        "conv_w": ffn_conv_w, "conv_b": ffn_conv_b,
        "rwkv": {"mu": rwkv_mu[0], "w0": rwkv_w0[0], "a0": rwkv_a0[0], "wl": jnp.stack(_sp(wl, 2)), "k_k": rwkv_k_k[0],
                 "k_a": rwkv_k_a[0], "r_k": rwkv_r_k[0], "lnx_g": rwkv_lnx_g[0], "lnx_b": rwkv_lnx_b[0]},
        "rel_bias": chunk_rel_bias[0], "band_bias_c": bias_s[:, :p_c], "band_bias_n": bias_s[:, p_c:],
        "hgrn_lb": lb, "hgrn_norm_g": hgrn_norm_g[0],
    }
    cache = {"nb": bs, "tn": ts, "fox_k": cache_fox_k, "fox_v": cache_fox_v, "fox_logf": cache_fox_logf,
             "rwkv": state_rwkv, "rwkv_shift": state_rwkv_shift, "chunk_k": cache_chunk_k,
             "chunk_v": cache_chunk_v, "hgrn": state_hgrn, "ffn_conv": state_ffn_conv}

    y_p, sp = _trunk(x_prompt, mods_p, P, None, tm=512, tq=512, L=CHUNK)
    y_s, ss = _trunk(x_sample.reshape(1, bs * ts, d), mods_s, P, cache, tm=bs * ts, tq=None, L=ts)
    names = ("fox_k", "fox_v", "fox_logf", "rwkv", "rwkv_shift", "chunk_k", "chunk_v", "hgrn", "ffn_conv")
    return (y_p, y_s) + tuple(sp[n] for n in names) + tuple(ss[n] for n in names)
```

```python
import functools
import math

import jax
import jax.numpy as jnp
from jax import lax
from jax.experimental import pallas as pl
from jax.experimental.pallas import tpu as pltpu

F32 = jnp.float32
BF16 = jnp.bfloat16
HI = lax.Precision.HIGHEST

HEAD_DIM = 64
N_HEADS = 8
D_MIX = N_HEADS * HEAD_DIM
GROUP = 256
N_GROUPS = D_MIX // GROUP
HEADS_PER_GROUP = GROUP // HEAD_DIM
SUB = 16
CHUNK = 64
BAND = 512
REL_CLIP = 128
RMS_EPS = 1e-6
GN_EPS = 64e-5
ATTN_SCALE = HEAD_DIM ** -0.5
NEG = -1e30
HG_SCORE = 1
HG_STATE = 1
HG_VALUE = 1
HG_UPDATE = 1
VMEM_LIMIT = 56 * 1024 * 1024


def _dot(a, b, prec=None):
    return jnp.dot(a, b, preferred_element_type=F32, precision=prec)


def _dot_nt(a, b, prec=None):
    return lax.dot_general(a, b, (((1,), (1,)), ((), ())), preferred_element_type=F32, precision=prec)


def _dot_tn(a, b, prec=None):
    return lax.dot_general(a, b, (((0,), (0,)), ((), ())), preferred_element_type=F32, precision=prec)


_NN = ((1,), (0,))
_NT = ((1,), (1,))
_TN = ((0,), (0,))


def _sp(x, n):
    hi = x.astype(BF16)
    if n == 1:
        return (hi,)
    r = x - hi.astype(F32)
    mid = r.astype(BF16)
    if n == 2:
        return (hi, mid)
    return (hi, mid, (r - mid.astype(F32)).astype(BF16))


def bf(x):
    return (x.astype(BF16),)


def _mm(a, b, dims=_NN):
    n = max(len(a), len(b))
    out = None
    for i, ai in enumerate(a):
        for j, bj in enumerate(b):
            if i + j < n:
                d = lax.dot_general(ai, bj, (dims, ((), ())), preferred_element_type=F32)
                out = d if out is None else out + d
    return out


def _head_sum(x, bo_group):
    return jnp.concatenate(
        [_mm(_sp(x[:, gi * GROUP:(gi + 1) * GROUP], 2), (bo_group,)) for gi in range(N_GROUPS)], axis=1)


def _sigmoid(x):
    return jax.nn.sigmoid(x)


def _silu(x):
    return x * jax.nn.sigmoid(x)


def _softplus(x):
    return jnp.maximum(x, 0.0) + jnp.log1p(jnp.exp(-jnp.abs(x)))


def _log_sigmoid(x):
    return -_softplus(-x)


def _iota(shape, dim):
    return lax.broadcasted_iota(jnp.int32, shape, dim)


def _const_spec(shape):
    nd = len(shape)
    return pl.BlockSpec(shape, lambda *_: (0,) * nd, pipeline_mode=pl.Buffered(1))


def _params(sem):
    return pltpu.CompilerParams(dimension_semantics=sem, vmem_limit_bytes=VMEM_LIMIT)


def _norm_mod(x, g, shift, scale):
    xn = x * lax.rsqrt(jnp.mean(x * x, axis=-1, keepdims=True) + RMS_EPS) * g
    return xn * (1.0 + scale) + shift


def _block_ones(n, blk):
    i = jnp.arange(n) // blk
    return (i[:, None] == i[None, :]).astype(BF16)


def _bd(x, rows_per_head, cols_per_head):
    n = HEADS_PER_GROUP
    t = jnp.concatenate([x] * n, axis=0)
    r = _iota(t.shape, 0) // rows_per_head
    c = _iota(t.shape, 1) // cols_per_head
    return jnp.where(r == c, t, 0.0)


def _unbd(m):
    out = m[0:HEAD_DIM]
    for h in range(1, HEADS_PER_GROUP):
        out = out + m[h * HEAD_DIM:(h + 1) * HEAD_DIM]
    return out


def _adaln_kernel(c_ref, w_ref, b_ref, o_ref):
    o_ref[0] = _dot(_silu(c_ref[...]), w_ref[0], HI) + b_ref[0]


def _adaln(c_all, ada_w, ada_b):
    n, d, d3 = ada_w.shape
    bt = c_all.shape[0]
    nt = d3 // d
    return pl.pallas_call(
        _adaln_kernel,
        grid=(n, nt),
        in_specs=[pl.BlockSpec((bt, d), lambda i, j: (0, 0)),
                  pl.BlockSpec((1, d, d), lambda i, j: (i, 0, j)),
                  pl.BlockSpec((1, 1, d), lambda i, j: (i, 0, j))],
        out_specs=pl.BlockSpec((1, bt, d), lambda i, j: (i, 0, j)),
        out_shape=jax.ShapeDtypeStruct((n, bt, d3), F32),
        compiler_params=_params(("parallel", "parallel")),
        name="adaln",
    )(c_all, ada_w, ada_b.reshape(n, 1, d3))


def _proj_kernel(*refs, splits, fox):
    x_ref, mod_ref, g_ref, w_ref = refs[:4]
    pos = 4
    if fox:
        wf_ref, bf_ref = refs[4:6]
        pos = 6
    outs = refs[pos:]
    d = x_ref.shape[-1]
    mod = mod_ref[0]
    h = _norm_mod(x_ref[0], g_ref[...], mod[:, :d], mod[:, d:2 * d]).astype(BF16)
    off = 0
    for o_ref, n in zip(outs, splits):
        o_ref[0] = _dot(h, w_ref[:, off:off + n])
        off += n
    if fox:
        zf = _dot(h, wf_ref[...]) + bf_ref[...]
        outs[len(splits)][0] = _log_sigmoid(zf)[:, :N_HEADS]


def _proj(x, mod, g, w_bf, splits, tm, wf=None, bf=None):
    b, t, d = x.shape
    r = mod.shape[1]
    fox = wf is not None
    mod_spec = (pl.BlockSpec((1, 1, 3 * d), lambda i, j: (i, 0, 0)) if r == 1
                else pl.BlockSpec((1, tm, 3 * d), lambda i, j: (i, j, 0)))
    in_specs = [pl.BlockSpec((1, tm, d), lambda i, j: (i, j, 0)), mod_spec,
                _const_spec((1, d)), _const_spec(w_bf.shape)]
    args = [x, mod, g.reshape(1, d), w_bf]
    widths = list(splits)
    if fox:
        in_specs += [_const_spec(wf.shape), _const_spec(bf.shape)]
        args += [wf, bf]
        widths.append(N_HEADS)
    return pl.pallas_call(
        functools.partial(_proj_kernel, splits=tuple(splits), fox=fox),
        grid=(b, t // tm),
        in_specs=in_specs,
        out_specs=[pl.BlockSpec((1, tm, n), lambda i, j: (i, j, 0)) for n in widths],
        out_shape=[jax.ShapeDtypeStruct((b, t, n), F32) for n in widths],
        compiler_params=_params(("parallel", "parallel")),
        name="proj",
    )(*args)


def _tail_kernel(*refs, sample, final, period, cb):
    (x_ref, ya_ref, yb_ref, m1_ref, m2_ref, wo_ref, g_ref, wup_ref, cw_ref, cbias_ref,
     wdn_ref, fg_ref) = refs[:12]
    if sample:
        fix1_ref, fix2_ref, o_ref, u_ref = refs[12:]
    else:
        o_ref, cs_ref, tail_sc = refs[12:]
    tm, d = x_ref.shape[1], x_ref.shape[2]
    ff = wdn_ref.shape[0]
    dm = ya_ref.shape[2]
    m1 = m1_ref[0]
    m2 = m2_ref[0]
    ymix = (_dot(ya_ref[0].astype(BF16), wo_ref[0:dm, :])
            + _dot(yb_ref[0].astype(BF16), wo_ref[dm:2 * dm, :]))
    x1 = x_ref[0] + m1[:, 2 * d:3 * d] * ymix
    h = _norm_mod(x1, g_ref[...], m2[:, :d], m2[:, d:2 * d]).astype(BF16)
    rows = _iota((tm, 1), 0)
    if sample:
        tpos = rows & (period - 1)
    else:
        t = pl.program_id(1)

        @pl.when(t == 0)
        def _():
            tail_sc[...] = jnp.zeros_like(tail_sc)

    acc = jnp.zeros((tm, d), F32)
    for j in range(ff // cb):
        halves = []
        for half in range(2):
            c0 = half * ff + j * cb
            u = _dot(h, wup_ref[:, c0:c0 + cb])
            r1 = pltpu.roll(u, 1, 0)
            r2 = pltpu.roll(u, 2, 0)
            if sample:
                u1 = jnp.where(tpos == 0, fix1_ref[0, :, c0:c0 + cb], r1)
                u2 = jnp.where(tpos < 2, fix2_ref[0, :, c0:c0 + cb], r2)
                u_ref[0, :, c0:c0 + cb] = u
            else:
                t6 = tail_sc[6:7, c0:c0 + cb]
                t7 = tail_sc[7:8, c0:c0 + cb]
                u1 = jnp.where(rows == 0, t7, r1)
                u2 = jnp.where(rows == 0, t6, jnp.where(rows == 1, t7, r2))
                tail_sc[:, c0:c0 + cb] = u[tm - 8:tm, :]
            cw = cw_ref[:, c0:c0 + cb]
            halves.append(cbias_ref[:, c0:c0 + cb] + cw[0:1] * u2 + cw[1:2] * u1 + cw[2:3] * u)
        gact = (_silu(halves[0]) * halves[1]).astype(BF16)
        acc = acc + _dot(gact, wdn_ref[j * cb:(j + 1) * cb, :])
    out = x1 + m2[:, 2 * d:3 * d] * acc
    if final:
        out = out * lax.rsqrt(jnp.mean(out * out, axis=-1, keepdims=True) + RMS_EPS) * fg_ref[...]
    o_ref[0] = out
    if not sample:
        @pl.when(t == pl.num_programs(1) - 1)
        def _():
            cs_ref[0] = tail_sc[6:8, :]


def _tail(x, ya, yb, mod1, mod2, wo_bf, g, wup_bf, conv_w, conv_b, wdn_bf, fg, final, tm,
          fix=None, period=None):
    b, t, d = x.shape
    ff = wdn_bf.shape[0]
    dm = ya.shape[2]
    sample = fix is not None
    r = mod1.shape[1]
    mod_spec = (pl.BlockSpec((1, 1, 3 * d), lambda i, j: (i, 0, 0)) if r == 1
                else pl.BlockSpec((1, tm, 3 * d), lambda i, j: (i, j, 0)))
    row_spec = lambda n: pl.BlockSpec((1, tm, n), lambda i, j: (i, j, 0))
    in_specs = [row_spec(d), row_spec(dm), row_spec(dm), mod_spec, mod_spec,
                _const_spec(wo_bf.shape), _const_spec((1, d)), _const_spec(wup_bf.shape),
                _const_spec(conv_w.shape), _const_spec((1, 2 * ff)), _const_spec(wdn_bf.shape),
                _const_spec((1, d))]
    args = [x, ya, yb, mod1, mod2, wo_bf, g.reshape(1, d), wup_bf, conv_w, conv_b.reshape(1, 2 * ff),
            wdn_bf, fg.reshape(1, d)]
    if sample:
        in_specs += [row_spec(2 * ff), row_spec(2 * ff)]
        args += list(fix)
        out_specs = [row_spec(d), row_spec(2 * ff)]
        out_shape = [jax.ShapeDtypeStruct((b, t, d), F32), jax.ShapeDtypeStruct((b, t, 2 * ff), F32)]
        scratch = []
    else:
        out_specs = [row_spec(d), pl.BlockSpec((1, 2, 2 * ff), lambda i, j: (i, 0, 0))]
        out_shape = [jax.ShapeDtypeStruct((b, t, d), F32), jax.ShapeDtypeStruct((b, 2, 2 * ff), F32)]
        scratch = [pltpu.VMEM((8, 2 * ff), F32)]
    return pl.pallas_call(
        functools.partial(_tail_kernel, sample=sample, final=final, period=period, cb=256),
        grid=(b, t // tm),
        in_specs=in_specs,
        out_specs=out_specs,
        out_shape=out_shape,
        scratch_shapes=scratch,
        compiler_params=_params(("parallel", "arbitrary")),
        name="tail",
    )(*args)


def _rwkv_kernel(z_ref, shift0_ref, s0_ref, mu_ref, w0_ref, a0_ref, wl_ref, kk_ref, ka_ref, rk_ref,
                 lng_ref, lnb_ref, bo_ref, y_ref, st_ref, prev_sc, st_sc, *, L, nc):
    t = pl.program_id(1)
    tb = nc * L

    @pl.when(t == 0)
    def _():
        prev_sc[...] = shift0_ref[0]
        for gi in range(N_GROUPS):
            st_sc[gi] = _bd(s0_ref[0, gi], HEAD_DIM, HEAD_DIM)

    z = z_ref[0]
    rows = _iota((tb, 1), 0)
    z_prev = jnp.where(rows == 0, prev_sc[...], pltpu.roll(z, 1, 0))
    prev_sc[...] = z[tb - 1:tb, :]
    zs = z + mu_ref[...] * (z_prev - z)
    dm = D_MIX
    r = zs[:, 0:dm]
    k = zs[:, dm:2 * dm]
    v = zs[:, 2 * dm:3 * dm]
    lr = zs[:, 3 * dm:]
    nl = lr.shape[1]
    lane = _iota((1, nl), 1)
    act = jnp.where(lane < 64, jnp.tanh(lr), jnp.where(lane < 128, lr, _sigmoid(lr)))
    lo = _mm(bf(act), (wl_ref[...],))
    w = -_softplus(-(w0_ref[...] + lo[:, 0:dm])) - 0.5
    lw = -jnp.exp(w)
    a = _sigmoid(a0_ref[...] + lo[:, dm:2 * dm])
    g = lo[:, 2 * dm:3 * dm]
    bo = bo_ref[...]
    kk = k * kk_ref[...]
    k2 = k * (1.0 + (a - 1.0) * ka_ref[...])
    sums = _head_sum(jnp.concatenate([kk * kk, r * k2 * rk_ref[...]], axis=0), bo)
    kk = kk / jnp.maximum(jnp.sqrt(sums[:tb]), 1e-12)
    bonus = sums[tb:]
    am = -kk
    bm = kk * a

    ri = _iota((tb, tb), 0)
    ci_ = _iota((tb, tb), 1)
    tri = ((ci_ <= ri) & (ci_ // L == ri // L)).astype(BF16)
    c = _mm((tri,), _sp(lw, 3))
    c_last = jnp.concatenate(
        [jnp.broadcast_to(c[(j + 1) * L - 1:(j + 1) * L, :], (L, c.shape[1])) for j in range(nc)], axis=0)
    a_t = am * jnp.exp(c - lw)
    r_t = r * jnp.exp(c)
    g_inv = jnp.exp(-c)
    k_t = k2 * g_inv
    b_t = bm * g_inv
    g_end = jnp.exp(c_last - c)
    k_end = k2 * g_end
    b_end = bm * g_end
    g_last = jnp.exp(c_last)

    cw = HEADS_PER_GROUP * L
    e_tile = ((_iota((L, cw), 1) % L) == _iota((L, cw), 0)).astype(BF16)
    tmask = (_iota((GROUP, cw), 0) // HEAD_DIM) == (_iota((GROUP, cw), 1) // L)
    col_s = _iota((L, cw), 1) % L
    row_t = _iota((L, cw), 0)
    strict = col_s < row_t
    incl = col_s <= row_t
    bdmask = (_iota((GROUP, GROUP), 0) // HEAD_DIM) == (_iota((GROUP, GROUP), 1) // HEAD_DIM)
    n_dbl = int(math.log2(L))

    def bd(x, n, cols_per_head=HEAD_DIM):
        return tuple(_bd(p, L, cols_per_head) for p in _sp(x, n))

    eye_cat = (col_s == row_t).astype(F32)

    insts = [(cj, gi) for cj in range(nc) for gi in range(N_GROUPS)]
    rsl = {cj: slice(cj * L, (cj + 1) * L) for cj in range(nc)}
    gsl = {gi: slice(gi * GROUP, (gi + 1) * GROUP) for gi in range(N_GROUPS)}
    bk_t = {cj: _mm(bf(jnp.concatenate([b_t[rsl[cj]], k_t[rsl[cj]]], axis=1)), (e_tile,), _TN) for cj in range(nc)}
    ar = {(cj, gi): bf(jnp.concatenate([a_t[rsl[cj], gsl[gi]], r_t[rsl[cj], gsl[gi]]], axis=0)) for cj, gi in insts}
    arb = {(cj, gi): _mm(ar[cj, gi], bf(jnp.where(tmask, bk_t[cj][gsl[gi]], 0.0))) for cj, gi in insts}
    ark = {(cj, gi): _mm(ar[cj, gi], bf(jnp.where(tmask, bk_t[cj][dm + gi * GROUP:dm + (gi + 1) * GROUP], 0.0)))
           for cj, gi in insts}
    p = {k_: jnp.where(strict, arb[k_][:L], 0.0) for k_ in insts}
    m_rb = {k_: bf(jnp.where(incl, arb[k_][L:], 0.0)) for k_ in insts}
    nmv = {(cj, gi): _mm(bf(jnp.concatenate([jnp.where(strict, ark[cj, gi][:L], 0.0),
                                              jnp.where(incl, ark[cj, gi][L:], 0.0)], axis=0)),
                         bd(v[rsl[cj], gsl[gi]], 1)) for cj, gi in insts}
    tm = {k_: eye_cat + p[k_] for k_ in insts}
    for i in range(n_dbl - 1):
        for k_ in insts:
            w = bd(p[k_], 1, L)
            if i == 0:
                p[k_] = _mm(bf(p[k_]), w)
            else:
                both = _mm(bf(jnp.concatenate([p[k_], tm[k_]], axis=0)), w)
                p[k_] = both[:L]
                tm[k_] = tm[k_] + both[L:]
    tm = {k_: bf(tm[k_] + _mm(bf(tm[k_]), bd(p[k_], 1, L))) for k_ in insts}

    groups = range(N_GROUPS)
    state = [st_sc[gi] for gi in groups]
    y_rows = []
    for cj in range(nc):
        rs = rsl[cj]
        ars = [_mm(ar[cj, gi], bf(state[gi]), _NT) for gi in groups]
        u = [_mm(tm[cj, gi], bd(ars[gi][:L] + nmv[cj, gi][:L], 1)) for gi in groups]
        ys = [ars[gi][L:] + nmv[cj, gi][L:] + _mm(m_rb[cj, gi], bd(u[gi], 1)) for gi in groups]
        upd = [_mm(bf(jnp.concatenate([v[rs, gsl[gi]], u[gi]], axis=0)),
                   bf(jnp.concatenate([k_end[rs, gsl[gi]], b_end[rs, gsl[gi]]], axis=0)), _TN) for gi in groups]
        state = [state[gi] * g_last[rs, gsl[gi]][0:1] + jnp.where(bdmask, upd[gi], 0.0) for gi in groups]
        y_rows.append(jnp.concatenate(ys, axis=1))
    for gi in groups:
        st_sc[gi] = state[gi]

    y = jnp.concatenate(y_rows, axis=0) if nc > 1 else y_rows[0]
    inv = 1.0 / HEAD_DIM
    mean = _head_sum(y, bo) * inv
    dlt = y - mean
    var = _head_sum(dlt * dlt, bo) * inv
    yn = dlt * lax.rsqrt(var + GN_EPS) * lng_ref[...] + lnb_ref[...]
    y_ref[0] = (yn + bonus * v) * g

    @pl.when(t == pl.num_programs(1) - 1)
    def _():
        for gi in range(N_GROUPS):
            st_ref[0, gi] = _unbd(st_sc[gi])


def _state_spec():
    return pl.BlockSpec((1, N_GROUPS, HEAD_DIM, GROUP), lambda i, j: (i, 0, 0, 0))


def _rwkv(za, shift0, s0_cat, pr, L, nc):
    b, t, ac = za.shape
    dm = D_MIX
    tb = L * nc
    row = lambda a: a.reshape(1, -1)
    consts = [row(pr["mu"]), row(pr["w0"]), row(pr["a0"]), pr["wl"], row(pr["k_k"]), row(pr["k_a"]),
              row(pr["r_k"]), row(pr["lnx_g"]), row(pr["lnx_b"]), _block_ones(GROUP, HEAD_DIM)]
    return pl.pallas_call(
        functools.partial(_rwkv_kernel, L=L, nc=nc),
        grid=(b, t // tb),
        in_specs=[pl.BlockSpec((1, tb, ac), lambda i, j: (i, j, 0)),
                  pl.BlockSpec((1, 1, ac), lambda i, j: (i, 0, 0)), _state_spec()]
                 + [_const_spec(c.shape) for c in consts],
        out_specs=[pl.BlockSpec((1, tb, dm), lambda i, j: (i, j, 0)), _state_spec()],
        out_shape=[jax.ShapeDtypeStruct((b, t, dm), F32),
                   jax.ShapeDtypeStruct((b, N_GROUPS, HEAD_DIM, GROUP), F32)],
        scratch_shapes=[pltpu.VMEM((1, ac), F32), pltpu.VMEM((N_GROUPS, GROUP, GROUP), F32)],
        compiler_params=_params(("parallel", "arbitrary")),
        name="rwkv7",
    )(za, shift0.reshape(b, 1, ac), s0_cat, *consts)


def _hgrn_kernel(zq_ref, zf_ref, zi_ref, zg_ref, s0_ref, lb_ref, ng_ref, bo_ref, y_ref, st_ref, st_sc, *, L):
    t = pl.program_id(1)

    @pl.when(t == 0)
    def _():
        for gi in range(N_GROUPS):
            st_sc[gi] = _bd(s0_ref[0, gi], HEAD_DIM, HEAD_DIM)

    dm = D_MIX
    q = _silu(zq_ref[0])
    xf = zf_ref[0]
    v = zi_ref[0]
    lb = lb_ref[...]
    la = jnp.log(lb)
    lc = jnp.log1p(-lb) + _log_sigmoid(xf)
    logf = jnp.maximum(la, lc) + jnp.log1p(jnp.exp(-jnp.abs(la - lc)))
    k = (1.0 - lb) * _sigmoid(-xf)
    bo = bo_ref[...]

    tri = (_iota((L, L), 1) <= _iota((L, L), 0)).astype(BF16)
    b = _mm((tri,), _sp(logf, 3))
    b_last = b[L - 1:L, :]
    qe = q * jnp.exp(b)
    k_end = k * jnp.exp(b_last - b)
    g_last = jnp.exp(b_last)

    o = jnp.concatenate([_mm(_sp(qe[:, gi * GROUP:(gi + 1) * GROUP], HG_STATE), _sp(st_sc[gi], HG_STATE), _NT)
                         for gi in range(N_GROUPS)], axis=1)

    n_sub = L // SUB
    nst = N_HEADS * SUB
    stackmask = (_iota((nst, dm), 0) // SUB) == (_iota((nst, dm), 1) // HEAD_DIM)
    sub_rows = _iota((SUB, 1), 0)
    o_subs = []
    for i in range(n_sub):
        lo_, hi_ = i * SUB, (i + 1) * SUB
        bi = b[lo_:hi_]
        qi = q[lo_:hi_]
        ki = k[lo_:hi_]
        vi = v[lo_:hi_]
        xs = []
        for s in range(SUB):
            e = jnp.exp(jnp.where(sub_rows >= s, bi - bi[s:s + 1], NEG))
            xs.append(qi * e * ki[s:s + 1])
        att = _head_sum(jnp.concatenate(xs, axis=0), bo)
        oi = att[0:SUB] * vi[0:1]
        for s in range(1, SUB):
            oi = oi + att[s * SUB:(s + 1) * SUB] * vi[s:s + 1]
        if i > 0:
            bref = b[lo_ - 1:lo_]
            qh = qi * jnp.exp(bi - bref)
            kh = k[0:lo_] * jnp.exp(bref - b[0:lo_])
            qst = jnp.where(stackmask, jnp.concatenate([qh] * N_HEADS, axis=0), 0.0)
            att2 = _mm(_sp(qst, HG_SCORE), _sp(kh, HG_SCORE), _NT)
            ov = jnp.where(stackmask, _mm(_sp(att2, HG_VALUE), _sp(v[0:lo_], HG_VALUE)), 0.0)
            for hh in range(N_HEADS):
                oi = oi + ov[hh * SUB:(hh + 1) * SUB]
        o_subs.append(oi)
    o = o + (jnp.concatenate(o_subs, axis=0) if n_sub > 1 else o_subs[0])

    bdmask = (_iota((GROUP, GROUP), 0) // HEAD_DIM) == (_iota((GROUP, GROUP), 1) // HEAD_DIM)
    for gi in range(N_GROUPS):
        sl = slice(gi * GROUP, (gi + 1) * GROUP)
        upd = _mm(_sp(v[:, sl], HG_UPDATE), _sp(k_end[:, sl], HG_UPDATE), _TN)
        st_sc[gi] = st_sc[gi] * g_last[:, sl] + jnp.where(bdmask, upd, 0.0)

    ms = _head_sum(o * o, bo) * (1.0 / HEAD_DIM)
    y_ref[0] = o * lax.rsqrt(ms + RMS_EPS) * ng_ref[...] * _silu(zg_ref[0])

    @pl.when(t == pl.num_programs(1) - 1)
    def _():
        for gi in range(N_GROUPS):
            st_ref[0, gi] = _unbd(st_sc[gi])


def _hgrn(zq, zf, zi, zg, s0_cat, lb, norm_g, L):
    b, t, dm = zq.shape
    consts = [lb.reshape(1, dm), norm_g.reshape(1, dm), _block_ones(GROUP, HEAD_DIM)]
    row_spec = pl.BlockSpec((1, L, dm), lambda i, j: (i, j, 0))
    return pl.pallas_call(
        functools.partial(_hgrn_kernel, L=L),
        grid=(b, t // L),
        in_specs=[row_spec] * 4 + [_state_spec()] + [_const_spec(c.shape) for c in consts],
        out_specs=[row_spec, _state_spec()],
        out_shape=[jax.ShapeDtypeStruct((b, t, dm), F32),
                   jax.ShapeDtypeStruct((b, N_GROUPS, HEAD_DIM, GROUP), F32)],
        scratch_shapes=[pltpu.VMEM((N_GROUPS, GROUP, GROUP), F32)],
        compiler_params=_params(("parallel", "arbitrary")),
        name="hgrn2",
    )(zq, zf, zi, zg, s0_cat, *consts)


def _cumsum_kernel(x_ref, o_ref):
    x = x_ref[0]
    n, w = x.shape
    nblk = n // N_HEADS
    upper = (_iota((w, w), 0) <= _iota((w, w), 1)).astype(F32)
    c = _dot(x, upper, HI)
    tot = jnp.broadcast_to(c[:, w - 1:w], (n, w))
    ri = _iota((n, n), 0)
    ci = _iota((n, n), 1)
    prior = ((ri // nblk == ci // nblk) & (ci < ri)).astype(F32)
    o_ref[0] = c + _dot(prior, tot, HI)


def _cumsum_time(logf_bht):
    b, h, t = logf_bht.shape
    w = 128
    n = h * (t // w)
    x = logf_bht.reshape(b, n, w)
    out = pl.pallas_call(
        _cumsum_kernel,
        grid=(b,),
        in_specs=[pl.BlockSpec((1, n, w), lambda i: (i, 0, 0))],
        out_specs=pl.BlockSpec((1, n, w), lambda i: (i, 0, 0)),
        out_shape=jax.ShapeDtypeStruct((b, n, w), F32),
        compiler_params=_params(("parallel",)),
        name="cumsum",
    )(x)
    return out.reshape(b, h, t)


def _fox_kernel(q_ref, k_ref, v_ref, f_ref, o_ref, *, tq):
    i = pl.program_id(2)
    q = q_ref[0] * ATTN_SCALE
    lane = _iota((1, 2 * HEAD_DIM), 1)
    causal = _iota((tq, tq), 1) <= _iota((tq, tq), 0)
    outs = []
    for hh in range(2):
        qm = jnp.where((lane // HEAD_DIM) == hh, q, 0.0).astype(BF16)

        def scores(j):
            kb = k_ref[0, pl.ds(j * tq, tq), :].astype(BF16)
            return _dot_nt(qm, kb) - f_ref[0, 0, j, hh:hh + 1, :]

        def update(j, s, carry):
            m, l, acc = carry
            m_new = jnp.maximum(m, jnp.max(s, axis=-1, keepdims=True))
            alpha = jnp.exp(m - m_new)
            p = jnp.exp(s - m_new)
            vb = v_ref[0, pl.ds(j * tq, tq), :].astype(BF16)
            return (m_new, alpha * l + jnp.sum(p, axis=-1, keepdims=True),
                    alpha * acc + _dot(p.astype(BF16), vb))

        init = (jnp.full((tq, 1), NEG, F32), jnp.zeros((tq, 1), F32), jnp.zeros((tq, 2 * HEAD_DIM), F32))
        carry = lax.fori_loop(0, i, lambda j, cr: update(j, scores(j), cr), init)
        m, l, acc = update(i, jnp.where(causal, scores(i), NEG), carry)
        outs.append(acc / l)
    o_ref[0] = jnp.where((lane // HEAD_DIM) == 0, outs[0], outs[1])


def _fox_prompt(q, k, v, cum_bht, tq):
    b, t, dm = q.shape
    hp = dm // (2 * HEAD_DIM)
    nk = t // tq
    f = cum_bht.reshape(b, hp, 2, nk, tq).transpose(0, 1, 3, 2, 4)
    kv_spec = pl.BlockSpec((1, t, 2 * HEAD_DIM), lambda i, h, j: (i, 0, h))
    return pl.pallas_call(
        functools.partial(_fox_kernel, tq=tq),
        grid=(b, hp, nk),
        in_specs=[pl.BlockSpec((1, tq, 2 * HEAD_DIM), lambda i, h, j: (i, j, h)), kv_spec, kv_spec,
                  pl.BlockSpec((1, 1, nk, 2, tq), lambda i, h, j: (i, h, 0, 0, 0))],
        out_specs=pl.BlockSpec((1, tq, 2 * HEAD_DIM), lambda i, h, j: (i, j, h)),
        out_shape=jax.ShapeDtypeStruct((b, t, dm), F32),
        compiler_params=_params(("parallel", "parallel", "arbitrary")),
        name="fox_prompt",
    )(q, k, v, f)


def _band_kernel(q_ref, k_ref, v_ref, bias_ref, o_ref, *, tq):
    i = pl.program_id(2)
    q = q_ref[0] * ATTN_SCALE
    lane = _iota((1, 2 * HEAD_DIM), 1)
    n_piece = BAND // tq + 1
    outs = []
    for hh in range(2):
        qm = jnp.where((lane // HEAD_DIM) == hh, q, 0.0).astype(BF16)
        ss, vs = [], []
        for p in range(n_piece):
            blk = i - (n_piece - 1) + p
            start = jnp.maximum(blk, 0) * tq
            kb = k_ref[0, pl.ds(start, tq), :].astype(BF16)
            s = _dot_nt(qm, kb) + bias_ref[0, hh, :, p * tq:(p + 1) * tq]
            ss.append(jnp.where(blk >= 0, s, NEG))
            vs.append(v_ref[0, pl.ds(start, tq), :].astype(BF16))
        m = jnp.max(ss[0], axis=-1, keepdims=True)
        for s in ss[1:]:
            m = jnp.maximum(m, jnp.max(s, axis=-1, keepdims=True))
        l = jnp.zeros((tq, 1), F32)
        acc = jnp.zeros((tq, 2 * HEAD_DIM), F32)
        for s, vb in zip(ss, vs):
            p_ = jnp.exp(s - m)
            l = l + jnp.sum(p_, axis=-1, keepdims=True)
            acc = acc + _dot(p_.astype(BF16), vb)
        outs.append(acc / l)
    o_ref[0] = jnp.where((lane // HEAD_DIM) == 0, outs[0], outs[1])


def _toeplitz_bias(rel_bias, nq, nk, offset):
    h = rel_bias.shape[0]
    m = jnp.arange(nq + nk - 1)
    u = rel_bias[:, jnp.clip(offset + nq - 1 - m, -REL_CLIP, REL_CLIP) + REL_CLIP].astype(F32)
    period = nq + nk
    up = jnp.pad(u, ((0, 0), (0, 1)))
    skew = jnp.tile(up, (1, nq))[:, :nq * (period - 1)].reshape(h, nq, period - 1)
    return skew[:, :, nq - 1:]


def _band_bias_prompt(rel_bias, tq):
    nk = BAND + tq
    bias = _toeplitz_bias(rel_bias, tq, nk, BAND)
    qc = jnp.arange(tq)[:, None] // CHUNK
    kc = jnp.arange(nk)[None, :] // CHUNK
    valid = (kc >= qc) & (kc <= qc + BAND // CHUNK)
    return jnp.where(valid[None], bias, NEG)


def _band_prompt(q, k, v, rel_bias, tq):
    b, t, dm = q.shape
    hp = dm // (2 * HEAD_DIM)
    bias = _band_bias_prompt(rel_bias, tq).reshape(hp, 2, tq, BAND + tq)
    kv_spec = pl.BlockSpec((1, t, 2 * HEAD_DIM), lambda i, h, j: (i, 0, h))
    return pl.pallas_call(
        functools.partial(_band_kernel, tq=tq),
        grid=(b, hp, t // tq),
        in_specs=[pl.BlockSpec((1, tq, 2 * HEAD_DIM), lambda i, h, j: (i, j, h)), kv_spec, kv_spec,
                  pl.BlockSpec((1, 2, tq, BAND + tq), lambda i, h, j: (h, 0, 0, 0))],
        out_specs=pl.BlockSpec((1, tq, 2 * HEAD_DIM), lambda i, h, j: (i, j, h)),
        out_shape=jax.ShapeDtypeStruct((b, t, dm), F32),
        compiler_params=_params(("parallel", "parallel", "arbitrary")),
        name="band_prompt",
    )(q, k, v, bias)


def _cached_attn_kernel(*refs, fox):
    if fox:
        q_ref, kn_ref, vn_ref, ck_ref, cv_ref, fc_ref, fn_ref, o_ref = refs
    else:
        q_ref, kn_ref, vn_ref, ck_ref, cv_ref, bc_ref, bn_ref, o_ref = refs
    tn, dm = q_ref.shape[1], q_ref.shape[2]
    nst = N_HEADS * tn
    stackmask = (_iota((nst, dm), 0) // tn) == (_iota((nst, dm), 1) // HEAD_DIM)
    q = q_ref[0] * ATTN_SCALE
    qst = jnp.where(stackmask, jnp.concatenate([q] * N_HEADS, axis=0), 0.0).astype(BF16)
    s_c = _dot_nt(qst, ck_ref[0].astype(BF16))
    s_n = _dot_nt(qst, kn_ref[0].astype(BF16))
    if fox:
        expand = ((_iota((nst, N_HEADS), 0) // tn) == _iota((nst, N_HEADS), 1)).astype(F32)
        fc = fc_ref[0]
        upper = (_iota((tn, tn), 0) <= _iota((tn, tn), 1)).astype(F32)
        fnew = fc[:, fc.shape[1] - 1:] + _dot(fn_ref[0], upper, HI)
        s_c = s_c - _dot(expand, fc, HI)
        s_n = s_n - _dot(expand, fnew, HI)
        tq = _iota((nst, tn), 0) % tn
        s_n = jnp.where(_iota((nst, tn), 1) <= tq, s_n, NEG)
    else:
        s_c = s_c + bc_ref[...]
        s_n = s_n + bn_ref[...]
    m = jnp.maximum(jnp.max(s_c, axis=-1, keepdims=True), jnp.max(s_n, axis=-1, keepdims=True))
    p_c = jnp.exp(s_c - m)
    p_n = jnp.exp(s_n - m)
    l = jnp.sum(p_c, axis=-1, keepdims=True) + jnp.sum(p_n, axis=-1, keepdims=True)
    ov = _dot(p_c.astype(BF16), cv_ref[0].astype(BF16)) + _dot(p_n.astype(BF16), vn_ref[0].astype(BF16))
    ov = jnp.where(stackmask, ov / l, 0.0)
    out = ov[0:tn]
    for hh in range(1, N_HEADS):
        out = out + ov[hh * tn:(hh + 1) * tn]
    o_ref[0] = out


def _cached_attn(q, kn, vn, ck, cv, extra_c, extra_n, fox):
    b, tn, dm = q.shape
    p = ck.shape[1]
    new_spec = pl.BlockSpec((1, tn, dm), lambda i: (i, 0, 0))
    cache_spec = pl.BlockSpec((1, p, dm), lambda i: (i, 0, 0))
    if fox:
        ex_specs = [pl.BlockSpec((1, N_HEADS, p), lambda i: (i, 0, 0)),
                    pl.BlockSpec((1, N_HEADS, tn), lambda i: (i, 0, 0))]
    else:
        ex_specs = [_const_spec(extra_c.shape), _const_spec(extra_n.shape)]
    return pl.pallas_call(
        functools.partial(_cached_attn_kernel, fox=fox),
        grid=(b,),
        in_specs=[new_spec, new_spec, new_spec, cache_spec, cache_spec] + ex_specs,
        out_specs=new_spec,
        out_shape=jax.ShapeDtypeStruct((b, tn, dm), F32),
        compiler_params=_params(("parallel",)),
        name="fox_sample" if fox else "band_sample",
    )(q, kn, vn, ck, cv, extra_c, extra_n)


def _to_cat(s):
    b = s.shape[0]
    s5 = s.reshape(b, N_GROUPS, HEADS_PER_GROUP, HEAD_DIM, HEAD_DIM)
    return s5.transpose(0, 1, 3, 2, 4).reshape(b, N_GROUPS, HEAD_DIM, GROUP)


def _from_cat(c):
    b = c.shape[0]
    c5 = c.reshape(b, N_GROUPS, HEAD_DIM, HEADS_PER_GROUP, HEAD_DIM)
    return c5.transpose(0, 1, 3, 2, 4).reshape(b, N_HEADS, HEAD_DIM, HEAD_DIM)


def _trunk(x, mods, P, cache, tm, tq, L, nc):
    b, t, d = x.shape
    dm = D_MIX
    sample = cache is not None
    new = {}
    za, q, k, v, logf = _proj(x, mods[0], P["norm_mix_g"][0], P["ab_w"], (P["a_cols"], dm, dm, dm), tm,
                              wf=P["ab_wf"], bf=P["fox_bf"])
    ac = P["a_cols"]
    if sample:
        nb, tn = cache["nb"], cache["tn"]
        za_b = za.reshape(nb, tn, ac)
        shift0 = cache["rwkv_shift"][0]
        s0 = _to_cat(cache["rwkv"][0])
    else:
        nb, tn = b, t
        za_b = za
        shift0 = jnp.zeros((nb, ac), F32)
        s0 = jnp.zeros((nb, N_GROUPS, HEAD_DIM, GROUP), F32)
    ya, st = _rwkv(za_b, shift0, s0, P["rwkv"], L, nc)
    new["rwkv"] = _from_cat(st)[None]
    new["rwkv_shift"] = za_b[:, -1][None]
    qb, kb, vb = (a.reshape(nb, tn, dm) for a in (q, k, v))
    logf_b = logf.reshape(nb, tn, N_HEADS)
    logf_t = jnp.swapaxes(logf_b, 1, 2)
    if sample:
        ck = cache["fox_k"][0].reshape(nb, -1, dm)
        cv = cache["fox_v"][0].reshape(nb, -1, dm)
        fc = _cumsum_time(jnp.swapaxes(cache["fox_logf"][0], 1, 2))
        yb = _cached_attn(qb, kb, vb, ck, cv, fc, logf_t, fox=True)
    else:
        yb = _fox_prompt(qb, kb, vb, _cumsum_time(logf_t), tq)
    new["fox_k"] = kb.reshape(1, nb, tn, N_HEADS, HEAD_DIM)
    new["fox_v"] = vb.reshape(1, nb, tn, N_HEADS, HEAD_DIM)
    new["fox_logf"] = logf_b[None]
    fix = None
    if sample:
        buf = cache["ffn_conv"][0]
        zero = jnp.zeros((nb, tn - 2, buf.shape[-1]), F32)
        fix = (jnp.concatenate([buf[:, 1:2], buf[:, 0:1] * 0, zero], axis=1).reshape(1, nb * tn, -1),
               jnp.concatenate([buf, zero], axis=1).reshape(1, nb * tn, -1))
    res = _tail(x, ya.reshape(b, t, dm), yb.reshape(b, t, dm), mods[0], mods[1], P["ab_wo"],
                P["norm_ffn_g"][0], P["wup"][0], P["conv_w"][0], P["conv_b"][0], P["wdn"][0],
                P["final_g"], False, tm, fix=fix, period=tn)
    x = res[0]
    conv0 = res[1].reshape(nb, tn, -1)[:, tn - 2:] if sample else res[1]
    q, k, v, zq, zf, zi, zg = _proj(x, mods[2], P["norm_mix_g"][1], P["cd_w"], (dm,) * 7, tm)
    qb, kb, vb, zq, zf, zi, zg = (a.reshape(nb, tn, dm) for a in (q, k, v, zq, zf, zi, zg))
    if sample:
        ck = cache["chunk_k"][0].reshape(nb, -1, dm)
        cv = cache["chunk_v"][0].reshape(nb, -1, dm)
        yc = _cached_attn(qb, kb, vb, ck, cv, P["band_bias_c"], P["band_bias_n"], fox=False)
        new["chunk_k"] = kb.reshape(1, nb, tn, N_HEADS, HEAD_DIM)
        new["chunk_v"] = vb.reshape(1, nb, tn, N_HEADS, HEAD_DIM)
        s0 = _to_cat(jnp.swapaxes(cache["hgrn"][0], -1, -2))
    else:
        yc = _band_prompt(qb, kb, vb, P["rel_bias"], 4 * CHUNK)
        keep = min(BAND, tn)
        new["chunk_k"] = kb[:, tn - keep:].reshape(1, nb, keep, N_HEADS, HEAD_DIM)
        new["chunk_v"] = vb[:, tn - keep:].reshape(1, nb, keep, N_HEADS, HEAD_DIM)
        s0 = jnp.zeros((nb, N_GROUPS, HEAD_DIM, GROUP), F32)
    yd, st = _hgrn(zq, zf, zi, zg, s0, P["hgrn_lb"], P["hgrn_norm_g"], L)
    new["hgrn"] = jnp.swapaxes(_from_cat(st), -1, -2)[None]
    if sample:
        buf = cache["ffn_conv"][1]
        fix = (jnp.concatenate([buf[:, 1:2], buf[:, 0:1] * 0, zero], axis=1).reshape(1, nb * tn, -1),
               jnp.concatenate([buf, zero], axis=1).reshape(1, nb * tn, -1))
    res = _tail(x, yc.reshape(b, t, dm), yd.reshape(b, t, dm), mods[2], mods[3], P["cd_wo"],
                P["norm_ffn_g"][1], P["wup"][1], P["conv_w"][1], P["conv_b"][1], P["wdn"][1],
                P["final_g"], True, tm, fix=fix, period=tn)
    conv1 = res[1].reshape(nb, tn, -1)[:, tn - 2:] if sample else res[1]
    new["ffn_conv"] = jnp.stack([conv0, conv1])
    y = res[0].reshape(nb, tn, d)
    return y, new


def kernel(x_prompt, x_sample, c_prompt, c_sample, cache_fox_k, cache_fox_v, cache_fox_logf, state_rwkv,
           state_rwkv_shift, cache_chunk_k, cache_chunk_v, state_hgrn, state_ffn_conv, ada_w, ada_b,
           norm_mix_g, norm_ffn_g, ab_w_in, rwkv_mu, rwkv_w0, rwkv_w2, rwkv_a0, rwkv_a2, rwkv_g2,
           rwkv_k_k, rwkv_k_a, rwkv_r_k, rwkv_lnx_g, rwkv_lnx_b, fox_b_f, ab_w_out, cd_w_in,
           chunk_rel_bias, hgrn_lb_table, hgrn_norm_g, cd_w_out, ffn_w_up, ffn_conv_w, ffn_conv_b,
           ffn_w_down, final_norm_g):
    bp, tp, d = x_prompt.shape
    bs, ts, _ = x_sample.shape
    depth = ada_w.shape[0]
    dm = D_MIX
    a_cols = rwkv_mu.shape[1]
    n_lw, n_la, n_lg = rwkv_w2.shape[1], rwkv_a2.shape[1], rwkv_g2.shape[1]

    c_all = jnp.concatenate([c_prompt, c_sample], axis=0)
    mods = _adaln(c_all, ada_w.reshape(depth * 2, d, 3 * d), ada_b.reshape(depth * 2, 3 * d))
    mods_p = [mods[i, :bp].reshape(bp, 1, 3 * d) for i in range(depth * 2)]
    mods_s = [jnp.repeat(mods[i, bp:], ts, axis=0).reshape(1, bs * ts, 3 * d) for i in range(depth * 2)]

    wl = jnp.zeros((n_lw + n_la + n_lg, 3 * dm), F32)
    wl = wl.at[:n_lw, 0:dm].set(rwkv_w2[0])
    wl = wl.at[n_lw:n_lw + n_la, dm:2 * dm].set(rwkv_a2[0])
    wl = wl.at[n_lw + n_la:, 2 * dm:].set(rwkv_g2[0])
    sm = jax.nn.softmax(hgrn_lb_table.astype(F32), axis=0)
    lb = (jnp.cumsum(sm, axis=0) - sm[0])[1]
    wf = jnp.zeros((d, 128), F32).at[:, :N_HEADS].set(ab_w_in[0][:, a_cols + 3 * dm:]).astype(BF16)
    bf = jnp.zeros((1, 128), F32).at[0, :N_HEADS].set(fox_b_f[0])
    p_c = cache_chunk_k.shape[2]
    bias_s = _toeplitz_bias(chunk_rel_bias[0], ts, p_c + ts, p_c).reshape(N_HEADS * ts, p_c + ts)

    P = {
        "a_cols": a_cols,
        "norm_mix_g": norm_mix_g, "norm_ffn_g": norm_ffn_g, "final_g": final_norm_g,
        "ab_w": ab_w_in[0][:, :a_cols + 3 * dm].astype(BF16), "ab_wf": wf, "fox_bf": bf,
        "ab_wo": ab_w_out[0].astype(BF16),
        "cd_w": cd_w_in[0].astype(BF16), "cd_wo": cd_w_out[0].astype(BF16),
        "wup": [ffn_w_up[i].astype(BF16) for i in range(depth)],
        "wdn": [ffn_w_down[i].astype(BF16) for i in range(depth)],
        "conv_w": ffn_conv_w, "conv_b": ffn_conv_b,
        "rwkv": {"mu": rwkv_mu[0], "w0": rwkv_w0[0], "a0": rwkv_a0[0], "wl": wl.astype(BF16), "k_k": rwkv_k_k[0],
                 "k_a": rwkv_k_a[0], "r_k": rwkv_r_k[0], "lnx_g": rwkv_lnx_g[0], "lnx_b": rwkv_lnx_b[0]},
        "rel_bias": chunk_rel_bias[0], "band_bias_c": bias_s[:, :p_c], "band_bias_n": bias_s[:, p_c:],
        "hgrn_lb": lb, "hgrn_norm_g": hgrn_norm_g[0],
    }
    cache = {"nb": bs, "tn": ts, "fox_k": cache_fox_k, "fox_v": cache_fox_v, "fox_logf": cache_fox_logf,
             "rwkv": state_rwkv, "rwkv_shift": state_rwkv_shift, "chunk_k": cache_chunk_k,
             "chunk_v": cache_chunk_v, "hgrn": state_hgrn, "ffn_conv": state_ffn_conv}

    y_p, sp = _trunk(x_prompt, mods_p, P, None, tm=512, tq=512, L=CHUNK, nc=4)
    y_s, ss = _trunk(x_sample.reshape(1, bs * ts, d), mods_s, P, cache, tm=bs * ts, tq=None, L=ts, nc=1)
    names = ("fox_k", "fox_v", "fox_logf", "rwkv", "rwkv_shift", "chunk_k", "chunk_v", "hgrn", "ffn_conv")
    return (y_p, y_s) + tuple(sp[n] for n in names) + tuple(ss[n] for n in names)
```

```python
import functools
import math

import jax
import jax.numpy as jnp
from jax import lax
from jax.experimental import pallas as pl
from jax.experimental.pallas import tpu as pltpu

F32 = jnp.float32
BF16 = jnp.bfloat16
HI = lax.Precision.HIGHEST

HEAD_DIM = 64
N_HEADS = 8
D_MIX = N_HEADS * HEAD_DIM
GROUP = 256
N_GROUPS = D_MIX // GROUP
HEADS_PER_GROUP = GROUP // HEAD_DIM
SUB = 16
CHUNK = 64
BAND = 512
REL_CLIP = 128
RMS_EPS = 1e-6
GN_EPS = 64e-5
ATTN_SCALE = HEAD_DIM ** -0.5
NEG = -1e30
VMEM_LIMIT = 56 * 1024 * 1024


def _dot(a, b, prec=None):
    return jnp.dot(a, b, preferred_element_type=F32, precision=prec)


def _dot_nt(a, b, prec=None):
    return lax.dot_general(a, b, (((1,), (1,)), ((), ())), preferred_element_type=F32, precision=prec)


def _dot_tn(a, b, prec=None):
    return lax.dot_general(a, b, (((0,), (0,)), ((), ())), preferred_element_type=F32, precision=prec)


_NN = ((1,), (0,))
_NT = ((1,), (1,))
_TN = ((0,), (0,))


def _sp(x, n):
    hi = x.astype(BF16)
    if n == 1:
        return (hi,)
    r = x - hi.astype(F32)
    mid = r.astype(BF16)
    if n == 2:
        return (hi, mid)
    return (hi, mid, (r - mid.astype(F32)).astype(BF16))


def bf(x):
    return (x.astype(BF16),)


def _mm(a, b, dims=_NN):
    n = max(len(a), len(b))
    out = None
    for i, ai in enumerate(a):
        for j, bj in enumerate(b):
            if i + j < n:
                d = lax.dot_general(ai, bj, (dims, ((), ())), preferred_element_type=F32)
                out = d if out is None else out + d
    return out


def _head_sum(x, bo_group, terms=2):
    return jnp.concatenate(
        [_mm(_sp(x[:, gi * GROUP:(gi + 1) * GROUP], terms), (bo_group,)) for gi in range(N_GROUPS)], axis=1)


def _sigmoid(x):
    return jax.nn.sigmoid(x)


def _silu(x):
    return x * jax.nn.sigmoid(x)


def _softplus(x):
    return jnp.maximum(x, 0.0) + jnp.log1p(jnp.exp(-jnp.abs(x)))


def _log_sigmoid(x):
    return -_softplus(-x)


def _iota(shape, dim):
    return lax.broadcasted_iota(jnp.int32, shape, dim)


def _const_spec(shape):
    nd = len(shape)
    return pl.BlockSpec(shape, lambda *_: (0,) * nd, pipeline_mode=pl.Buffered(1))


def _params(sem):
    return pltpu.CompilerParams(dimension_semantics=sem, vmem_limit_bytes=VMEM_LIMIT)


def _norm_mod(x, g, shift, scale):
    xn = x * lax.rsqrt(jnp.mean(x * x, axis=-1, keepdims=True) + RMS_EPS) * g
    return xn * (1.0 + scale) + shift


def _block_ones(n, blk):
    i = jnp.arange(n) // blk
    return (i[:, None] == i[None, :]).astype(BF16)


def _bd(x, rows_per_head, cols_per_head):
    n = HEADS_PER_GROUP
    t = jnp.concatenate([x] * n, axis=0)
    r = _iota(t.shape, 0) // rows_per_head
    c = _iota(t.shape, 1) // cols_per_head
    return jnp.where(r == c, t, 0.0)


def _unbd(m):
    out = m[0:HEAD_DIM]
    for h in range(1, HEADS_PER_GROUP):
        out = out + m[h * HEAD_DIM:(h + 1) * HEAD_DIM]
    return out


def _adaln_kernel(c_ref, w_ref, b_ref, o_ref):
    o_ref[0] = _dot(_silu(c_ref[...]), w_ref[0], HI) + b_ref[0]


def _adaln(c_all, ada_w, ada_b):
    n, d, d3 = ada_w.shape
    bt = c_all.shape[0]
    nt = d3 // d
    return pl.pallas_call(
        _adaln_kernel,
        grid=(n, nt),
        in_specs=[pl.BlockSpec((bt, d), lambda i, j: (0, 0)),
                  pl.BlockSpec((1, d, d), lambda i, j: (i, 0, j)),
                  pl.BlockSpec((1, 1, d), lambda i, j: (i, 0, j))],
        out_specs=pl.BlockSpec((1, bt, d), lambda i, j: (i, 0, j)),
        out_shape=jax.ShapeDtypeStruct((n, bt, d3), F32),
        compiler_params=_params(("parallel", "parallel")),
        name="adaln",
    )(c_all, ada_w, ada_b.reshape(n, 1, d3))


def _proj_kernel(*refs, splits, fox):
    x_ref, mod_ref, g_ref, w_ref = refs[:4]
    pos = 4
    if fox:
        wf_ref, bf_ref = refs[4:6]
        pos = 6
    outs = refs[pos:]
    d = x_ref.shape[-1]
    mod = mod_ref[0]
    h = _norm_mod(x_ref[0], g_ref[...], mod[:, :d], mod[:, d:2 * d]).astype(BF16)
    off = 0
    for o_ref, n in zip(outs, splits):
        o_ref[0] = _dot(h, w_ref[:, off:off + n])
        off += n
    if fox:
        zf = _dot(h, wf_ref[...]) + bf_ref[...]
        outs[len(splits)][0] = _log_sigmoid(zf)[:, :N_HEADS]


def _proj(x, mod, g, w_bf, splits, tm, wf=None, bf=None):
    b, t, d = x.shape
    r = mod.shape[1]
    fox = wf is not None
    mod_spec = (pl.BlockSpec((1, 1, 3 * d), lambda i, j: (i, 0, 0)) if r == 1
                else pl.BlockSpec((1, tm, 3 * d), lambda i, j: (i, j, 0)))
    in_specs = [pl.BlockSpec((1, tm, d), lambda i, j: (i, j, 0)), mod_spec,
                _const_spec((1, d)), _const_spec(w_bf.shape)]
    args = [x, mod, g.reshape(1, d), w_bf]
    widths = list(splits)
    if fox:
        in_specs += [_const_spec(wf.shape), _const_spec(bf.shape)]
        args += [wf, bf]
        widths.append(N_HEADS)
    return pl.pallas_call(
        functools.partial(_proj_kernel, splits=tuple(splits), fox=fox),
        grid=(b, t // tm),
        in_specs=in_specs,
        out_specs=[pl.BlockSpec((1, tm, n), lambda i, j: (i, j, 0)) for n in widths],
        out_shape=[jax.ShapeDtypeStruct((b, t, n), F32) for n in widths],
        compiler_params=_params(("parallel", "parallel")),
        name="proj",
    )(*args)


def _tail_kernel(*refs, sample, final, period, cb, down_group):
    (x_ref, ya_ref, yb_ref, m1_ref, m2_ref, wo_ref, g_ref, wup_ref, cw_ref, cbias_ref,
     wdn_ref, fg_ref) = refs[:12]
    if sample:
        fix1_ref, fix2_ref, o_ref, u_ref, g_sc = refs[12:]
    else:
        o_ref, cs_ref, g_sc, tail_sc = refs[12:]
    tm, d = x_ref.shape[1], x_ref.shape[2]
    ff = wdn_ref.shape[0]
    dm = ya_ref.shape[2]
    m1 = m1_ref[0]
    m2 = m2_ref[0]
    ymix = (_dot(ya_ref[0].astype(BF16), wo_ref[0:dm, :])
            + _dot(yb_ref[0].astype(BF16), wo_ref[dm:2 * dm, :]))
    x1 = x_ref[0] + m1[:, 2 * d:3 * d] * ymix
    h = _norm_mod(x1, g_ref[...], m2[:, :d], m2[:, d:2 * d]).astype(BF16)
    rows = _iota((tm, 1), 0)
    if sample:
        tpos = rows & (period - 1)
    else:
        t = pl.program_id(1)

        @pl.when(t == 0)
        def _():
            tail_sc[...] = jnp.zeros_like(tail_sc)

    def up(j):
        return [_dot(h, wup_ref[:, half * ff + j * cb:half * ff + (j + 1) * cb]) for half in range(2)]

    def conv(u, u1, u2, c0):
        cw = cw_ref[:, c0:c0 + cb]
        return cbias_ref[:, c0:c0 + cb] + cw[0:1] * u2 + cw[1:2] * u1 + cw[2:3] * u

    def gated(j, us):
        halves = []
        for half, u in enumerate(us):
            c0 = half * ff + j * cb
            if sample:
                u1 = jnp.where(tpos == 0, fix1_ref[0, :, c0:c0 + cb], pltpu.roll(u, 1, 0))
                u2 = jnp.where(tpos < 2, fix2_ref[0, :, c0:c0 + cb], pltpu.roll(u, 2, 0))
                u_ref[0, :, c0:c0 + cb] = u
                halves.append(conv(u, u1, u2, c0))
            else:
                edge = jnp.concatenate([tail_sc[:, c0:c0 + cb], u[0:8]], axis=0)
                first = conv(edge, pltpu.roll(edge, 1, 0), pltpu.roll(edge, 2, 0), c0)[8:16]
                tail_sc[:, c0:c0 + cb] = u[tm - 8:tm, :]
                main = conv(u, pltpu.roll(u, 1, 0), pltpu.roll(u, 2, 0), c0)
                halves.append(jnp.concatenate([first, main[8:]], axis=0))
        return (_silu(halves[0]) * halves[1]).astype(BF16)

    acc = None
    n_blk = ff // cb
    us = up(0)
    j0 = 0
    for j in range(n_blk):
        us_next = up(j + 1) if j + 1 < n_blk else None
        g_sc[:, j * cb:(j + 1) * cb] = gated(j, us)
        if (j + 1 - j0) == down_group or j + 1 == n_blk:
            part = _dot(g_sc[:, j0 * cb:(j + 1) * cb], wdn_ref[j0 * cb:(j + 1) * cb, :])
            acc = part if acc is None else acc + part
            j0 = j + 1
        us = us_next
    out = x1 + m2[:, 2 * d:3 * d] * acc
    if final:
        out = out * lax.rsqrt(jnp.mean(out * out, axis=-1, keepdims=True) + RMS_EPS) * fg_ref[...]
    o_ref[0] = out
    if not sample:
        @pl.when(t == pl.num_programs(1) - 1)
        def _():
            cs_ref[0] = tail_sc[6:8, :]


def _tail(x, ya, yb, mod1, mod2, wo_bf, g, wup_bf, conv_w, conv_b, wdn_bf, fg, final, tm,
          fix=None, period=None):
    b, t, d = x.shape
    ff = wdn_bf.shape[0]
    dm = ya.shape[2]
    sample = fix is not None
    r = mod1.shape[1]
    mod_spec = (pl.BlockSpec((1, 1, 3 * d), lambda i, j: (i, 0, 0)) if r == 1
                else pl.BlockSpec((1, tm, 3 * d), lambda i, j: (i, j, 0)))
    row_spec = lambda n: pl.BlockSpec((1, tm, n), lambda i, j: (i, j, 0))
    in_specs = [row_spec(d), row_spec(dm), row_spec(dm), mod_spec, mod_spec,
                _const_spec(wo_bf.shape), _const_spec((1, d)), _const_spec(wup_bf.shape),
                _const_spec(conv_w.shape), _const_spec((1, 2 * ff)), _const_spec(wdn_bf.shape),
                _const_spec((1, d))]
    args = [x, ya, yb, mod1, mod2, wo_bf, g.reshape(1, d), wup_bf, conv_w, conv_b.reshape(1, 2 * ff),
            wdn_bf, fg.reshape(1, d)]
    if sample:
        in_specs += [row_spec(2 * ff), row_spec(2 * ff)]
        args += list(fix)
        out_specs = [row_spec(d), row_spec(2 * ff)]
        out_shape = [jax.ShapeDtypeStruct((b, t, d), F32), jax.ShapeDtypeStruct((b, t, 2 * ff), F32)]
        scratch = [pltpu.VMEM((tm, ff), BF16)]
    else:
        out_specs = [row_spec(d), pl.BlockSpec((1, 2, 2 * ff), lambda i, j: (i, 0, 0))]
        out_shape = [jax.ShapeDtypeStruct((b, t, d), F32), jax.ShapeDtypeStruct((b, 2, 2 * ff), F32)]
        scratch = [pltpu.VMEM((tm, ff), BF16), pltpu.VMEM((8, 2 * ff), F32)]
    return pl.pallas_call(
        functools.partial(_tail_kernel, sample=sample, final=final, period=period, cb=256, down_group=4),
        grid=(b, t // tm),
        in_specs=in_specs,
        out_specs=out_specs,
        out_shape=out_shape,
        scratch_shapes=scratch,
        compiler_params=_params(("parallel", "arbitrary")),
        name="tail",
    )(*args)


def _rwkv_kernel(z_ref, shift0_ref, s0_ref, mu_ref, w0_ref, a0_ref, wl_ref, kk_ref, ka_ref, rk_ref,
                 lng_ref, lnb_ref, bo_ref, y_ref, st_ref, prev_sc, st_sc, *, L, nc):
    t = pl.program_id(1)
    tb = nc * L

    @pl.when(t == 0)
    def _():
        prev_sc[...] = shift0_ref[0]
        for gi in range(N_GROUPS):
            st_sc[gi] = _bd(s0_ref[0, gi], HEAD_DIM, HEAD_DIM)

    z = z_ref[0]
    rows = _iota((tb, 1), 0)
    z_prev = jnp.where(rows == 0, prev_sc[...], pltpu.roll(z, 1, 0))
    prev_sc[...] = z[tb - 1:tb, :]
    zs = z + mu_ref[...] * (z_prev - z)
    dm = D_MIX
    r = zs[:, 0:dm]
    k = zs[:, dm:2 * dm]
    v = zs[:, 2 * dm:3 * dm]
    lr = zs[:, 3 * dm:]
    nl = lr.shape[1]
    lane = _iota((1, nl), 1)
    act = jnp.where(lane < 64, jnp.tanh(lr), jnp.where(lane < 128, lr, _sigmoid(lr)))
    lo = _mm(bf(act), (wl_ref[...],))
    w = -_softplus(-(w0_ref[...] + lo[:, 0:dm])) - 0.5
    lw = -jnp.exp(w)
    a = _sigmoid(a0_ref[...] + lo[:, dm:2 * dm])
    g = lo[:, 2 * dm:3 * dm]
    bo = bo_ref[...]
    kk = k * kk_ref[...]
    k2 = k * (1.0 + (a - 1.0) * ka_ref[...])
    sums = _head_sum(jnp.concatenate([kk * kk, r * k2 * rk_ref[...]], axis=0), bo)
    kk = kk / jnp.maximum(jnp.sqrt(sums[:tb]), 1e-12)
    bonus = sums[tb:]
    am = -kk
    bm = kk * a

    ri = _iota((tb, tb), 0)
    ci_ = _iota((tb, tb), 1)
    tri = ((ci_ <= ri) & (ci_ // L == ri // L)).astype(BF16)
    c = _mm((tri,), _sp(lw, 3))
    c_last = jnp.concatenate(
        [jnp.broadcast_to(c[(j + 1) * L - 1:(j + 1) * L, :], (L, c.shape[1])) for j in range(nc)], axis=0)
    a_t = am * jnp.exp(c - lw)
    r_t = r * jnp.exp(c)
    g_inv = jnp.exp(-c)
    k_t = k2 * g_inv
    b_t = bm * g_inv
    g_end = jnp.exp(c_last - c)
    k_end = k2 * g_end
    b_end = bm * g_end
    g_last = jnp.exp(c_last)

    cw = HEADS_PER_GROUP * L
    e_tile = ((_iota((L, cw), 1) % L) == _iota((L, cw), 0)).astype(BF16)
    tmask = (_iota((GROUP, cw), 0) // HEAD_DIM) == (_iota((GROUP, cw), 1) // L)
    col_s = _iota((L, cw), 1) % L
    row_t = _iota((L, cw), 0)
    strict = col_s < row_t
    incl = col_s <= row_t
    bdmask = (_iota((GROUP, GROUP), 0) // HEAD_DIM) == (_iota((GROUP, GROUP), 1) // HEAD_DIM)
    n_dbl = int(math.log2(L))

    def bd(x, n, cols_per_head=HEAD_DIM):
        return tuple(_bd(p, L, cols_per_head) for p in _sp(x, n))

    eye_cat = (col_s == row_t).astype(F32)

    insts = [(cj, gi) for cj in range(nc) for gi in range(N_GROUPS)]
    rsl = {cj: slice(cj * L, (cj + 1) * L) for cj in range(nc)}
    gsl = {gi: slice(gi * GROUP, (gi + 1) * GROUP) for gi in range(N_GROUPS)}
    bk_t = {cj: _mm(bf(jnp.concatenate([b_t[rsl[cj]], k_t[rsl[cj]]], axis=1)), (e_tile,), _TN) for cj in range(nc)}
    ar = {(cj, gi): bf(jnp.concatenate([a_t[rsl[cj], gsl[gi]], r_t[rsl[cj], gsl[gi]]], axis=0)) for cj, gi in insts}
    arb = {(cj, gi): _mm(ar[cj, gi], bf(jnp.where(tmask, bk_t[cj][gsl[gi]], 0.0))) for cj, gi in insts}
    ark = {(cj, gi): _mm(ar[cj, gi], bf(jnp.where(tmask, bk_t[cj][dm + gi * GROUP:dm + (gi + 1) * GROUP], 0.0)))
           for cj, gi in insts}
    p = {k_: jnp.where(strict, arb[k_][:L], 0.0) for k_ in insts}
    m_rb = {k_: bf(jnp.where(incl, arb[k_][L:], 0.0)) for k_ in insts}
    nmv = {(cj, gi): _mm(bf(jnp.concatenate([jnp.where(strict, ark[cj, gi][:L], 0.0),
                                              jnp.where(incl, ark[cj, gi][L:], 0.0)], axis=0)),
                         bd(v[rsl[cj], gsl[gi]], 1)) for cj, gi in insts}
    tm = {k_: eye_cat + p[k_] for k_ in insts}
    for i in range(n_dbl - 1):
        for k_ in insts:
            w = bd(p[k_], 1, L)
            if i == 0:
                p[k_] = _mm(bf(p[k_]), w)
            else:
                both = _mm(bf(jnp.concatenate([p[k_], tm[k_]], axis=0)), w)
                p[k_] = both[:L]
                tm[k_] = tm[k_] + both[L:]
    tm = {k_: bf(tm[k_] + _mm(bf(tm[k_]), bd(p[k_], 1, L))) for k_ in insts}

    groups = range(N_GROUPS)
    state = [st_sc[gi] for gi in groups]
    y_rows = []
    for cj in range(nc):
        rs = rsl[cj]
        ars = [_mm(ar[cj, gi], bf(state[gi]), _NT) for gi in groups]
        u = [_mm(tm[cj, gi], bd(ars[gi][:L] + nmv[cj, gi][:L], 1)) for gi in groups]
        ys = [ars[gi][L:] + nmv[cj, gi][L:] + _mm(m_rb[cj, gi], bd(u[gi], 1)) for gi in groups]
        upd = [_mm(bf(jnp.concatenate([v[rs, gsl[gi]], u[gi]], axis=0)),
                   bf(jnp.concatenate([k_end[rs, gsl[gi]], b_end[rs, gsl[gi]]], axis=0)), _TN) for gi in groups]
        state = [state[gi] * g_last[rs, gsl[gi]][0:1] + jnp.where(bdmask, upd[gi], 0.0) for gi in groups]
        y_rows.append(jnp.concatenate(ys, axis=1))
    for gi in groups:
        st_sc[gi] = state[gi]

    y = jnp.concatenate(y_rows, axis=0) if nc > 1 else y_rows[0]
    inv = 1.0 / HEAD_DIM
    mean = _head_sum(y, bo) * inv
    dlt = y - mean
    var = _head_sum(dlt * dlt, bo) * inv
    yn = dlt * lax.rsqrt(var + GN_EPS) * lng_ref[...] + lnb_ref[...]
    y_ref[0] = (yn + bonus * v) * g

    @pl.when(t == pl.num_programs(1) - 1)
    def _():
        for gi in range(N_GROUPS):
            st_ref[0, gi] = _unbd(st_sc[gi])


def _state_spec():
    return pl.BlockSpec((1, N_GROUPS, HEAD_DIM, GROUP), lambda i, j: (i, 0, 0, 0))


def _rwkv(za, shift0, s0_cat, pr, L, nc):
    b, t, ac = za.shape
    dm = D_MIX
    tb = L * nc
    row = lambda a: a.reshape(1, -1)
    consts = [row(pr["mu"]), row(pr["w0"]), row(pr["a0"]), pr["wl"], row(pr["k_k"]), row(pr["k_a"]),
              row(pr["r_k"]), row(pr["lnx_g"]), row(pr["lnx_b"]), _block_ones(GROUP, HEAD_DIM)]
    return pl.pallas_call(
        functools.partial(_rwkv_kernel, L=L, nc=nc),
        grid=(b, t // tb),
        in_specs=[pl.BlockSpec((1, tb, ac), lambda i, j: (i, j, 0)),
                  pl.BlockSpec((1, 1, ac), lambda i, j: (i, 0, 0)), _state_spec()]
                 + [_const_spec(c.shape) for c in consts],
        out_specs=[pl.BlockSpec((1, tb, dm), lambda i, j: (i, j, 0)), _state_spec()],
        out_shape=[jax.ShapeDtypeStruct((b, t, dm), F32),
                   jax.ShapeDtypeStruct((b, N_GROUPS, HEAD_DIM, GROUP), F32)],
        scratch_shapes=[pltpu.VMEM((1, ac), F32), pltpu.VMEM((N_GROUPS, GROUP, GROUP), F32)],
        compiler_params=_params(("parallel", "arbitrary")),
        name="rwkv7",
    )(za, shift0.reshape(b, 1, ac), s0_cat, *consts)


def _hgrn_kernel(zq_ref, zf_ref, zi_ref, zg_ref, s0_ref, lb_ref, ng_ref, bo_ref, y_ref, st_ref, st_sc, *, L):
    t = pl.program_id(1)

    @pl.when(t == 0)
    def _():
        for gi in range(N_GROUPS):
            st_sc[gi] = _bd(s0_ref[0, gi], HEAD_DIM, HEAD_DIM)

    dm = D_MIX
    q = _silu(zq_ref[0])
    xf = zf_ref[0]
    v = zi_ref[0]
    lb = lb_ref[...]
    la = jnp.log(lb)
    lc = jnp.log1p(-lb) + _log_sigmoid(xf)
    logf = jnp.maximum(la, lc) + jnp.log1p(jnp.exp(-jnp.abs(la - lc)))
    k = (1.0 - lb) * _sigmoid(-xf)
    bo = bo_ref[...]

    tri = (_iota((L, L), 1) <= _iota((L, L), 0)).astype(BF16)
    b = _mm((tri,), _sp(logf, 3))
    b_last = b[L - 1:L, :]
    qe = q * jnp.exp(b)
    k_end = k * jnp.exp(b_last - b)
    g_last = jnp.exp(b_last)

    o = jnp.concatenate([_mm(bf(qe[:, gi * GROUP:(gi + 1) * GROUP]), bf(st_sc[gi]), _NT)
                         for gi in range(N_GROUPS)], axis=1)

    n_sub = L // SUB
    nst = N_HEADS * SUB
    stackmask = (_iota((nst, dm), 0) // SUB) == (_iota((nst, dm), 1) // HEAD_DIM)
    sub_rows = _iota((SUB, 1), 0)
    subs = range(n_sub)
    rsub = [slice(i * SUB, (i + 1) * SUB) for i in subs]
    xs = {i: [] for i in subs}
    for s in range(SUB):
        for i in subs:
            bi = b[rsub[i]]
            e = jnp.exp(jnp.where(sub_rows >= s, bi - bi[s:s + 1], NEG))
            xs[i].append(q[rsub[i]] * e * k[rsub[i]][s:s + 1])
    att = {i: _head_sum(jnp.concatenate(xs[i], axis=0), bo, terms=1) for i in subs}
    later = [i for i in subs if i > 0]
    qst, kh = {}, {}
    for i in later:
        lo_ = i * SUB
        bref = b[lo_ - 1:lo_]
        qh = q[rsub[i]] * jnp.exp(b[rsub[i]] - bref)
        qst[i] = bf(jnp.where(stackmask, jnp.concatenate([qh] * N_HEADS, axis=0), 0.0))
        kh[i] = bf(k[0:lo_] * jnp.exp(bref - b[0:lo_]))
    att2 = {i: _mm(qst[i], kh[i], _NT) for i in later}
    ov = {i: jnp.where(stackmask, _mm(bf(att2[i]), bf(v[0:i * SUB])), 0.0) for i in later}
    o_subs = []
    for i in subs:
        vi = v[rsub[i]]
        oi = att[i][0:SUB] * vi[0:1]
        for s in range(1, SUB):
            oi = oi + att[i][s * SUB:(s + 1) * SUB] * vi[s:s + 1]
        if i > 0:
            for hh in range(N_HEADS):
                oi = oi + ov[i][hh * SUB:(hh + 1) * SUB]
        o_subs.append(oi)
    o = o + (jnp.concatenate(o_subs, axis=0) if n_sub > 1 else o_subs[0])

    bdmask = (_iota((GROUP, GROUP), 0) // HEAD_DIM) == (_iota((GROUP, GROUP), 1) // HEAD_DIM)
    for gi in range(N_GROUPS):
        sl = slice(gi * GROUP, (gi + 1) * GROUP)
        upd = _mm(bf(v[:, sl]), bf(k_end[:, sl]), _TN)
        st_sc[gi] = st_sc[gi] * g_last[:, sl] + jnp.where(bdmask, upd, 0.0)

    ms = _head_sum(o * o, bo) * (1.0 / HEAD_DIM)
    y_ref[0] = o * lax.rsqrt(ms + RMS_EPS) * ng_ref[...] * _silu(zg_ref[0])

    @pl.when(t == pl.num_programs(1) - 1)
    def _():
        for gi in range(N_GROUPS):
            st_ref[0, gi] = _unbd(st_sc[gi])


def _hgrn(zq, zf, zi, zg, s0_cat, lb, norm_g, L):
    b, t, dm = zq.shape
    consts = [lb.reshape(1, dm), norm_g.reshape(1, dm), _block_ones(GROUP, HEAD_DIM)]
    row_spec = pl.BlockSpec((1, L, dm), lambda i, j: (i, j, 0))
    return pl.pallas_call(
        functools.partial(_hgrn_kernel, L=L),
        grid=(b, t // L),
        in_specs=[row_spec] * 4 + [_state_spec()] + [_const_spec(c.shape) for c in consts],
        out_specs=[row_spec, _state_spec()],
        out_shape=[jax.ShapeDtypeStruct((b, t, dm), F32),
                   jax.ShapeDtypeStruct((b, N_GROUPS, HEAD_DIM, GROUP), F32)],
        scratch_shapes=[pltpu.VMEM((N_GROUPS, GROUP, GROUP), F32)],
        compiler_params=_params(("parallel", "arbitrary")),
        name="hgrn2",
    )(zq, zf, zi, zg, s0_cat, *consts)


def _cumsum_kernel(x_ref, o_ref):
    x = x_ref[0]
    n, w = x.shape
    nblk = n // N_HEADS
    upper = (_iota((w, w), 0) <= _iota((w, w), 1)).astype(F32)
    c = _dot(x, upper, HI)
    tot = jnp.broadcast_to(c[:, w - 1:w], (n, w))
    ri = _iota((n, n), 0)
    ci = _iota((n, n), 1)
    prior = ((ri // nblk == ci // nblk) & (ci < ri)).astype(F32)
    o_ref[0] = c + _dot(prior, tot, HI)


def _cumsum_time(logf_bht):
    b, h, t = logf_bht.shape
    w = 128
    n = h * (t // w)
    x = logf_bht.reshape(b, n, w)
    out = pl.pallas_call(
        _cumsum_kernel,
        grid=(b,),
        in_specs=[pl.BlockSpec((1, n, w), lambda i: (i, 0, 0))],
        out_specs=pl.BlockSpec((1, n, w), lambda i: (i, 0, 0)),
        out_shape=jax.ShapeDtypeStruct((b, n, w), F32),
        compiler_params=_params(("parallel",)),
        name="cumsum",
    )(x)
    return out.reshape(b, h, t)


def _fox_kernel(q_ref, k_ref, v_ref, f_ref, o_ref, kb_sc, vb_sc, *, tq):
    i = pl.program_id(2)

    @pl.when(i == 0)
    def _():
        kb_sc[...] = k_ref[0].astype(BF16)
        vb_sc[...] = v_ref[0].astype(BF16)

    q = q_ref[0] * ATTN_SCALE
    lane = _iota((1, 2 * HEAD_DIM), 1)
    causal = _iota((tq, tq), 1) <= _iota((tq, tq), 0)
    outs = []
    for hh in range(2):
        head_lane = (lane // HEAD_DIM) == hh
        qm = jnp.where(head_lane, q, 0.0).astype(BF16)

        def step(j, carry, masked, hh=hh, head_lane=head_lane, qm=qm):
            m, acc = carry
            rows = pl.ds(j * tq, tq)
            s = _dot_nt(qm, kb_sc[rows, :]) - f_ref[0, 0, j, hh:hh + 1, :]
            if masked:
                s = jnp.where(causal, s, NEG)
            m_new = jnp.maximum(m, jnp.max(s, axis=-1, keepdims=True))
            p = jnp.exp(s - m_new).astype(BF16)
            vx = jnp.where(head_lane, vb_sc[rows, :], 1.0)
            return m_new, jnp.exp(m - m_new) * acc + _dot(p, vx)

        init = (jnp.full((tq, 1), NEG, F32), jnp.zeros((tq, 2 * HEAD_DIM), F32))
        carry = lax.fori_loop(0, i, lambda j, cr: step(j, cr, False), init)
        _, acc = step(i, carry, True)
        outs.append(acc / pltpu.roll(acc, HEAD_DIM, 1))
    o_ref[0] = jnp.where((lane // HEAD_DIM) == 0, outs[0], outs[1])


def _fox_prompt(q, k, v, cum_bht, tq):
    b, t, dm = q.shape
    hp = dm // (2 * HEAD_DIM)
    nk = t // tq
    f = cum_bht.reshape(b, hp, 2, nk, tq).transpose(0, 1, 3, 2, 4)
    kv_spec = pl.BlockSpec((1, t, 2 * HEAD_DIM), lambda i, h, j: (i, 0, h))
    return pl.pallas_call(
        functools.partial(_fox_kernel, tq=tq),
        scratch_shapes=[pltpu.VMEM((t, 2 * HEAD_DIM), BF16), pltpu.VMEM((t, 2 * HEAD_DIM), BF16)],
        grid=(b, hp, nk),
        in_specs=[pl.BlockSpec((1, tq, 2 * HEAD_DIM), lambda i, h, j: (i, j, h)), kv_spec, kv_spec,
                  pl.BlockSpec((1, 1, nk, 2, tq), lambda i, h, j: (i, h, 0, 0, 0))],
        out_specs=pl.BlockSpec((1, tq, 2 * HEAD_DIM), lambda i, h, j: (i, j, h)),
        out_shape=jax.ShapeDtypeStruct((b, t, dm), F32),
        compiler_params=_params(("parallel", "parallel", "arbitrary")),
        name="fox_prompt",
    )(q, k, v, f)


def _band_kernel(q_ref, k_ref, v_ref, bias_ref, o_ref, *, tq):
    i = pl.program_id(2)
    q = q_ref[0] * ATTN_SCALE
    lane = _iota((1, 2 * HEAD_DIM), 1)
    n_piece = BAND // tq + 1
    outs = []
    for hh in range(2):
        qm = jnp.where((lane // HEAD_DIM) == hh, q, 0.0).astype(BF16)
        ss, vs = [], []
        for p in range(n_piece):
            blk = i - (n_piece - 1) + p
            start = jnp.maximum(blk, 0) * tq
            kb = k_ref[0, pl.ds(start, tq), :].astype(BF16)
            s = _dot_nt(qm, kb) + bias_ref[0, hh, :, p * tq:(p + 1) * tq]
            ss.append(jnp.where(blk >= 0, s, NEG))
            vs.append(v_ref[0, pl.ds(start, tq), :].astype(BF16))
        m = jnp.max(ss[0], axis=-1, keepdims=True)
        for s in ss[1:]:
            m = jnp.maximum(m, jnp.max(s, axis=-1, keepdims=True))
        l = jnp.zeros((tq, 1), F32)
        acc = jnp.zeros((tq, 2 * HEAD_DIM), F32)
        for s, vb in zip(ss, vs):
            p_ = jnp.exp(s - m)
            l = l + jnp.sum(p_, axis=-1, keepdims=True)
            acc = acc + _dot(p_.astype(BF16), vb)
        outs.append(acc / l)
    o_ref[0] = jnp.where((lane // HEAD_DIM) == 0, outs[0], outs[1])


def _toeplitz_bias(rel_bias, nq, nk, offset):
    h = rel_bias.shape[0]
    m = jnp.arange(nq + nk - 1)
    u = rel_bias[:, jnp.clip(offset + nq - 1 - m, -REL_CLIP, REL_CLIP) + REL_CLIP].astype(F32)
    period = nq + nk
    up = jnp.pad(u, ((0, 0), (0, 1)))
    skew = jnp.tile(up, (1, nq))[:, :nq * (period - 1)].reshape(h, nq, period - 1)
    return skew[:, :, nq - 1:]


def _band_bias_prompt(rel_bias, tq):
    nk = BAND + tq
    bias = _toeplitz_bias(rel_bias, tq, nk, BAND)
    qc = jnp.arange(tq)[:, None] // CHUNK
    kc = jnp.arange(nk)[None, :] // CHUNK
    valid = (kc >= qc) & (kc <= qc + BAND // CHUNK)
    return jnp.where(valid[None], bias, NEG)


def _band_prompt(q, k, v, rel_bias, tq):
    b, t, dm = q.shape
    hp = dm // (2 * HEAD_DIM)
    bias = _band_bias_prompt(rel_bias, tq).reshape(hp, 2, tq, BAND + tq)
    kv_spec = pl.BlockSpec((1, t, 2 * HEAD_DIM), lambda i, h, j: (i, 0, h))
    return pl.pallas_call(
        functools.partial(_band_kernel, tq=tq),
        grid=(b, hp, t // tq),
        in_specs=[pl.BlockSpec((1, tq, 2 * HEAD_DIM), lambda i, h, j: (i, j, h)), kv_spec, kv_spec,
                  pl.BlockSpec((1, 2, tq, BAND + tq), lambda i, h, j: (h, 0, 0, 0))],
        out_specs=pl.BlockSpec((1, tq, 2 * HEAD_DIM), lambda i, h, j: (i, j, h)),
        out_shape=jax.ShapeDtypeStruct((b, t, dm), F32),
        compiler_params=_params(("parallel", "parallel", "arbitrary")),
        name="band_prompt",
    )(q, k, v, bias)


def _cached_attn_kernel(*refs, fox):
    if fox:
        q_ref, kn_ref, vn_ref, ck_ref, cv_ref, fc_ref, fn_ref, o_ref = refs
    else:
        q_ref, kn_ref, vn_ref, ck_ref, cv_ref, bc_ref, bn_ref, o_ref = refs
    tn, dm = q_ref.shape[1], q_ref.shape[2]
    nst = N_HEADS * tn
    stackmask = (_iota((nst, dm), 0) // tn) == (_iota((nst, dm), 1) // HEAD_DIM)
    q = q_ref[0] * ATTN_SCALE
    qst = jnp.where(stackmask, jnp.concatenate([q] * N_HEADS, axis=0), 0.0).astype(BF16)
    s_c = _dot_nt(qst, ck_ref[0].astype(BF16))
    s_n = _dot_nt(qst, kn_ref[0].astype(BF16))
    if fox:
        expand = ((_iota((nst, N_HEADS), 0) // tn) == _iota((nst, N_HEADS), 1)).astype(F32)
        fc = fc_ref[0]
        upper = (_iota((tn, tn), 0) <= _iota((tn, tn), 1)).astype(F32)
        fnew = fc[:, fc.shape[1] - 1:] + _dot(fn_ref[0], upper, HI)
        s_c = s_c - _dot(expand, fc, HI)
        s_n = s_n - _dot(expand, fnew, HI)
        tq = _iota((nst, tn), 0) % tn
        s_n = jnp.where(_iota((nst, tn), 1) <= tq, s_n, NEG)
    else:
        s_c = s_c + bc_ref[...]
        s_n = s_n + bn_ref[...]
    m = jnp.maximum(jnp.max(s_c, axis=-1, keepdims=True), jnp.max(s_n, axis=-1, keepdims=True))
    p_c = jnp.exp(s_c - m)
    p_n = jnp.exp(s_n - m)
    l = jnp.sum(p_c, axis=-1, keepdims=True) + jnp.sum(p_n, axis=-1, keepdims=True)
    ov = _dot(p_c.astype(BF16), cv_ref[0].astype(BF16)) + _dot(p_n.astype(BF16), vn_ref[0].astype(BF16))
    ov = jnp.where(stackmask, ov / l, 0.0)
    out = ov[0:tn]
    for hh in range(1, N_HEADS):
        out = out + ov[hh * tn:(hh + 1) * tn]
    o_ref[0] = out


def _cached_attn(q, kn, vn, ck, cv, extra_c, extra_n, fox):
    b, tn, dm = q.shape
    p = ck.shape[1]
    new_spec = pl.BlockSpec((1, tn, dm), lambda i: (i, 0, 0))
    cache_spec = pl.BlockSpec((1, p, dm), lambda i: (i, 0, 0))
    if fox:
        ex_specs = [pl.BlockSpec((1, N_HEADS, p), lambda i: (i, 0, 0)),
                    pl.BlockSpec((1, N_HEADS, tn), lambda i: (i, 0, 0))]
    else:
        ex_specs = [_const_spec(extra_c.shape), _const_spec(extra_n.shape)]
    return pl.pallas_call(
        functools.partial(_cached_attn_kernel, fox=fox),
        grid=(b,),
        in_specs=[new_spec, new_spec, new_spec, cache_spec, cache_spec] + ex_specs,
        out_specs=new_spec,
        out_shape=jax.ShapeDtypeStruct((b, tn, dm), F32),
        compiler_params=_params(("parallel",)),
        name="fox_sample" if fox else "band_sample",
    )(q, kn, vn, ck, cv, extra_c, extra_n)


def _to_cat(s):
    b = s.shape[0]
    s5 = s.reshape(b, N_GROUPS, HEADS_PER_GROUP, HEAD_DIM, HEAD_DIM)
    return s5.transpose(0, 1, 3, 2, 4).reshape(b, N_GROUPS, HEAD_DIM, GROUP)


def _from_cat(c):
    b = c.shape[0]
    c5 = c.reshape(b, N_GROUPS, HEAD_DIM, HEADS_PER_GROUP, HEAD_DIM)
    return c5.transpose(0, 1, 3, 2, 4).reshape(b, N_HEADS, HEAD_DIM, HEAD_DIM)


def _trunk(x, mods, P, cache, tm, tq, L, nc):
    b, t, d = x.shape
    dm = D_MIX
    sample = cache is not None
    new = {}
    za, q, k, v, logf = _proj(x, mods[0], P["norm_mix_g"][0], P["ab_w"], (P["a_cols"], dm, dm, dm), tm,
                              wf=P["ab_wf"], bf=P["fox_bf"])
    ac = P["a_cols"]
    if sample:
        nb, tn = cache["nb"], cache["tn"]
        za_b = za.reshape(nb, tn, ac)
        shift0 = cache["rwkv_shift"][0]
        s0 = _to_cat(cache["rwkv"][0])
    else:
        nb, tn = b, t
        za_b = za
        shift0 = jnp.zeros((nb, ac), F32)
        s0 = jnp.zeros((nb, N_GROUPS, HEAD_DIM, GROUP), F32)
    ya, st = _rwkv(za_b, shift0, s0, P["rwkv"], L, nc)
    new["rwkv"] = _from_cat(st)[None]
    new["rwkv_shift"] = za_b[:, -1][None]
    qb, kb, vb = (a.reshape(nb, tn, dm) for a in (q, k, v))
    logf_b = logf.reshape(nb, tn, N_HEADS)
    logf_t = jnp.swapaxes(logf_b, 1, 2)
    if sample:
        ck = cache["fox_k"][0].reshape(nb, -1, dm)
        cv = cache["fox_v"][0].reshape(nb, -1, dm)
        fc = _cumsum_time(jnp.swapaxes(cache["fox_logf"][0], 1, 2))
        yb = _cached_attn(qb, kb, vb, ck, cv, fc, logf_t, fox=True)
    else:
        yb = _fox_prompt(qb, kb, vb, _cumsum_time(logf_t), tq)
    new["fox_k"] = kb.reshape(1, nb, tn, N_HEADS, HEAD_DIM)
    new["fox_v"] = vb.reshape(1, nb, tn, N_HEADS, HEAD_DIM)
    new["fox_logf"] = logf_b[None]
    fix = None
    if sample:
        buf = cache["ffn_conv"][0]
        zero = jnp.zeros((nb, tn - 2, buf.shape[-1]), F32)
        fix = (jnp.concatenate([buf[:, 1:2], buf[:, 0:1] * 0, zero], axis=1).reshape(1, nb * tn, -1),
               jnp.concatenate([buf, zero], axis=1).reshape(1, nb * tn, -1))
    res = _tail(x, ya.reshape(b, t, dm), yb.reshape(b, t, dm), mods[0], mods[1], P["ab_wo"],
                P["norm_ffn_g"][0], P["wup"][0], P["conv_w"][0], P["conv_b"][0], P["wdn"][0],
                P["final_g"], False, tm, fix=fix, period=tn)
    x = res[0]
    conv0 = res[1].reshape(nb, tn, -1)[:, tn - 2:] if sample else res[1]
    q, k, v, zq, zf, zi, zg = _proj(x, mods[2], P["norm_mix_g"][1], P["cd_w"], (dm,) * 7, tm)
    qb, kb, vb, zq, zf, zi, zg = (a.reshape(nb, tn, dm) for a in (q, k, v, zq, zf, zi, zg))
    if sample:
        ck = cache["chunk_k"][0].reshape(nb, -1, dm)
        cv = cache["chunk_v"][0].reshape(nb, -1, dm)
        yc = _cached_attn(qb, kb, vb, ck, cv, P["band_bias_c"], P["band_bias_n"], fox=False)
        new["chunk_k"] = kb.reshape(1, nb, tn, N_HEADS, HEAD_DIM)
        new["chunk_v"] = vb.reshape(1, nb, tn, N_HEADS, HEAD_DIM)
        s0 = _to_cat(jnp.swapaxes(cache["hgrn"][0], -1, -2))
    else:
        yc = _band_prompt(qb, kb, vb, P["rel_bias"], 4 * CHUNK)
        keep = min(BAND, tn)
        new["chunk_k"] = kb[:, tn - keep:].reshape(1, nb, keep, N_HEADS, HEAD_DIM)
        new["chunk_v"] = vb[:, tn - keep:].reshape(1, nb, keep, N_HEADS, HEAD_DIM)
        s0 = jnp.zeros((nb, N_GROUPS, HEAD_DIM, GROUP), F32)
    yd, st = _hgrn(zq, zf, zi, zg, s0, P["hgrn_lb"], P["hgrn_norm_g"], L)
    new["hgrn"] = jnp.swapaxes(_from_cat(st), -1, -2)[None]
    if sample:
        buf = cache["ffn_conv"][1]
        fix = (jnp.concatenate([buf[:, 1:2], buf[:, 0:1] * 0, zero], axis=1).reshape(1, nb * tn, -1),
               jnp.concatenate([buf, zero], axis=1).reshape(1, nb * tn, -1))
    res = _tail(x, yc.reshape(b, t, dm), yd.reshape(b, t, dm), mods[2], mods[3], P["cd_wo"],
                P["norm_ffn_g"][1], P["wup"][1], P["conv_w"][1], P["conv_b"][1], P["wdn"][1],
                P["final_g"], True, tm, fix=fix, period=tn)
    conv1 = res[1].reshape(nb, tn, -1)[:, tn - 2:] if sample else res[1]
    new["ffn_conv"] = jnp.stack([conv0, conv1])
    y = res[0].reshape(nb, tn, d)
    return y, new


def kernel(x_prompt, x_sample, c_prompt, c_sample, cache_fox_k, cache_fox_v, cache_fox_logf, state_rwkv,
           state_rwkv_shift, cache_chunk_k, cache_chunk_v, state_hgrn, state_ffn_conv, ada_w, ada_b,
           norm_mix_g, norm_ffn_g, ab_w_in, rwkv_mu, rwkv_w0, rwkv_w2, rwkv_a0, rwkv_a2, rwkv_g2,
           rwkv_k_k, rwkv_k_a, rwkv_r_k, rwkv_lnx_g, rwkv_lnx_b, fox_b_f, ab_w_out, cd_w_in,
           chunk_rel_bias, hgrn_lb_table, hgrn_norm_g, cd_w_out, ffn_w_up, ffn_conv_w, ffn_conv_b,
           ffn_w_down, final_norm_g):
    bp, tp, d = x_prompt.shape
    bs, ts, _ = x_sample.shape
    depth = ada_w.shape[0]
    dm = D_MIX
    a_cols = rwkv_mu.shape[1]
    n_lw, n_la, n_lg = rwkv_w2.shape[1], rwkv_a2.shape[1], rwkv_g2.shape[1]

    c_all = jnp.concatenate([c_prompt, c_sample], axis=0)
    mods = _adaln(c_all, ada_w.reshape(depth * 2, d, 3 * d), ada_b.reshape(depth * 2, 3 * d))
    mods_p = [mods[i, :bp].reshape(bp, 1, 3 * d) for i in range(depth * 2)]
    mods_s = [jnp.repeat(mods[i, bp:], ts, axis=0).reshape(1, bs * ts, 3 * d) for i in range(depth * 2)]

    wl = jnp.zeros((n_lw + n_la + n_lg, 3 * dm), F32)
    wl = wl.at[:n_lw, 0:dm].set(rwkv_w2[0])
    wl = wl.at[n_lw:n_lw + n_la, dm:2 * dm].set(rwkv_a2[0])
    wl = wl.at[n_lw + n_la:, 2 * dm:].set(rwkv_g2[0])
    sm = jax.nn.softmax(hgrn_lb_table.astype(F32), axis=0)
    lb = (jnp.cumsum(sm, axis=0) - sm[0])[1]
    wf = jnp.zeros((d, 128), F32).at[:, :N_HEADS].set(ab_w_in[0][:, a_cols + 3 * dm:]).astype(BF16)
    bf = jnp.zeros((1, 128), F32).at[0, :N_HEADS].set(fox_b_f[0])
    p_c = cache_chunk_k.shape[2]
    bias_s = _toeplitz_bias(chunk_rel_bias[0], ts, p_c + ts, p_c).reshape(N_HEADS * ts, p_c + ts)

    P = {
        "a_cols": a_cols,
        "norm_mix_g": norm_mix_g, "norm_ffn_g": norm_ffn_g, "final_g": final_norm_g,
        "ab_w": ab_w_in[0][:, :a_cols + 3 * dm].astype(BF16), "ab_wf": wf, "fox_bf": bf,
        "ab_wo": ab_w_out[0].astype(BF16),
        "cd_w": cd_w_in[0].astype(BF16), "cd_wo": cd_w_out[0].astype(BF16),
        "wup": [ffn_w_up[i].astype(BF16) for i in range(depth)],
        "wdn": [ffn_w_down[i].astype(BF16) for i in range(depth)],
        "conv_w": ffn_conv_w, "conv_b": ffn_conv_b,
        "rwkv": {"mu": rwkv_mu[0], "w0": rwkv_w0[0], "a0": rwkv_a0[0], "wl": wl.astype(BF16), "k_k": rwkv_k_k[0],
                 "k_a": rwkv_k_a[0], "r_k": rwkv_r_k[0], "lnx_g": rwkv_lnx_g[0], "lnx_b": rwkv_lnx_b[0]},
        "rel_bias": chunk_rel_bias[0], "band_bias_c": bias_s[:, :p_c], "band_bias_n": bias_s[:, p_c:],
        "hgrn_lb": lb, "hgrn_norm_g": hgrn_norm_g[0],
    }
    cache = {"nb": bs, "tn": ts, "fox_k": cache_fox_k, "fox_v": cache_fox_v, "fox_logf": cache_fox_logf,
             "rwkv": state_rwkv, "rwkv_shift": state_rwkv_shift, "chunk_k": cache_chunk_k,
             "chunk_v": cache_chunk_v, "hgrn": state_hgrn, "ffn_conv": state_ffn_conv}

    y_p, sp = _trunk(x_prompt, mods_p, P, None, tm=512, tq=512, L=CHUNK, nc=4)
    y_s, ss = _trunk(x_sample.reshape(1, bs * ts, d), mods_s, P, cache, tm=bs * ts, tq=None, L=ts, nc=1)
    names = ("fox_k", "fox_v", "fox_logf", "rwkv", "rwkv_shift", "chunk_k", "chunk_v", "hgrn", "ffn_conv")
    return (y_p, y_s) + tuple(sp[n] for n in names) + tuple(ss[n] for n in names)
```

```python
import functools
import math

import jax
import jax.numpy as jnp
from jax import lax
from jax.experimental import pallas as pl
from jax.experimental.pallas import tpu as pltpu

F32 = jnp.float32
BF16 = jnp.bfloat16
HI = lax.Precision.HIGHEST

HEAD_DIM = 64
N_HEADS = 8
D_MIX = N_HEADS * HEAD_DIM
GROUP = 256
N_GROUPS = D_MIX // GROUP
HEADS_PER_GROUP = GROUP // HEAD_DIM
SUB = 16
CHUNK = 64
BAND = 512
REL_CLIP = 128
RMS_EPS = 1e-6
GN_EPS = 64e-5
ATTN_SCALE = HEAD_DIM ** -0.5
NEG = -1e30
FACTORED_MAX_DECAY = 60.0
VMEM_LIMIT = 56 * 1024 * 1024


def _dot(a, b, prec=None):
    return jnp.dot(a, b, preferred_element_type=F32, precision=prec)


def _dot_nt(a, b, prec=None):
    return lax.dot_general(a, b, (((1,), (1,)), ((), ())), preferred_element_type=F32, precision=prec)


def _dot_tn(a, b, prec=None):
    return lax.dot_general(a, b, (((0,), (0,)), ((), ())), preferred_element_type=F32, precision=prec)


_NN = ((1,), (0,))
_NT = ((1,), (1,))
_TN = ((0,), (0,))


def _sp(x, n):
    hi = x.astype(BF16)
    if n == 1:
        return (hi,)
    r = x - hi.astype(F32)
    mid = r.astype(BF16)
    if n == 2:
        return (hi, mid)
    return (hi, mid, (r - mid.astype(F32)).astype(BF16))


def bf(x):
    return (x.astype(BF16),)


def _mm(a, b, dims=_NN):
    n = max(len(a), len(b))
    out = None
    for i, ai in enumerate(a):
        for j, bj in enumerate(b):
            if i + j < n:
                d = lax.dot_general(ai, bj, (dims, ((), ())), preferred_element_type=F32)
                out = d if out is None else out + d
    return out


def _head_sum(x, bo_group, terms=2):
    return jnp.concatenate(
        [_mm(_sp(x[:, gi * GROUP:(gi + 1) * GROUP], terms), (bo_group,)) for gi in range(N_GROUPS)], axis=1)


def _sigmoid(x):
    return jax.nn.sigmoid(x)


def _silu(x):
    return x * jax.nn.sigmoid(x)


def _softplus(x):
    return jnp.maximum(x, 0.0) + jnp.log1p(jnp.exp(-jnp.abs(x)))


def _log_sigmoid(x):
    return -_softplus(-x)


def _iota(shape, dim):
    return lax.broadcasted_iota(jnp.int32, shape, dim)


def _const_spec(shape):
    nd = len(shape)
    return pl.BlockSpec(shape, lambda *_: (0,) * nd, pipeline_mode=pl.Buffered(1))


def _params(sem):
    return pltpu.CompilerParams(dimension_semantics=sem, vmem_limit_bytes=VMEM_LIMIT)


def _norm_mod(x, g, shift, scale):
    xn = x * lax.rsqrt(jnp.mean(x * x, axis=-1, keepdims=True) + RMS_EPS) * g
    return xn * (1.0 + scale) + shift


def _block_ones(n, blk):
    i = jnp.arange(n) // blk
    return (i[:, None] == i[None, :]).astype(BF16)


def _bd(x, rows_per_head, cols_per_head):
    n = HEADS_PER_GROUP
    t = jnp.concatenate([x] * n, axis=0)
    r = _iota(t.shape, 0) // rows_per_head
    c = _iota(t.shape, 1) // cols_per_head
    return jnp.where(r == c, t, 0.0)


def _unbd(m):
    out = m[0:HEAD_DIM]
    for h in range(1, HEADS_PER_GROUP):
        out = out + m[h * HEAD_DIM:(h + 1) * HEAD_DIM]
    return out


def _adaln_kernel(c_ref, w_ref, b_ref, o_ref):
    o_ref[0] = _dot(_silu(c_ref[...]), w_ref[0], HI) + b_ref[0]


def _adaln(c_all, ada_w, ada_b):
    n, d, d3 = ada_w.shape
    bt = c_all.shape[0]
    nt = d3 // d
    return pl.pallas_call(
        _adaln_kernel,
        grid=(n, nt),
        in_specs=[pl.BlockSpec((bt, d), lambda i, j: (0, 0)),
                  pl.BlockSpec((1, d, d), lambda i, j: (i, 0, j)),
                  pl.BlockSpec((1, 1, d), lambda i, j: (i, 0, j))],
        out_specs=pl.BlockSpec((1, bt, d), lambda i, j: (i, 0, j)),
        out_shape=jax.ShapeDtypeStruct((n, bt, d3), F32),
        compiler_params=_params(("parallel", "parallel")),
        name="adaln",
    )(c_all, ada_w, ada_b.reshape(n, 1, d3))


def _proj_kernel(*refs, splits, fox):
    x_ref, mod_ref, g_ref, w_ref = refs[:4]
    pos = 4
    if fox:
        wf_ref, bf_ref = refs[4:6]
        pos = 6
    outs = refs[pos:]
    d = x_ref.shape[-1]
    mod = mod_ref[0]
    h = _norm_mod(x_ref[0], g_ref[...], mod[:, :d], mod[:, d:2 * d]).astype(BF16)
    off = 0
    for o_ref, n in zip(outs, splits):
        o_ref[0] = _dot(h, w_ref[:, off:off + n])
        off += n
    if fox:
        zf = _dot(h, wf_ref[...]) + bf_ref[...]
        outs[len(splits)][0] = _log_sigmoid(zf)[:, :N_HEADS]


def _proj(x, mod, g, w_bf, splits, tm, wf=None, bf=None):
    b, t, d = x.shape
    r = mod.shape[1]
    fox = wf is not None
    mod_spec = (pl.BlockSpec((1, 1, 3 * d), lambda i, j: (i, 0, 0)) if r == 1
                else pl.BlockSpec((1, tm, 3 * d), lambda i, j: (i, j, 0)))
    in_specs = [pl.BlockSpec((1, tm, d), lambda i, j: (i, j, 0)), mod_spec,
                _const_spec((1, d)), _const_spec(w_bf.shape)]
    args = [x, mod, g.reshape(1, d), w_bf]
    widths = list(splits)
    if fox:
        in_specs += [_const_spec(wf.shape), _const_spec(bf.shape)]
        args += [wf, bf]
        widths.append(N_HEADS)
    return pl.pallas_call(
        functools.partial(_proj_kernel, splits=tuple(splits), fox=fox),
        grid=(b, t // tm),
        in_specs=in_specs,
        out_specs=[pl.BlockSpec((1, tm, n), lambda i, j: (i, j, 0)) for n in widths],
        out_shape=[jax.ShapeDtypeStruct((b, t, n), F32) for n in widths],
        compiler_params=_params(("parallel", "parallel")),
        name="proj",
    )(*args)


def _tail_kernel(*refs, sample, final, period, cb, down_group):
    (x_ref, ya_ref, yb_ref, m1_ref, m2_ref, wo_ref, g_ref, wup_ref, cw_ref, cbias_ref,
     wdn_ref, fg_ref) = refs[:12]
    if sample:
        fix1_ref, fix2_ref, o_ref, u_ref, g_sc = refs[12:]
    else:
        o_ref, cs_ref, g_sc, tail_sc = refs[12:]
    tm, d = x_ref.shape[1], x_ref.shape[2]
    ff = wdn_ref.shape[0]
    dm = ya_ref.shape[2]
    m1 = m1_ref[0]
    m2 = m2_ref[0]
    ymix = (_dot(ya_ref[0].astype(BF16), wo_ref[0:dm, :])
            + _dot(yb_ref[0].astype(BF16), wo_ref[dm:2 * dm, :]))
    x1 = x_ref[0] + m1[:, 2 * d:3 * d] * ymix
    h = _norm_mod(x1, g_ref[...], m2[:, :d], m2[:, d:2 * d]).astype(BF16)
    rows = _iota((tm, 1), 0)
    if sample:
        tpos = rows & (period - 1)
    else:
        t = pl.program_id(1)

        @pl.when(t == 0)
        def _():
            tail_sc[...] = jnp.zeros_like(tail_sc)

    def up(j):
        return [_dot(h, wup_ref[:, half * ff + j * cb:half * ff + (j + 1) * cb]) for half in range(2)]

    def conv(u, u1, u2, c0):
        cw = cw_ref[:, c0:c0 + cb]
        return cbias_ref[:, c0:c0 + cb] + cw[0:1] * u2 + cw[1:2] * u1 + cw[2:3] * u

    def gated(j, us):
        halves = []
        for half, u in enumerate(us):
            c0 = half * ff + j * cb
            if sample:
                u1 = jnp.where(tpos == 0, fix1_ref[0, :, c0:c0 + cb], pltpu.roll(u, 1, 0))
                u2 = jnp.where(tpos < 2, fix2_ref[0, :, c0:c0 + cb], pltpu.roll(u, 2, 0))
                u_ref[0, :, c0:c0 + cb] = u
                halves.append(conv(u, u1, u2, c0))
            else:
                edge = jnp.concatenate([tail_sc[:, c0:c0 + cb], u[0:8]], axis=0)
                first = conv(edge, pltpu.roll(edge, 1, 0), pltpu.roll(edge, 2, 0), c0)[8:16]
                tail_sc[:, c0:c0 + cb] = u[tm - 8:tm, :]
                main = conv(u, pltpu.roll(u, 1, 0), pltpu.roll(u, 2, 0), c0)
                halves.append(jnp.concatenate([first, main[8:]], axis=0))
        return (_silu(halves[0]) * halves[1]).astype(BF16)

    acc = None
    n_blk = ff // cb
    us = up(0)
    j0 = 0
    for j in range(n_blk):
        us_next = up(j + 1) if j + 1 < n_blk else None
        g_sc[:, j * cb:(j + 1) * cb] = gated(j, us)
        if (j + 1 - j0) == down_group or j + 1 == n_blk:
            part = _dot(g_sc[:, j0 * cb:(j + 1) * cb], wdn_ref[j0 * cb:(j + 1) * cb, :])
            acc = part if acc is None else acc + part
            j0 = j + 1
        us = us_next
    out = x1 + m2[:, 2 * d:3 * d] * acc
    if final:
        out = out * lax.rsqrt(jnp.mean(out * out, axis=-1, keepdims=True) + RMS_EPS) * fg_ref[...]
    o_ref[0] = out
    if not sample:
        @pl.when(t == pl.num_programs(1) - 1)
        def _():
            cs_ref[0] = tail_sc[6:8, :]


def _tail(x, ya, yb, mod1, mod2, wo_bf, g, wup_bf, conv_w, conv_b, wdn_bf, fg, final, tm,
          fix=None, period=None):
    b, t, d = x.shape
    ff = wdn_bf.shape[0]
    dm = ya.shape[2]
    sample = fix is not None
    r = mod1.shape[1]
    mod_spec = (pl.BlockSpec((1, 1, 3 * d), lambda i, j: (i, 0, 0)) if r == 1
                else pl.BlockSpec((1, tm, 3 * d), lambda i, j: (i, j, 0)))
    row_spec = lambda n: pl.BlockSpec((1, tm, n), lambda i, j: (i, j, 0))
    in_specs = [row_spec(d), row_spec(dm), row_spec(dm), mod_spec, mod_spec,
                _const_spec(wo_bf.shape), _const_spec((1, d)), _const_spec(wup_bf.shape),
                _const_spec(conv_w.shape), _const_spec((1, 2 * ff)), _const_spec(wdn_bf.shape),
                _const_spec((1, d))]
    args = [x, ya, yb, mod1, mod2, wo_bf, g.reshape(1, d), wup_bf, conv_w, conv_b.reshape(1, 2 * ff),
            wdn_bf, fg.reshape(1, d)]
    if sample:
        in_specs += [row_spec(2 * ff), row_spec(2 * ff)]
        args += list(fix)
        out_specs = [row_spec(d), row_spec(2 * ff)]
        out_shape = [jax.ShapeDtypeStruct((b, t, d), F32), jax.ShapeDtypeStruct((b, t, 2 * ff), F32)]
        scratch = [pltpu.VMEM((tm, ff), BF16)]
    else:
        out_specs = [row_spec(d), pl.BlockSpec((1, 2, 2 * ff), lambda i, j: (i, 0, 0))]
        out_shape = [jax.ShapeDtypeStruct((b, t, d), F32), jax.ShapeDtypeStruct((b, 2, 2 * ff), F32)]
        scratch = [pltpu.VMEM((tm, ff), BF16), pltpu.VMEM((8, 2 * ff), F32)]
    return pl.pallas_call(
        functools.partial(_tail_kernel, sample=sample, final=final, period=period, cb=256, down_group=4),
        grid=(b, t // tm),
        in_specs=in_specs,
        out_specs=out_specs,
        out_shape=out_shape,
        scratch_shapes=scratch,
        compiler_params=_params(("parallel", "arbitrary")),
        name="tail",
    )(*args)


def _rwkv_kernel(z_ref, shift0_ref, s0_ref, mu_ref, w0_ref, a0_ref, wl_ref, kk_ref, ka_ref, rk_ref,
                 lng_ref, lnb_ref, bo_ref, y_ref, st_ref, prev_sc, st_sc, *, L, nc):
    t = pl.program_id(1)
    tb = nc * L

    @pl.when(t == 0)
    def _():
        prev_sc[...] = shift0_ref[0]
        for gi in range(N_GROUPS):
            st_sc[gi] = _bd(s0_ref[0, gi], HEAD_DIM, HEAD_DIM)

    z = z_ref[0]
    rows = _iota((tb, 1), 0)
    z_prev = jnp.where(rows == 0, prev_sc[...], pltpu.roll(z, 1, 0))
    prev_sc[...] = z[tb - 1:tb, :]
    zs = z + mu_ref[...] * (z_prev - z)
    dm = D_MIX
    r = zs[:, 0:dm]
    k = zs[:, dm:2 * dm]
    v = zs[:, 2 * dm:3 * dm]
    lr = zs[:, 3 * dm:]
    nl = lr.shape[1]
    lane = _iota((1, nl), 1)
    act = jnp.where(lane < 64, jnp.tanh(lr), jnp.where(lane < 128, lr, _sigmoid(lr)))
    lo = _mm(bf(act), (wl_ref[...],))
    w = -_softplus(-(w0_ref[...] + lo[:, 0:dm])) - 0.5
    lw = -jnp.exp(w)
    a = _sigmoid(a0_ref[...] + lo[:, dm:2 * dm])
    g = lo[:, 2 * dm:3 * dm]
    bo = bo_ref[...]
    kk = k * kk_ref[...]
    k2 = k * (1.0 + (a - 1.0) * ka_ref[...])
    sums = _head_sum(jnp.concatenate([kk * kk, r * k2 * rk_ref[...]], axis=0), bo)
    kk = kk / jnp.maximum(jnp.sqrt(sums[:tb]), 1e-12)
    bonus = sums[tb:]
    am = -kk
    bm = kk * a

    ri = _iota((tb, tb), 0)
    ci_ = _iota((tb, tb), 1)
    tri = ((ci_ <= ri) & (ci_ // L == ri // L)).astype(BF16)
    c = _mm((tri,), _sp(lw, 3))
    c_last = jnp.concatenate(
        [jnp.broadcast_to(c[(j + 1) * L - 1:(j + 1) * L, :], (L, c.shape[1])) for j in range(nc)], axis=0)
    a_t = am * jnp.exp(c - lw)
    r_t = r * jnp.exp(c)
    g_inv = jnp.exp(-c)
    k_t = k2 * g_inv
    b_t = bm * g_inv
    g_end = jnp.exp(c_last - c)
    k_end = k2 * g_end
    b_end = bm * g_end
    g_last = jnp.exp(c_last)

    cw = HEADS_PER_GROUP * L
    e_tile = ((_iota((L, cw), 1) % L) == _iota((L, cw), 0)).astype(BF16)
    tmask = (_iota((GROUP, cw), 0) // HEAD_DIM) == (_iota((GROUP, cw), 1) // L)
    col_s = _iota((L, cw), 1) % L
    row_t = _iota((L, cw), 0)
    strict = col_s < row_t
    incl = col_s <= row_t
    bdmask = (_iota((GROUP, GROUP), 0) // HEAD_DIM) == (_iota((GROUP, GROUP), 1) // HEAD_DIM)
    n_dbl = int(math.log2(L))

    def bd(x, n, cols_per_head=HEAD_DIM):
        return tuple(_bd(p, L, cols_per_head) for p in _sp(x, n))

    eye_cat = (col_s == row_t).astype(F32)

    insts = [(cj, gi) for cj in range(nc) for gi in range(N_GROUPS)]
    rsl = {cj: slice(cj * L, (cj + 1) * L) for cj in range(nc)}
    gsl = {gi: slice(gi * GROUP, (gi + 1) * GROUP) for gi in range(N_GROUPS)}
    bk_t = {cj: _mm(bf(jnp.concatenate([b_t[rsl[cj]], k_t[rsl[cj]]], axis=1)), (e_tile,), _TN) for cj in range(nc)}
    ar = {(cj, gi): bf(jnp.concatenate([a_t[rsl[cj], gsl[gi]], r_t[rsl[cj], gsl[gi]]], axis=0)) for cj, gi in insts}
    arb = {(cj, gi): _mm(ar[cj, gi], bf(jnp.where(tmask, bk_t[cj][gsl[gi]], 0.0))) for cj, gi in insts}
    ark = {(cj, gi): _mm(ar[cj, gi], bf(jnp.where(tmask, bk_t[cj][dm + gi * GROUP:dm + (gi + 1) * GROUP], 0.0)))
           for cj, gi in insts}
    p = {k_: jnp.where(strict, arb[k_][:L], 0.0) for k_ in insts}
    m_rb = {k_: bf(jnp.where(incl, arb[k_][L:], 0.0)) for k_ in insts}
    nmv = {(cj, gi): _mm(bf(jnp.concatenate([jnp.where(strict, ark[cj, gi][:L], 0.0),
                                              jnp.where(incl, ark[cj, gi][L:], 0.0)], axis=0)),
                         bd(v[rsl[cj], gsl[gi]], 1)) for cj, gi in insts}
    tm = {k_: eye_cat + p[k_] for k_ in insts}
    for i in range(n_dbl - 1):
        for k_ in insts:
            w = bd(p[k_], 1, L)
            if i == 0:
                p[k_] = _mm(bf(p[k_]), w)
            else:
                both = _mm(bf(jnp.concatenate([p[k_], tm[k_]], axis=0)), w)
                p[k_] = both[:L]
                tm[k_] = tm[k_] + both[L:]
    tm = {k_: bf(tm[k_] + _mm(bf(tm[k_]), bd(p[k_], 1, L))) for k_ in insts}

    groups = range(N_GROUPS)
    state = [st_sc[gi] for gi in groups]
    y_rows = []
    for cj in range(nc):
        rs = rsl[cj]
        ars = [_mm(ar[cj, gi], bf(state[gi]), _NT) for gi in groups]
        u = [_mm(tm[cj, gi], bd(ars[gi][:L] + nmv[cj, gi][:L], 1)) for gi in groups]
        ys = [ars[gi][L:] + nmv[cj, gi][L:] + _mm(m_rb[cj, gi], bd(u[gi], 1)) for gi in groups]
        upd = [_mm(bf(jnp.concatenate([v[rs, gsl[gi]], u[gi]], axis=0)),
                   bf(jnp.concatenate([k_end[rs, gsl[gi]], b_end[rs, gsl[gi]]], axis=0)), _TN) for gi in groups]
        state = [state[gi] * g_last[rs, gsl[gi]][0:1] + jnp.where(bdmask, upd[gi], 0.0) for gi in groups]
        y_rows.append(jnp.concatenate(ys, axis=1))
    for gi in groups:
        st_sc[gi] = state[gi]

    y = jnp.concatenate(y_rows, axis=0) if nc > 1 else y_rows[0]
    inv = 1.0 / HEAD_DIM
    mean = _head_sum(y, bo) * inv
    dlt = y - mean
    var = _head_sum(dlt * dlt, bo) * inv
    yn = dlt * lax.rsqrt(var + GN_EPS) * lng_ref[...] + lnb_ref[...]
    y_ref[0] = (yn + bonus * v) * g

    @pl.when(t == pl.num_programs(1) - 1)
    def _():
        for gi in range(N_GROUPS):
            st_ref[0, gi] = _unbd(st_sc[gi])


def _state_spec():
    return pl.BlockSpec((1, N_GROUPS, HEAD_DIM, GROUP), lambda i, j: (i, 0, 0, 0))


def _rwkv(za, shift0, s0_cat, pr, L, nc):
    b, t, ac = za.shape
    dm = D_MIX
    tb = L * nc
    row = lambda a: a.reshape(1, -1)
    consts = [row(pr["mu"]), row(pr["w0"]), row(pr["a0"]), pr["wl"], row(pr["k_k"]), row(pr["k_a"]),
              row(pr["r_k"]), row(pr["lnx_g"]), row(pr["lnx_b"]), _block_ones(GROUP, HEAD_DIM)]
    return pl.pallas_call(
        functools.partial(_rwkv_kernel, L=L, nc=nc),
        grid=(b, t // tb),
        in_specs=[pl.BlockSpec((1, tb, ac), lambda i, j: (i, j, 0)),
                  pl.BlockSpec((1, 1, ac), lambda i, j: (i, 0, 0)), _state_spec()]
                 + [_const_spec(c.shape) for c in consts],
        out_specs=[pl.BlockSpec((1, tb, dm), lambda i, j: (i, j, 0)), _state_spec()],
        out_shape=[jax.ShapeDtypeStruct((b, t, dm), F32),
                   jax.ShapeDtypeStruct((b, N_GROUPS, HEAD_DIM, GROUP), F32)],
        scratch_shapes=[pltpu.VMEM((1, ac), F32), pltpu.VMEM((N_GROUPS, GROUP, GROUP), F32)],
        compiler_params=_params(("parallel", "arbitrary")),
        name="rwkv7",
    )(za, shift0.reshape(b, 1, ac), s0_cat, *consts)


def _hgrn_kernel(zq_ref, zf_ref, zi_ref, zg_ref, s0_ref, lb_ref, ng_ref, bo_ref, y_ref, st_ref, st_sc, *, L):
    t = pl.program_id(1)

    @pl.when(t == 0)
    def _():
        for gi in range(N_GROUPS):
            st_sc[gi] = _bd(s0_ref[0, gi], HEAD_DIM, HEAD_DIM)

    dm = D_MIX
    q = _silu(zq_ref[0])
    xf = zf_ref[0]
    v = zi_ref[0]
    lb = lb_ref[...]
    la = jnp.log(lb)
    lc = jnp.log1p(-lb) + _log_sigmoid(xf)
    logf = jnp.maximum(la, lc) + jnp.log1p(jnp.exp(-jnp.abs(la - lc)))
    k = (1.0 - lb) * _sigmoid(-xf)
    bo = bo_ref[...]

    tri = (_iota((L, L), 1) <= _iota((L, L), 0)).astype(BF16)
    b = _mm((tri,), _sp(logf, 3))
    b_last = b[L - 1:L, :]
    qe = q * jnp.exp(b)
    k_end = k * jnp.exp(b_last - b)
    g_last = jnp.exp(b_last)

    o = jnp.concatenate([_mm(bf(qe[:, gi * GROUP:(gi + 1) * GROUP]), bf(st_sc[gi]), _NT)
                         for gi in range(N_GROUPS)], axis=1)

    n_sub = L // SUB
    nst = N_HEADS * SUB
    stackmask = (_iota((nst, dm), 0) // SUB) == (_iota((nst, dm), 1) // HEAD_DIM)
    sub_rows = _iota((SUB, 1), 0)
    subs = range(n_sub)
    rsub = [slice(i * SUB, (i + 1) * SUB) for i in subs]
    brefs = [jnp.zeros((1, dm), F32) if i == 0 else b[i * SUB - 1:i * SUB] for i in subs]

    def fold_heads(ov):
        out = ov[0:SUB]
        for hh in range(1, N_HEADS):
            out = out + ov[hh * SUB:(hh + 1) * SUB]
        return out

    def stacked_q(i):
        qh = q[rsub[i]] * jnp.exp(b[rsub[i]] - brefs[i])
        return bf(jnp.where(stackmask, jnp.concatenate([qh] * N_HEADS, axis=0), 0.0))

    def intra_exact():
        xs = {i: [] for i in subs}
        for s in range(SUB):
            for i in subs:
                bi = b[rsub[i]]
                e = jnp.exp(jnp.where(sub_rows >= s, bi - bi[s:s + 1], NEG))
                xs[i].append(q[rsub[i]] * e * k[rsub[i]][s:s + 1])
        att = {i: _head_sum(jnp.concatenate(xs[i], axis=0), bo, terms=1) for i in subs}
        later = [i for i in subs if i > 0]
        qst = {i: stacked_q(i) for i in later}
        kh = {i: bf(k[0:i * SUB] * jnp.exp(brefs[i] - b[0:i * SUB])) for i in later}
        att2 = {i: _mm(qst[i], kh[i], _NT) for i in later}
        ov = {i: jnp.where(stackmask, _mm(bf(att2[i]), bf(v[0:i * SUB])), 0.0) for i in later}
        o_subs = []
        for i in subs:
            vi = v[rsub[i]]
            oi = att[i][0:SUB] * vi[0:1]
            for s in range(1, SUB):
                oi = oi + att[i][s * SUB:(s + 1) * SUB] * vi[s:s + 1]
            if i > 0:
                oi = oi + fold_heads(ov[i])
            o_subs.append(oi)
        return jnp.concatenate(o_subs, axis=0) if n_sub > 1 else o_subs[0]

    def intra_factored():
        qst = {i: stacked_q(i) for i in subs}
        kh = {i: bf(k[0:(i + 1) * SUB] * jnp.exp(brefs[i] - b[0:(i + 1) * SUB])) for i in subs}
        att2 = {}
        for i in subs:
            hi_ = (i + 1) * SUB
            visible = _iota((nst, hi_), 1) <= i * SUB + _iota((nst, hi_), 0) % SUB
            att2[i] = jnp.where(visible, _mm(qst[i], kh[i], _NT), 0.0)
        ov = {i: jnp.where(stackmask, _mm(bf(att2[i]), bf(v[0:(i + 1) * SUB])), 0.0) for i in subs}
        o_subs = [fold_heads(ov[i]) for i in subs]
        return jnp.concatenate(o_subs, axis=0) if n_sub > 1 else o_subs[0]

    drop = jnp.concatenate([b[(i + 1) * SUB - 1:(i + 1) * SUB] - brefs[i] for i in subs], axis=0)
    o = o + lax.cond(jnp.min(drop) >= -FACTORED_MAX_DECAY, intra_factored, intra_exact)

    bdmask = (_iota((GROUP, GROUP), 0) // HEAD_DIM) == (_iota((GROUP, GROUP), 1) // HEAD_DIM)
    for gi in range(N_GROUPS):
        sl = slice(gi * GROUP, (gi + 1) * GROUP)
        upd = _mm(bf(v[:, sl]), bf(k_end[:, sl]), _TN)
        st_sc[gi] = st_sc[gi] * g_last[:, sl] + jnp.where(bdmask, upd, 0.0)

    ms = _head_sum(o * o, bo) * (1.0 / HEAD_DIM)
    y_ref[0] = o * lax.rsqrt(ms + RMS_EPS) * ng_ref[...] * _silu(zg_ref[0])

    @pl.when(t == pl.num_programs(1) - 1)
    def _():
        for gi in range(N_GROUPS):
            st_ref[0, gi] = _unbd(st_sc[gi])


def _hgrn(zq, zf, zi, zg, s0_cat, lb, norm_g, L):
    b, t, dm = zq.shape
    consts = [lb.reshape(1, dm), norm_g.reshape(1, dm), _block_ones(GROUP, HEAD_DIM)]
    row_spec = pl.BlockSpec((1, L, dm), lambda i, j: (i, j, 0))
    return pl.pallas_call(
        functools.partial(_hgrn_kernel, L=L),
        grid=(b, t // L),
        in_specs=[row_spec] * 4 + [_state_spec()] + [_const_spec(c.shape) for c in consts],
        out_specs=[row_spec, _state_spec()],
        out_shape=[jax.ShapeDtypeStruct((b, t, dm), F32),
                   jax.ShapeDtypeStruct((b, N_GROUPS, HEAD_DIM, GROUP), F32)],
        scratch_shapes=[pltpu.VMEM((N_GROUPS, GROUP, GROUP), F32)],
        compiler_params=_params(("parallel", "arbitrary")),
        name="hgrn2",
    )(zq, zf, zi, zg, s0_cat, *consts)


def _cumsum_kernel(x_ref, o_ref):
    x = x_ref[0]
    n, w = x.shape
    nblk = n // N_HEADS
    upper = (_iota((w, w), 0) <= _iota((w, w), 1)).astype(F32)
    c = _dot(x, upper, HI)
    tot = jnp.broadcast_to(c[:, w - 1:w], (n, w))
    ri = _iota((n, n), 0)
    ci = _iota((n, n), 1)
    prior = ((ri // nblk == ci // nblk) & (ci < ri)).astype(F32)
    o_ref[0] = c + _dot(prior, tot, HI)


def _cumsum_time(logf_bht):
    b, h, t = logf_bht.shape
    w = 128
    n = h * (t // w)
    x = logf_bht.reshape(b, n, w)
    out = pl.pallas_call(
        _cumsum_kernel,
        grid=(b,),
        in_specs=[pl.BlockSpec((1, n, w), lambda i: (i, 0, 0))],
        out_specs=pl.BlockSpec((1, n, w), lambda i: (i, 0, 0)),
        out_shape=jax.ShapeDtypeStruct((b, n, w), F32),
        compiler_params=_params(("parallel",)),
        name="cumsum",
    )(x)
    return out.reshape(b, h, t)


def _fox_kernel(q_ref, k_ref, v_ref, f_ref, o_ref, kb_sc, vb_sc, *, tq):
    i = pl.program_id(2)

    @pl.when(i == 0)
    def _():
        kb_sc[...] = k_ref[0].astype(BF16)
        vb_sc[...] = v_ref[0].astype(BF16)

    q = q_ref[0] * ATTN_SCALE
    lane = _iota((1, 2 * HEAD_DIM), 1)
    causal = _iota((tq, tq), 1) <= _iota((tq, tq), 0)
    outs = []
    for hh in range(2):
        head_lane = (lane // HEAD_DIM) == hh
        qm = jnp.where(head_lane, q, 0.0).astype(BF16)

        def scores(j, hh=hh, qm=qm):
            return _dot_nt(qm, kb_sc[pl.ds(j * tq, tq), :]) - f_ref[0, 0, j, hh:hh + 1, :]

        def update(j, s, m, acc, head_lane=head_lane):
            m_new = jnp.maximum(m, jnp.max(s, axis=-1, keepdims=True))
            p = jnp.exp(s - m_new).astype(BF16)
            vx = jnp.where(head_lane, vb_sc[pl.ds(j * tq, tq), :], 1.0)
            return m_new, jnp.exp(m - m_new) * acc + _dot(p, vx)

        def pair(j2, carry):
            sa, sb = scores(2 * j2), scores(2 * j2 + 1)
            return update(2 * j2 + 1, sb, *update(2 * j2, sa, *carry))

        init = (jnp.full((tq, 1), NEG, F32), jnp.zeros((tq, 2 * HEAD_DIM), F32))
        m, acc = lax.fori_loop(0, i // 2, pair, init)

        def odd_tail(m, acc):
            sa, sb = scores(i - 1), jnp.where(causal, scores(i), NEG)
            return update(i, sb, *update(i - 1, sa, m, acc))[1]

        def even_tail(m, acc):
            return update(i, jnp.where(causal, scores(i), NEG), m, acc)[1]

        acc = lax.cond(i % 2 == 1, odd_tail, even_tail, m, acc)
        outs.append(acc / pltpu.roll(acc, HEAD_DIM, 1))
    o_ref[0] = jnp.where((lane // HEAD_DIM) == 0, outs[0], outs[1])


def _fox_prompt(q, k, v, cum_bht, tq):
    b, t, dm = q.shape
    hp = dm // (2 * HEAD_DIM)
    nk = t // tq
    f = cum_bht.reshape(b, hp, 2, nk, tq).transpose(0, 1, 3, 2, 4)
    kv_spec = pl.BlockSpec((1, t, 2 * HEAD_DIM), lambda i, h, j: (i, 0, h))
    return pl.pallas_call(
        functools.partial(_fox_kernel, tq=tq),
        scratch_shapes=[pltpu.VMEM((t, 2 * HEAD_DIM), BF16), pltpu.VMEM((t, 2 * HEAD_DIM), BF16)],
        grid=(b, hp, nk),
        in_specs=[pl.BlockSpec((1, tq, 2 * HEAD_DIM), lambda i, h, j: (i, j, h)), kv_spec, kv_spec,
                  pl.BlockSpec((1, 1, nk, 2, tq), lambda i, h, j: (i, h, 0, 0, 0))],
        out_specs=pl.BlockSpec((1, tq, 2 * HEAD_DIM), lambda i, h, j: (i, j, h)),
        out_shape=jax.ShapeDtypeStruct((b, t, dm), F32),
        compiler_params=_params(("parallel", "parallel", "arbitrary")),
        name="fox_prompt",
    )(q, k, v, f)


def _band_kernel(q_ref, k_ref, v_ref, bias_ref, o_ref, *, tq):
    i = pl.program_id(2)
    q = q_ref[0] * ATTN_SCALE
    lane = _iota((1, 2 * HEAD_DIM), 1)
    n_piece = BAND // tq + 1
    outs = []
    for hh in range(2):
        qm = jnp.where((lane // HEAD_DIM) == hh, q, 0.0).astype(BF16)
        ss, vs = [], []
        for p in range(n_piece):
            blk = i - (n_piece - 1) + p
            start = jnp.maximum(blk, 0) * tq
            kb = k_ref[0, pl.ds(start, tq), :].astype(BF16)
            s = _dot_nt(qm, kb) + bias_ref[0, hh, :, p * tq:(p + 1) * tq]
            ss.append(jnp.where(blk >= 0, s, NEG))
            vs.append(v_ref[0, pl.ds(start, tq), :].astype(BF16))
        m = jnp.max(ss[0], axis=-1, keepdims=True)
        for s in ss[1:]:
            m = jnp.maximum(m, jnp.max(s, axis=-1, keepdims=True))
        l = jnp.zeros((tq, 1), F32)
        acc = jnp.zeros((tq, 2 * HEAD_DIM), F32)
        for s, vb in zip(ss, vs):
            p_ = jnp.exp(s - m)
            l = l + jnp.sum(p_, axis=-1, keepdims=True)
            acc = acc + _dot(p_.astype(BF16), vb)
        outs.append(acc / l)
    o_ref[0] = jnp.where((lane // HEAD_DIM) == 0, outs[0], outs[1])


def _toeplitz_bias(rel_bias, nq, nk, offset):
    h = rel_bias.shape[0]
    m = jnp.arange(nq + nk - 1)
    u = rel_bias[:, jnp.clip(offset + nq - 1 - m, -REL_CLIP, REL_CLIP) + REL_CLIP].astype(F32)
    period = nq + nk
    up = jnp.pad(u, ((0, 0), (0, 1)))
    skew = jnp.tile(up, (1, nq))[:, :nq * (period - 1)].reshape(h, nq, period - 1)
    return skew[:, :, nq - 1:]


def _band_bias_prompt(rel_bias, tq):
    nk = BAND + tq
    bias = _toeplitz_bias(rel_bias, tq, nk, BAND)
    qc = jnp.arange(tq)[:, None] // CHUNK
    kc = jnp.arange(nk)[None, :] // CHUNK
    valid = (kc >= qc) & (kc <= qc + BAND // CHUNK)
    return jnp.where(valid[None], bias, NEG)


def _band_prompt(q, k, v, rel_bias, tq):
    b, t, dm = q.shape
    hp = dm // (2 * HEAD_DIM)
    bias = _band_bias_prompt(rel_bias, tq).reshape(hp, 2, tq, BAND + tq)
    kv_spec = pl.BlockSpec((1, t, 2 * HEAD_DIM), lambda i, h, j: (i, 0, h))
    return pl.pallas_call(
        functools.partial(_band_kernel, tq=tq),
        grid=(b, hp, t // tq),
        in_specs=[pl.BlockSpec((1, tq, 2 * HEAD_DIM), lambda i, h, j: (i, j, h)), kv_spec, kv_spec,
                  pl.BlockSpec((1, 2, tq, BAND + tq), lambda i, h, j: (h, 0, 0, 0))],
        out_specs=pl.BlockSpec((1, tq, 2 * HEAD_DIM), lambda i, h, j: (i, j, h)),
        out_shape=jax.ShapeDtypeStruct((b, t, dm), F32),
        compiler_params=_params(("parallel", "parallel", "arbitrary")),
        name="band_prompt",
    )(q, k, v, bias)


def _cached_attn_kernel(*refs, fox):
    if fox:
        q_ref, kn_ref, vn_ref, ck_ref, cv_ref, fc_ref, fn_ref, o_ref = refs
    else:
        q_ref, kn_ref, vn_ref, ck_ref, cv_ref, bc_ref, bn_ref, o_ref = refs
    tn, dm = q_ref.shape[1], q_ref.shape[2]
    nst = N_HEADS * tn
    stackmask = (_iota((nst, dm), 0) // tn) == (_iota((nst, dm), 1) // HEAD_DIM)
    q = q_ref[0] * ATTN_SCALE
    qst = jnp.where(stackmask, jnp.concatenate([q] * N_HEADS, axis=0), 0.0).astype(BF16)
    s_c = _dot_nt(qst, ck_ref[0].astype(BF16))
    s_n = _dot_nt(qst, kn_ref[0].astype(BF16))
    if fox:
        expand = ((_iota((nst, N_HEADS), 0) // tn) == _iota((nst, N_HEADS), 1)).astype(F32)
        fc = fc_ref[0]
        upper = (_iota((tn, tn), 0) <= _iota((tn, tn), 1)).astype(F32)
        fnew = fc[:, fc.shape[1] - 1:] + _dot(fn_ref[0], upper, HI)
        s_c = s_c - _dot(expand, fc, HI)
        s_n = s_n - _dot(expand, fnew, HI)
        tq = _iota((nst, tn), 0) % tn
        s_n = jnp.where(_iota((nst, tn), 1) <= tq, s_n, NEG)
    else:
        s_c = s_c + bc_ref[...]
        s_n = s_n + bn_ref[...]
    m = jnp.maximum(jnp.max(s_c, axis=-1, keepdims=True), jnp.max(s_n, axis=-1, keepdims=True))
    p_c = jnp.exp(s_c - m)
    p_n = jnp.exp(s_n - m)
    l = jnp.sum(p_c, axis=-1, keepdims=True) + jnp.sum(p_n, axis=-1, keepdims=True)
    ov = _dot(p_c.astype(BF16), cv_ref[0].astype(BF16)) + _dot(p_n.astype(BF16), vn_ref[0].astype(BF16))
    ov = jnp.where(stackmask, ov / l, 0.0)
    out = ov[0:tn]
    for hh in range(1, N_HEADS):
        out = out + ov[hh * tn:(hh + 1) * tn]
    o_ref[0] = out


def _cached_attn(q, kn, vn, ck, cv, extra_c, extra_n, fox):
    b, tn, dm = q.shape
    p = ck.shape[1]
    new_spec = pl.BlockSpec((1, tn, dm), lambda i: (i, 0, 0))
    cache_spec = pl.BlockSpec((1, p, dm), lambda i: (i, 0, 0))
    if fox:
        ex_specs = [pl.BlockSpec((1, N_HEADS, p), lambda i: (i, 0, 0)),
                    pl.BlockSpec((1, N_HEADS, tn), lambda i: (i, 0, 0))]
    else:
        ex_specs = [_const_spec(extra_c.shape), _const_spec(extra_n.shape)]
    return pl.pallas_call(
        functools.partial(_cached_attn_kernel, fox=fox),
        grid=(b,),
        in_specs=[new_spec, new_spec, new_spec, cache_spec, cache_spec] + ex_specs,
        out_specs=new_spec,
        out_shape=jax.ShapeDtypeStruct((b, tn, dm), F32),
        compiler_params=_params(("parallel",)),
        name="fox_sample" if fox else "band_sample",
    )(q, kn, vn, ck, cv, extra_c, extra_n)


def _to_cat(s):
    b = s.shape[0]
    s5 = s.reshape(b, N_GROUPS, HEADS_PER_GROUP, HEAD_DIM, HEAD_DIM)
    return s5.transpose(0, 1, 3, 2, 4).reshape(b, N_GROUPS, HEAD_DIM, GROUP)


def _from_cat(c):
    b = c.shape[0]
    c5 = c.reshape(b, N_GROUPS, HEAD_DIM, HEADS_PER_GROUP, HEAD_DIM)
    return c5.transpose(0, 1, 3, 2, 4).reshape(b, N_HEADS, HEAD_DIM, HEAD_DIM)


def _trunk(x, mods, P, cache, tm, tq, L, nc):
    b, t, d = x.shape
    dm = D_MIX
    sample = cache is not None
    new = {}
    za, q, k, v, logf = _proj(x, mods[0], P["norm_mix_g"][0], P["ab_w"], (P["a_cols"], dm, dm, dm), tm,
                              wf=P["ab_wf"], bf=P["fox_bf"])
    ac = P["a_cols"]
    if sample:
        nb, tn = cache["nb"], cache["tn"]
        za_b = za.reshape(nb, tn, ac)
        shift0 = cache["rwkv_shift"][0]
        s0 = _to_cat(cache["rwkv"][0])
    else:
        nb, tn = b, t
        za_b = za
        shift0 = jnp.zeros((nb, ac), F32)
        s0 = jnp.zeros((nb, N_GROUPS, HEAD_DIM, GROUP), F32)
    ya, st = _rwkv(za_b, shift0, s0, P["rwkv"], L, nc)
    new["rwkv"] = _from_cat(st)[None]
    new["rwkv_shift"] = za_b[:, -1][None]
    qb, kb, vb = (a.reshape(nb, tn, dm) for a in (q, k, v))
    logf_b = logf.reshape(nb, tn, N_HEADS)
    logf_t = jnp.swapaxes(logf_b, 1, 2)
    if sample:
        ck = cache["fox_k"][0].astype(BF16).reshape(nb, -1, dm)
        cv = cache["fox_v"][0].astype(BF16).reshape(nb, -1, dm)
        fc = _cumsum_time(jnp.swapaxes(cache["fox_logf"][0], 1, 2))
        yb = _cached_attn(qb, kb, vb, ck, cv, fc, logf_t, fox=True)
    else:
        yb = _fox_prompt(qb, kb, vb, _cumsum_time(logf_t), tq)
    new["fox_k"] = kb.reshape(1, nb, tn, N_HEADS, HEAD_DIM)
    new["fox_v"] = vb.reshape(1, nb, tn, N_HEADS, HEAD_DIM)
    new["fox_logf"] = logf_b[None]
    fix = None
    if sample:
        buf = cache["ffn_conv"][0]
        zero = jnp.zeros((nb, tn - 2, buf.shape[-1]), F32)
        fix = (jnp.concatenate([buf[:, 1:2], buf[:, 0:1] * 0, zero], axis=1).reshape(1, nb * tn, -1),
               jnp.concatenate([buf, zero], axis=1).reshape(1, nb * tn, -1))
    res = _tail(x, ya.reshape(b, t, dm), yb.reshape(b, t, dm), mods[0], mods[1], P["ab_wo"],
                P["norm_ffn_g"][0], P["wup"][0], P["conv_w"][0], P["conv_b"][0], P["wdn"][0],
                P["final_g"], False, tm, fix=fix, period=tn)
    x = res[0]
    conv0 = res[1].reshape(nb, tn, -1)[:, tn - 2:] if sample else res[1]
    q, k, v, zq, zf, zi, zg = _proj(x, mods[2], P["norm_mix_g"][1], P["cd_w"], (dm,) * 7, tm)
    qb, kb, vb, zq, zf, zi, zg = (a.reshape(nb, tn, dm) for a in (q, k, v, zq, zf, zi, zg))
    if sample:
        ck = cache["chunk_k"][0].astype(BF16).reshape(nb, -1, dm)
        cv = cache["chunk_v"][0].astype(BF16).reshape(nb, -1, dm)
        yc = _cached_attn(qb, kb, vb, ck, cv, P["band_bias_c"], P["band_bias_n"], fox=False)
        new["chunk_k"] = kb.reshape(1, nb, tn, N_HEADS, HEAD_DIM)
        new["chunk_v"] = vb.reshape(1, nb, tn, N_HEADS, HEAD_DIM)
        s0 = _to_cat(jnp.swapaxes(cache["hgrn"][0], -1, -2))
    else:
        yc = _band_prompt(qb, kb, vb, P["rel_bias"], 4 * CHUNK)
        keep = min(BAND, tn)
        new["chunk_k"] = kb[:, tn - keep:].reshape(1, nb, keep, N_HEADS, HEAD_DIM)
        new["chunk_v"] = vb[:, tn - keep:].reshape(1, nb, keep, N_HEADS, HEAD_DIM)
        s0 = jnp.zeros((nb, N_GROUPS, HEAD_DIM, GROUP), F32)
    yd, st = _hgrn(zq, zf, zi, zg, s0, P["hgrn_lb"], P["hgrn_norm_g"], L)
    new["hgrn"] = jnp.swapaxes(_from_cat(st), -1, -2)[None]
    if sample:
        buf = cache["ffn_conv"][1]
        fix = (jnp.concatenate([buf[:, 1:2], buf[:, 0:1] * 0, zero], axis=1).reshape(1, nb * tn, -1),
               jnp.concatenate([buf, zero], axis=1).reshape(1, nb * tn, -1))
    res = _tail(x, yc.reshape(b, t, dm), yd.reshape(b, t, dm), mods[2], mods[3], P["cd_wo"],
                P["norm_ffn_g"][1], P["wup"][1], P["conv_w"][1], P["conv_b"][1], P["wdn"][1],
                P["final_g"], True, tm, fix=fix, period=tn)
    conv1 = res[1].reshape(nb, tn, -1)[:, tn - 2:] if sample else res[1]
    new["ffn_conv"] = jnp.stack([conv0, conv1])
    y = res[0].reshape(nb, tn, d)
    return y, new


def kernel(x_prompt, x_sample, c_prompt, c_sample, cache_fox_k, cache_fox_v, cache_fox_logf, state_rwkv,
           state_rwkv_shift, cache_chunk_k, cache_chunk_v, state_hgrn, state_ffn_conv, ada_w, ada_b,
           norm_mix_g, norm_ffn_g, ab_w_in, rwkv_mu, rwkv_w0, rwkv_w2, rwkv_a0, rwkv_a2, rwkv_g2,
           rwkv_k_k, rwkv_k_a, rwkv_r_k, rwkv_lnx_g, rwkv_lnx_b, fox_b_f, ab_w_out, cd_w_in,
           chunk_rel_bias, hgrn_lb_table, hgrn_norm_g, cd_w_out, ffn_w_up, ffn_conv_w, ffn_conv_b,
           ffn_w_down, final_norm_g):
    bp, tp, d = x_prompt.shape
    bs, ts, _ = x_sample.shape
    depth = ada_w.shape[0]
    dm = D_MIX
    a_cols = rwkv_mu.shape[1]
    n_lw, n_la, n_lg = rwkv_w2.shape[1], rwkv_a2.shape[1], rwkv_g2.shape[1]

    c_all = jnp.concatenate([c_prompt, c_sample], axis=0)
    mods = _adaln(c_all, ada_w.reshape(depth * 2, d, 3 * d), ada_b.reshape(depth * 2, 3 * d))
    mods_p = [mods[i, :bp].reshape(bp, 1, 3 * d) for i in range(depth * 2)]
    mods_s = [jnp.repeat(mods[i, bp:], ts, axis=0).reshape(1, bs * ts, 3 * d) for i in range(depth * 2)]

    wl = jnp.zeros((n_lw + n_la + n_lg, 3 * dm), F32)
    wl = wl.at[:n_lw, 0:dm].set(rwkv_w2[0])
    wl = wl.at[n_lw:n_lw + n_la, dm:2 * dm].set(rwkv_a2[0])
    wl = wl.at[n_lw + n_la:, 2 * dm:].set(rwkv_g2[0])
    sm = jax.nn.softmax(hgrn_lb_table.astype(F32), axis=0)
    lb = (jnp.cumsum(sm, axis=0) - sm[0])[1]
    wf = jnp.zeros((d, 128), F32).at[:, :N_HEADS].set(ab_w_in[0][:, a_cols + 3 * dm:]).astype(BF16)
    bf = jnp.zeros((1, 128), F32).at[0, :N_HEADS].set(fox_b_f[0])
    p_c = cache_chunk_k.shape[2]
    bias_s = _toeplitz_bias(chunk_rel_bias[0], ts, p_c + ts, p_c).reshape(N_HEADS * ts, p_c + ts)

    P = {
        "a_cols": a_cols,
        "norm_mix_g": norm_mix_g, "norm_ffn_g": norm_ffn_g, "final_g": final_norm_g,
        "ab_w": ab_w_in[0][:, :a_cols + 3 * dm].astype(BF16), "ab_wf": wf, "fox_bf": bf,
        "ab_wo": ab_w_out[0].astype(BF16),
        "cd_w": cd_w_in[0].astype(BF16), "cd_wo": cd_w_out[0].astype(BF16),
        "wup": [ffn_w_up[i].astype(BF16) for i in range(depth)],
        "wdn": [ffn_w_down[i].astype(BF16) for i in range(depth)],
        "conv_w": ffn_conv_w, "conv_b": ffn_conv_b,
        "rwkv": {"mu": rwkv_mu[0], "w0": rwkv_w0[0], "a0": rwkv_a0[0], "wl": wl.astype(BF16), "k_k": rwkv_k_k[0],
                 "k_a": rwkv_k_a[0], "r_k": rwkv_r_k[0], "lnx_g": rwkv_lnx_g[0], "lnx_b": rwkv_lnx_b[0]},
        "rel_bias": chunk_rel_bias[0], "band_bias_c": bias_s[:, :p_c], "band_bias_n": bias_s[:, p_c:],
        "hgrn_lb": lb, "hgrn_norm_g": hgrn_norm_g[0],
    }
    cache = {"nb": bs, "tn": ts, "fox_k": cache_fox_k, "fox_v": cache_fox_v, "fox_logf": cache_fox_logf,
             "rwkv": state_rwkv, "rwkv_shift": state_rwkv_shift, "chunk_k": cache_chunk_k,
             "chunk_v": cache_chunk_v, "hgrn": state_hgrn, "ffn_conv": state_ffn_conv}

    y_p, sp = _trunk(x_prompt, mods_p, P, None, tm=512, tq=512, L=CHUNK, nc=4)
    y_s, ss = _trunk(x_sample.reshape(1, bs * ts, d), mods_s, P, cache, tm=bs * ts, tq=None, L=ts, nc=1)
    names = ("fox_k", "fox_v", "fox_logf", "rwkv", "rwkv_shift", "chunk_k", "chunk_v", "hgrn", "ffn_conv")
    return (y_p, y_s) + tuple(sp[n] for n in names) + tuple(ss[n] for n in names)
```

```python
import functools
import math

import jax
import jax.numpy as jnp
from jax import lax
from jax.experimental import pallas as pl
from jax.experimental.pallas import tpu as pltpu

F32 = jnp.float32
BF16 = jnp.bfloat16
HI = lax.Precision.HIGHEST

HEAD_DIM = 64
N_HEADS = 8
D_MIX = N_HEADS * HEAD_DIM
GROUP = 256
N_GROUPS = D_MIX // GROUP
HEADS_PER_GROUP = GROUP // HEAD_DIM
SUB = 16
CHUNK = 64
BAND = 512
REL_CLIP = 128
RMS_EPS = 1e-6
GN_EPS = 64e-5
ATTN_SCALE = HEAD_DIM ** -0.5
NEG = -1e30
FACTORED_MAX_DECAY = 60.0
VMEM_LIMIT = 56 * 1024 * 1024


def _dot(a, b, prec=None):
    return jnp.dot(a, b, preferred_element_type=F32, precision=prec)


def _dot_nt(a, b, prec=None):
    return lax.dot_general(a, b, (((1,), (1,)), ((), ())), preferred_element_type=F32, precision=prec)


def _dot_tn(a, b, prec=None):
    return lax.dot_general(a, b, (((0,), (0,)), ((), ())), preferred_element_type=F32, precision=prec)


_NN = ((1,), (0,))
_NT = ((1,), (1,))
_TN = ((0,), (0,))


def _sp(x, n):
    hi = x.astype(BF16)
    if n == 1:
        return (hi,)
    r = x - hi.astype(F32)
    mid = r.astype(BF16)
    if n == 2:
        return (hi, mid)
    return (hi, mid, (r - mid.astype(F32)).astype(BF16))


def bf(x):
    return (x.astype(BF16),)


def _mm(a, b, dims=_NN):
    n = max(len(a), len(b))
    out = None
    for i, ai in enumerate(a):
        for j, bj in enumerate(b):
            if i + j < n:
                d = lax.dot_general(ai, bj, (dims, ((), ())), preferred_element_type=F32)
                out = d if out is None else out + d
    return out


def _head_sum(x, bo_group, terms=2):
    return jnp.concatenate(
        [_mm(_sp(x[:, gi * GROUP:(gi + 1) * GROUP], terms), (bo_group,)) for gi in range(N_GROUPS)], axis=1)


def _sigmoid(x):
    return jax.nn.sigmoid(x)


def _silu(x):
    return x * jax.nn.sigmoid(x)


def _softplus(x):
    return jnp.maximum(x, 0.0) + jnp.log1p(jnp.exp(-jnp.abs(x)))


def _log_sigmoid(x):
    return -_softplus(-x)


def _iota(shape, dim):
    return lax.broadcasted_iota(jnp.int32, shape, dim)


def _const_spec(shape):
    nd = len(shape)
    return pl.BlockSpec(shape, lambda *_: (0,) * nd, pipeline_mode=pl.Buffered(1))


def _params(sem):
    return pltpu.CompilerParams(dimension_semantics=sem, vmem_limit_bytes=VMEM_LIMIT)


def _norm_mod(x, g, shift, scale):
    xn = x * lax.rsqrt(jnp.mean(x * x, axis=-1, keepdims=True) + RMS_EPS) * g
    return xn * (1.0 + scale) + shift


def _block_ones(n, blk):
    i = jnp.arange(n) // blk
    return (i[:, None] == i[None, :]).astype(BF16)


def _bd(x, rows_per_head, cols_per_head):
    n = HEADS_PER_GROUP
    t = jnp.concatenate([x] * n, axis=0)
    r = _iota(t.shape, 0) // rows_per_head
    c = _iota(t.shape, 1) // cols_per_head
    return jnp.where(r == c, t, 0.0)


def _unbd(m):
    out = m[0:HEAD_DIM]
    for h in range(1, HEADS_PER_GROUP):
        out = out + m[h * HEAD_DIM:(h + 1) * HEAD_DIM]
    return out


def _adaln_kernel(c_ref, w_ref, b_ref, o_ref):
    o_ref[0] = _dot(_silu(c_ref[...]), w_ref[0], HI) + b_ref[0]


def _adaln(c_all, ada_w, ada_b):
    n, d, d3 = ada_w.shape
    bt = c_all.shape[0]
    nt = d3 // d
    return pl.pallas_call(
        _adaln_kernel,
        grid=(n, nt),
        in_specs=[pl.BlockSpec((bt, d), lambda i, j: (0, 0)),
                  pl.BlockSpec((1, d, d), lambda i, j: (i, 0, j)),
                  pl.BlockSpec((1, 1, d), lambda i, j: (i, 0, j))],
        out_specs=pl.BlockSpec((1, bt, d), lambda i, j: (i, 0, j)),
        out_shape=jax.ShapeDtypeStruct((n, bt, d3), F32),
        compiler_params=_params(("parallel", "parallel")),
        name="adaln",
    )(c_all, ada_w, ada_b.reshape(n, 1, d3))


def _proj_kernel(*refs, splits, fox):
    x_ref, mod_ref, g_ref, w_ref = refs[:4]
    pos = 4
    if fox:
        wf_ref, bf_ref = refs[4:6]
        pos = 6
    outs = refs[pos:]
    d = x_ref.shape[-1]
    mod = mod_ref[0]
    h = _norm_mod(x_ref[0], g_ref[...], mod[:, :d], mod[:, d:2 * d]).astype(BF16)
    off = 0
    for o_ref, n in zip(outs, splits):
        o_ref[0] = _dot(h, w_ref[:, off:off + n])
        off += n
    if fox:
        zf = _dot(h, wf_ref[...]) + bf_ref[...]
        outs[len(splits)][0] = _log_sigmoid(zf)[:, :N_HEADS]


def _proj(x, mod, g, w_bf, splits, tm, wf=None, bf=None):
    b, t, d = x.shape
    r = mod.shape[1]
    fox = wf is not None
    mod_spec = (pl.BlockSpec((1, 1, 3 * d), lambda i, j: (i, 0, 0)) if r == 1
                else pl.BlockSpec((1, tm, 3 * d), lambda i, j: (i, j, 0)))
    in_specs = [pl.BlockSpec((1, tm, d), lambda i, j: (i, j, 0)), mod_spec,
                _const_spec((1, d)), _const_spec(w_bf.shape)]
    args = [x, mod, g.reshape(1, d), w_bf]
    widths = list(splits)
    if fox:
        in_specs += [_const_spec(wf.shape), _const_spec(bf.shape)]
        args += [wf, bf]
        widths.append(N_HEADS)
    return pl.pallas_call(
        functools.partial(_proj_kernel, splits=tuple(splits), fox=fox),
        grid=(b, t // tm),
        in_specs=in_specs,
        out_specs=[pl.BlockSpec((1, tm, n), lambda i, j: (i, j, 0)) for n in widths],
        out_shape=[jax.ShapeDtypeStruct((b, t, n), F32) for n in widths],
        compiler_params=_params(("parallel", "parallel")),
        name="proj",
    )(*args)


def _tail_kernel(*refs, sample, final, period, cb, down_group):
    (x_ref, ya_ref, yb_ref, m1_ref, m2_ref, wo_ref, g_ref, wup_ref, cw_ref, cbias_ref,
     wdn_ref, fg_ref) = refs[:12]
    if sample:
        fix1_ref, fix2_ref, o_ref, u_ref, g_sc = refs[12:]
    else:
        o_ref, cs_ref, g_sc, tail_sc = refs[12:]
    tm, d = x_ref.shape[1], x_ref.shape[2]
    ff = wdn_ref.shape[0]
    dm = ya_ref.shape[2]
    m1 = m1_ref[0]
    m2 = m2_ref[0]
    ymix = (_dot(ya_ref[0].astype(BF16), wo_ref[0:dm, :])
            + _dot(yb_ref[0].astype(BF16), wo_ref[dm:2 * dm, :]))
    x1 = x_ref[0] + m1[:, 2 * d:3 * d] * ymix
    h = _norm_mod(x1, g_ref[...], m2[:, :d], m2[:, d:2 * d]).astype(BF16)
    rows = _iota((tm, 1), 0)
    if sample:
        tpos = rows & (period - 1)
    else:
        t = pl.program_id(1)

        @pl.when(t == 0)
        def _():
            tail_sc[...] = jnp.zeros_like(tail_sc)

    def up(j):
        return [_dot(h, wup_ref[:, half * ff + j * cb:half * ff + (j + 1) * cb]) for half in range(2)]

    def conv(u, u1, u2, c0):
        cw = cw_ref[:, c0:c0 + cb]
        return cbias_ref[:, c0:c0 + cb] + cw[0:1] * u2 + cw[1:2] * u1 + cw[2:3] * u

    def gated(j, us):
        halves = []
        for half, u in enumerate(us):
            c0 = half * ff + j * cb
            if sample:
                u1 = jnp.where(tpos == 0, fix1_ref[0, :, c0:c0 + cb], pltpu.roll(u, 1, 0))
                u2 = jnp.where(tpos < 2, fix2_ref[0, :, c0:c0 + cb], pltpu.roll(u, 2, 0))
                u_ref[0, :, c0:c0 + cb] = u
                halves.append(conv(u, u1, u2, c0))
            else:
                edge = jnp.concatenate([tail_sc[:, c0:c0 + cb], u[0:8]], axis=0)
                first = conv(edge, pltpu.roll(edge, 1, 0), pltpu.roll(edge, 2, 0), c0)[8:16]
                tail_sc[:, c0:c0 + cb] = u[tm - 8:tm, :]
                main = conv(u, pltpu.roll(u, 1, 0), pltpu.roll(u, 2, 0), c0)
                halves.append(jnp.concatenate([first, main[8:]], axis=0))
        return (_silu(halves[0]) * halves[1]).astype(BF16)

    acc = None
    n_blk = ff // cb
    us = up(0)
    j0 = 0
    for j in range(n_blk):
        us_next = up(j + 1) if j + 1 < n_blk else None
        g_sc[:, j * cb:(j + 1) * cb] = gated(j, us)
        if (j + 1 - j0) == down_group or j + 1 == n_blk:
            part = _dot(g_sc[:, j0 * cb:(j + 1) * cb], wdn_ref[j0 * cb:(j + 1) * cb, :])
            acc = part if acc is None else acc + part
            j0 = j + 1
        us = us_next
    out = x1 + m2[:, 2 * d:3 * d] * acc
    if final:
        out = out * lax.rsqrt(jnp.mean(out * out, axis=-1, keepdims=True) + RMS_EPS) * fg_ref[...]
    o_ref[0] = out
    if not sample:
        @pl.when(t == pl.num_programs(1) - 1)
        def _():
            cs_ref[0] = tail_sc[6:8, :]


def _tail(x, ya, yb, mod1, mod2, wo_bf, g, wup_bf, conv_w, conv_b, wdn_bf, fg, final, tm,
          fix=None, period=None):
    b, t, d = x.shape
    ff = wdn_bf.shape[0]
    dm = ya.shape[2]
    sample = fix is not None
    r = mod1.shape[1]
    mod_spec = (pl.BlockSpec((1, 1, 3 * d), lambda i, j: (i, 0, 0)) if r == 1
                else pl.BlockSpec((1, tm, 3 * d), lambda i, j: (i, j, 0)))
    row_spec = lambda n: pl.BlockSpec((1, tm, n), lambda i, j: (i, j, 0))
    in_specs = [row_spec(d), row_spec(dm), row_spec(dm), mod_spec, mod_spec,
                _const_spec(wo_bf.shape), _const_spec((1, d)), _const_spec(wup_bf.shape),
                _const_spec(conv_w.shape), _const_spec((1, 2 * ff)), _const_spec(wdn_bf.shape),
                _const_spec((1, d))]
    args = [x, ya, yb, mod1, mod2, wo_bf, g.reshape(1, d), wup_bf, conv_w, conv_b.reshape(1, 2 * ff),
            wdn_bf, fg.reshape(1, d)]
    if sample:
        in_specs += [row_spec(2 * ff), row_spec(2 * ff)]
        args += list(fix)
        out_specs = [row_spec(d), row_spec(2 * ff)]
        out_shape = [jax.ShapeDtypeStruct((b, t, d), F32), jax.ShapeDtypeStruct((b, t, 2 * ff), F32)]
        scratch = [pltpu.VMEM((tm, ff), BF16)]
    else:
        out_specs = [row_spec(d), pl.BlockSpec((1, 2, 2 * ff), lambda i, j: (i, 0, 0))]
        out_shape = [jax.ShapeDtypeStruct((b, t, d), F32), jax.ShapeDtypeStruct((b, 2, 2 * ff), F32)]
        scratch = [pltpu.VMEM((tm, ff), BF16), pltpu.VMEM((8, 2 * ff), F32)]
    return pl.pallas_call(
        functools.partial(_tail_kernel, sample=sample, final=final, period=period, cb=256, down_group=4),
        grid=(b, t // tm),
        in_specs=in_specs,
        out_specs=out_specs,
        out_shape=out_shape,
        scratch_shapes=scratch,
        compiler_params=_params(("parallel", "arbitrary")),
        name="tail",
    )(*args)


def _rwkv_kernel(z_ref, shift0_ref, s0_ref, mu_ref, w0_ref, a0_ref, wl_ref, kk_ref, ka_ref, rk_ref,
                 lng_ref, lnb_ref, bo_ref, y_ref, st_ref, prev_sc, st_sc, *, L, nc):
    t = pl.program_id(1)
    tb = nc * L

    @pl.when(t == 0)
    def _():
        prev_sc[...] = shift0_ref[0]
        for gi in range(N_GROUPS):
            st_sc[gi] = _bd(s0_ref[0, gi], HEAD_DIM, HEAD_DIM)

    z = z_ref[0]
    rows = _iota((tb, 1), 0)
    z_prev = jnp.where(rows == 0, prev_sc[...], pltpu.roll(z, 1, 0))
    prev_sc[...] = z[tb - 1:tb, :]
    zs = z + mu_ref[...] * (z_prev - z)
    dm = D_MIX
    r = zs[:, 0:dm]
    k = zs[:, dm:2 * dm]
    v = zs[:, 2 * dm:3 * dm]
    lr = zs[:, 3 * dm:]
    nl = lr.shape[1]
    lane = _iota((1, nl), 1)
    act = jnp.where(lane < 64, jnp.tanh(lr), jnp.where(lane < 128, lr, _sigmoid(lr)))
    lo = _mm(bf(act), (wl_ref[...],))
    w = -_softplus(-(w0_ref[...] + lo[:, 0:dm])) - 0.5
    lw = -jnp.exp(w)
    a = _sigmoid(a0_ref[...] + lo[:, dm:2 * dm])
    g = lo[:, 2 * dm:3 * dm]
    bo = bo_ref[...]
    kk = k * kk_ref[...]
    k2 = k * (1.0 + (a - 1.0) * ka_ref[...])
    sums = _head_sum(jnp.concatenate([kk * kk, r * k2 * rk_ref[...]], axis=0), bo)
    kk = kk / jnp.maximum(jnp.sqrt(sums[:tb]), 1e-12)
    bonus = sums[tb:]
    am = -kk
    bm = kk * a

    ri = _iota((tb, tb), 0)
    ci_ = _iota((tb, tb), 1)
    tri = ((ci_ <= ri) & (ci_ // L == ri // L)).astype(BF16)
    c = _mm((tri,), _sp(lw, 3))
    c_last = jnp.concatenate(
        [jnp.broadcast_to(c[(j + 1) * L - 1:(j + 1) * L, :], (L, c.shape[1])) for j in range(nc)], axis=0)
    a_t = am * jnp.exp(c - lw)
    r_t = r * jnp.exp(c)
    g_inv = jnp.exp(-c)
    k_t = k2 * g_inv
    b_t = bm * g_inv
    g_end = jnp.exp(c_last - c)
    k_end = k2 * g_end
    b_end = bm * g_end
    g_last = jnp.exp(c_last)

    cw = HEADS_PER_GROUP * L
    e_tile = ((_iota((L, cw), 1) % L) == _iota((L, cw), 0)).astype(BF16)
    tmask = (_iota((GROUP, cw), 0) // HEAD_DIM) == (_iota((GROUP, cw), 1) // L)
    col_s = _iota((L, cw), 1) % L
    row_t = _iota((L, cw), 0)
    strict = col_s < row_t
    incl = col_s <= row_t
    bdmask = (_iota((GROUP, GROUP), 0) // HEAD_DIM) == (_iota((GROUP, GROUP), 1) // HEAD_DIM)
    n_dbl = int(math.log2(L))

    bd_masks = {c: (_iota((cw, 4 * c), 0) // L) == (_iota((cw, 4 * c), 1) // c) for c in {HEAD_DIM, L}}

    def bd(x, n, cols_per_head=HEAD_DIM):
        return tuple(jnp.where(bd_masks[cols_per_head], jnp.concatenate([p] * HEADS_PER_GROUP, axis=0), 0.0)
                     for p in _sp(x, n))

    eye_cat = (col_s == row_t).astype(F32)

    insts = [(cj, gi) for cj in range(nc) for gi in range(N_GROUPS)]
    rsl = {cj: slice(cj * L, (cj + 1) * L) for cj in range(nc)}
    gsl = {gi: slice(gi * GROUP, (gi + 1) * GROUP) for gi in range(N_GROUPS)}
    bk_t = {cj: _mm(bf(jnp.concatenate([b_t[rsl[cj]], k_t[rsl[cj]]], axis=1)), (e_tile,), _TN) for cj in range(nc)}
    ar = {(cj, gi): bf(jnp.concatenate([a_t[rsl[cj], gsl[gi]], r_t[rsl[cj], gsl[gi]]], axis=0)) for cj, gi in insts}
    arb = {(cj, gi): _mm(ar[cj, gi], bf(jnp.where(tmask, bk_t[cj][gsl[gi]], 0.0))) for cj, gi in insts}
    ark = {(cj, gi): _mm(ar[cj, gi], bf(jnp.where(tmask, bk_t[cj][dm + gi * GROUP:dm + (gi + 1) * GROUP], 0.0)))
           for cj, gi in insts}
    p = {k_: jnp.where(strict, arb[k_][:L], 0.0) for k_ in insts}
    m_rb = {k_: bf(jnp.where(incl, arb[k_][L:], 0.0)) for k_ in insts}
    nmv = {(cj, gi): _mm(bf(jnp.concatenate([jnp.where(strict, ark[cj, gi][:L], 0.0),
                                              jnp.where(incl, ark[cj, gi][L:], 0.0)], axis=0)),
                         bd(v[rsl[cj], gsl[gi]], 1)) for cj, gi in insts}
    tm = {k_: eye_cat + p[k_] for k_ in insts}
    for i in range(n_dbl - 1):
        for k_ in insts:
            w = bd(p[k_], 1, L)
            if i == 0:
                p[k_] = _mm(bf(p[k_]), w)
            else:
                both = _mm(bf(jnp.concatenate([p[k_], tm[k_]], axis=0)), w)
                p[k_] = both[:L]
                tm[k_] = tm[k_] + both[L:]
    tm = {k_: bf(tm[k_] + _mm(bf(tm[k_]), bd(p[k_], 1, L))) for k_ in insts}

    groups = range(N_GROUPS)
    state = [st_sc[gi] for gi in groups]
    y_rows = []
    for cj in range(nc):
        rs = rsl[cj]
        ars = [_mm(ar[cj, gi], bf(state[gi]), _NT) for gi in groups]
        u = [_mm(tm[cj, gi], bd(ars[gi][:L] + nmv[cj, gi][:L], 1)) for gi in groups]
        ys = [ars[gi][L:] + nmv[cj, gi][L:] + _mm(m_rb[cj, gi], bd(u[gi], 1)) for gi in groups]
        upd = [_mm(bf(jnp.concatenate([v[rs, gsl[gi]], u[gi]], axis=0)),
                   bf(jnp.concatenate([k_end[rs, gsl[gi]], b_end[rs, gsl[gi]]], axis=0)), _TN) for gi in groups]
        state = [state[gi] * g_last[rs, gsl[gi]][0:1] + jnp.where(bdmask, upd[gi], 0.0) for gi in groups]
        y_rows.append(jnp.concatenate(ys, axis=1))
    for gi in groups:
        st_sc[gi] = state[gi]

    y = jnp.concatenate(y_rows, axis=0) if nc > 1 else y_rows[0]
    inv = 1.0 / HEAD_DIM
    mean = _head_sum(y, bo) * inv
    dlt = y - mean
    var = _head_sum(dlt * dlt, bo) * inv
    yn = dlt * lax.rsqrt(var + GN_EPS) * lng_ref[...] + lnb_ref[...]
    y_ref[0] = (yn + bonus * v) * g

    @pl.when(t == pl.num_programs(1) - 1)
    def _():
        for gi in range(N_GROUPS):
            st_ref[0, gi] = _unbd(st_sc[gi])


def _state_spec():
    return pl.BlockSpec((1, N_GROUPS, HEAD_DIM, GROUP), lambda i, j: (i, 0, 0, 0))


def _rwkv(za, shift0, s0_cat, pr, L, nc):
    b, t, ac = za.shape
    dm = D_MIX
    tb = L * nc
    row = lambda a: a.reshape(1, -1)
    consts = [row(pr["mu"]), row(pr["w0"]), row(pr["a0"]), pr["wl"], row(pr["k_k"]), row(pr["k_a"]),
              row(pr["r_k"]), row(pr["lnx_g"]), row(pr["lnx_b"]), _block_ones(GROUP, HEAD_DIM)]
    return pl.pallas_call(
        functools.partial(_rwkv_kernel, L=L, nc=nc),
        grid=(b, t // tb),
        in_specs=[pl.BlockSpec((1, tb, ac), lambda i, j: (i, j, 0)),
                  pl.BlockSpec((1, 1, ac), lambda i, j: (i, 0, 0)), _state_spec()]
                 + [_const_spec(c.shape) for c in consts],
        out_specs=[pl.BlockSpec((1, tb, dm), lambda i, j: (i, j, 0)), _state_spec()],
        out_shape=[jax.ShapeDtypeStruct((b, t, dm), F32),
                   jax.ShapeDtypeStruct((b, N_GROUPS, HEAD_DIM, GROUP), F32)],
        scratch_shapes=[pltpu.VMEM((1, ac), F32), pltpu.VMEM((N_GROUPS, GROUP, GROUP), F32)],
        compiler_params=_params(("parallel", "arbitrary")),
        name="rwkv7",
    )(za, shift0.reshape(b, 1, ac), s0_cat, *consts)


def _hgrn_kernel(zq_ref, zf_ref, zi_ref, zg_ref, s0_ref, lb_ref, ng_ref, bo_ref, y_ref, st_ref, st_sc, *, L, nc):
    t = pl.program_id(1)
    tb = nc * L

    @pl.when(t == 0)
    def _():
        for gi in range(N_GROUPS):
            st_sc[gi] = _bd(s0_ref[0, gi], HEAD_DIM, HEAD_DIM)

    dm = D_MIX
    q = _silu(zq_ref[0])
    xf = zf_ref[0]
    v = zi_ref[0]
    lb = lb_ref[...]
    la = jnp.log(lb)
    lc = jnp.log1p(-lb) + _log_sigmoid(xf)
    logf = jnp.maximum(la, lc) + jnp.log1p(jnp.exp(-jnp.abs(la - lc)))
    k = (1.0 - lb) * _sigmoid(-xf)
    bo = bo_ref[...]

    ri = _iota((tb, tb), 0)
    ci = _iota((tb, tb), 1)
    tri = ((ci <= ri) & (ci // L == ri // L)).astype(BF16)
    b = _mm((tri,), _sp(logf, 3))
    b_last = jnp.concatenate(
        [jnp.broadcast_to(b[(j + 1) * L - 1:(j + 1) * L, :], (L, dm)) for j in range(nc)], axis=0)
    qe = q * jnp.exp(b)
    k_end = k * jnp.exp(b_last - b)
    g_last = jnp.exp(b_last)

    n_sub = L // SUB
    nst = N_HEADS * SUB
    stackmask = (_iota((nst, dm), 0) // SUB) == (_iota((nst, dm), 1) // HEAD_DIM)
    sub_rows = _iota((SUB, 1), 0)
    subs = [(cj * L, cj * L + i * SUB, i) for cj in range(nc) for i in range(n_sub)]
    rows_of = {sb: slice(sb[1], sb[1] + SUB) for sb in subs}
    brefs = {sb: jnp.zeros((1, dm), F32) if sb[2] == 0 else b[sb[1] - 1:sb[1]] for sb in subs}

    def fold_heads(ov):
        out = ov[0:SUB]
        for hh in range(1, N_HEADS):
            out = out + ov[hh * SUB:(hh + 1) * SUB]
        return out

    def stacked_q(sb):
        qh = q[rows_of[sb]] * jnp.exp(b[rows_of[sb]] - brefs[sb])
        return bf(jnp.where(stackmask, jnp.concatenate([qh] * N_HEADS, axis=0), 0.0))

    def intra_exact():
        xs = {sb: [] for sb in subs}
        for s in range(SUB):
            for sb in subs:
                bi = b[rows_of[sb]]
                e = jnp.exp(jnp.where(sub_rows >= s, bi - bi[s:s + 1], NEG))
                xs[sb].append(q[rows_of[sb]] * e * k[rows_of[sb]][s:s + 1])
        att = {sb: _head_sum(jnp.concatenate(xs[sb], axis=0), bo, terms=1) for sb in subs}
        later = [sb for sb in subs if sb[2] > 0]
        qst = {sb: stacked_q(sb) for sb in later}
        kh = {sb: bf(k[sb[0]:sb[1]] * jnp.exp(brefs[sb] - b[sb[0]:sb[1]])) for sb in later}
        att2 = {sb: _mm(qst[sb], kh[sb], _NT) for sb in later}
        ov = {sb: jnp.where(stackmask, _mm(bf(att2[sb]), bf(v[sb[0]:sb[1]])), 0.0) for sb in later}
        o_subs = []
        for sb in subs:
            vi = v[rows_of[sb]]
            oi = att[sb][0:SUB] * vi[0:1]
            for s in range(1, SUB):
                oi = oi + att[sb][s * SUB:(s + 1) * SUB] * vi[s:s + 1]
            if sb[2] > 0:
                oi = oi + fold_heads(ov[sb])
            o_subs.append(oi)
        return jnp.concatenate(o_subs, axis=0) if len(o_subs) > 1 else o_subs[0]

    def intra_factored():
        qst = {sb: stacked_q(sb) for sb in subs}
        kh = {sb: bf(k[sb[0]:sb[1] + SUB] * jnp.exp(brefs[sb] - b[sb[0]:sb[1] + SUB])) for sb in subs}
        att2 = {}
        for sb in subs:
            n_keys = sb[1] + SUB - sb[0]
            visible = _iota((nst, n_keys), 1) <= sb[2] * SUB + _iota((nst, n_keys), 0) % SUB
            att2[sb] = jnp.where(visible, _mm(qst[sb], kh[sb], _NT), 0.0)
        ov = {sb: jnp.where(stackmask, _mm(bf(att2[sb]), bf(v[sb[0]:sb[1] + SUB])), 0.0) for sb in subs}
        o_subs = [fold_heads(ov[sb]) for sb in subs]
        return jnp.concatenate(o_subs, axis=0) if len(o_subs) > 1 else o_subs[0]

    drop = jnp.concatenate([b[sb[1] + SUB - 1:sb[1] + SUB] - brefs[sb] for sb in subs], axis=0)
    o_intra = lax.cond(jnp.min(drop) >= -FACTORED_MAX_DECAY, intra_factored, intra_exact)

    groups = range(N_GROUPS)
    gsl = {gi: slice(gi * GROUP, (gi + 1) * GROUP) for gi in groups}
    bdmask = (_iota((GROUP, GROUP), 0) // HEAD_DIM) == (_iota((GROUP, GROUP), 1) // HEAD_DIM)
    upd = {(cj, gi): jnp.where(bdmask, _mm(bf(v[cj * L:(cj + 1) * L, gsl[gi]]),
                                          bf(k_end[cj * L:(cj + 1) * L, gsl[gi]]), _TN), 0.0)
           for cj in range(nc) for gi in groups}
    state = [st_sc[gi] for gi in groups]
    o_rows = []
    for cj in range(nc):
        rs = slice(cj * L, (cj + 1) * L)
        o_rows.append(jnp.concatenate([_mm(bf(qe[rs, gsl[gi]]), bf(state[gi]), _NT) for gi in groups], axis=1))
        state = [state[gi] * g_last[rs, gsl[gi]][0:1] + upd[cj, gi] for gi in groups]
    for gi in groups:
        st_sc[gi] = state[gi]
    o = o_intra + (jnp.concatenate(o_rows, axis=0) if nc > 1 else o_rows[0])

    ms = _head_sum(o * o, bo) * (1.0 / HEAD_DIM)
    y_ref[0] = o * lax.rsqrt(ms + RMS_EPS) * ng_ref[...] * _silu(zg_ref[0])

    @pl.when(t == pl.num_programs(1) - 1)
    def _():
        for gi in range(N_GROUPS):
            st_ref[0, gi] = _unbd(st_sc[gi])


def _hgrn(zq, zf, zi, zg, s0_cat, lb, norm_g, L, nc):
    b, t, dm = zq.shape
    tb = L * nc
    consts = [lb.reshape(1, dm), norm_g.reshape(1, dm), _block_ones(GROUP, HEAD_DIM)]
    row_spec = pl.BlockSpec((1, tb, dm), lambda i, j: (i, j, 0))
    return pl.pallas_call(
        functools.partial(_hgrn_kernel, L=L, nc=nc),
        grid=(b, t // tb),
        in_specs=[row_spec] * 4 + [_state_spec()] + [_const_spec(c.shape) for c in consts],
        out_specs=[row_spec, _state_spec()],
        out_shape=[jax.ShapeDtypeStruct((b, t, dm), F32),
                   jax.ShapeDtypeStruct((b, N_GROUPS, HEAD_DIM, GROUP), F32)],
        scratch_shapes=[pltpu.VMEM((N_GROUPS, GROUP, GROUP), F32)],
        compiler_params=_params(("parallel", "arbitrary")),
        name="hgrn2",
    )(zq, zf, zi, zg, s0_cat, *consts)


def _cumsum_kernel(x_ref, o_ref):
    x = x_ref[0]
    n, w = x.shape
    nblk = n // N_HEADS
    upper = (_iota((w, w), 0) <= _iota((w, w), 1)).astype(F32)
    c = _dot(x, upper, HI)
    tot = jnp.broadcast_to(c[:, w - 1:w], (n, w))
    ri = _iota((n, n), 0)
    ci = _iota((n, n), 1)
    prior = ((ri // nblk == ci // nblk) & (ci < ri)).astype(F32)
    o_ref[0] = c + _dot(prior, tot, HI)


def _cumsum_time(logf_bht):
    b, h, t = logf_bht.shape
    w = 128
    n = h * (t // w)
    x = logf_bht.reshape(b, n, w)
    out = pl.pallas_call(
        _cumsum_kernel,
        grid=(b,),
        in_specs=[pl.BlockSpec((1, n, w), lambda i: (i, 0, 0))],
        out_specs=pl.BlockSpec((1, n, w), lambda i: (i, 0, 0)),
        out_shape=jax.ShapeDtypeStruct((b, n, w), F32),
        compiler_params=_params(("parallel",)),
        name="cumsum",
    )(x)
    return out.reshape(b, h, t)


def _fox_kernel(q_ref, k_ref, v_ref, f_ref, o_ref, kb_sc, vb_sc, *, tq):
    i = pl.program_id(2)

    @pl.when(i == 0)
    def _():
        kb_sc[...] = k_ref[0].astype(BF16)
        vb_sc[...] = v_ref[0].astype(BF16)

    q = q_ref[0] * ATTN_SCALE
    lane = _iota((1, 2 * HEAD_DIM), 1)
    causal = _iota((tq, tq), 1) <= _iota((tq, tq), 0)
    outs = []
    for hh in range(2):
        head_lane = (lane // HEAD_DIM) == hh
        qm = jnp.where(head_lane, q, 0.0).astype(BF16)

        def scores(j, hh=hh, qm=qm):
            return _dot_nt(qm, kb_sc[pl.ds(j * tq, tq), :]) - f_ref[0, 0, j, hh:hh + 1, :]

        def update(j, s, m, acc, head_lane=head_lane):
            m_new = jnp.maximum(m, jnp.max(s, axis=-1, keepdims=True))
            p = jnp.exp(s - m_new).astype(BF16)
            vx = jnp.where(head_lane, vb_sc[pl.ds(j * tq, tq), :], 1.0)
            return m_new, jnp.exp(m - m_new) * acc + _dot(p, vx)

        def pair(j2, carry):
            sa, sb = scores(2 * j2), scores(2 * j2 + 1)
            return update(2 * j2 + 1, sb, *update(2 * j2, sa, *carry))

        init = (jnp.full((tq, 1), NEG, F32), jnp.zeros((tq, 2 * HEAD_DIM), F32))
        m, acc = lax.fori_loop(0, i // 2, pair, init)

        def odd_tail(m, acc):
            sa, sb = scores(i - 1), jnp.where(causal, scores(i), NEG)
            return update(i, sb, *update(i - 1, sa, m, acc))[1]

        def even_tail(m, acc):
            return update(i, jnp.where(causal, scores(i), NEG), m, acc)[1]

        acc = lax.cond(i % 2 == 1, odd_tail, even_tail, m, acc)
        outs.append(acc / pltpu.roll(acc, HEAD_DIM, 1))
    o_ref[0] = jnp.where((lane // HEAD_DIM) == 0, outs[0], outs[1])


def _fox_prompt(q, k, v, cum_bht, tq):
    b, t, dm = q.shape
    hp = dm // (2 * HEAD_DIM)
    nk = t // tq
    f = cum_bht.reshape(b, hp, 2, nk, tq).transpose(0, 1, 3, 2, 4)
    kv_spec = pl.BlockSpec((1, t, 2 * HEAD_DIM), lambda i, h, j: (i, 0, h))
    return pl.pallas_call(
        functools.partial(_fox_kernel, tq=tq),
        scratch_shapes=[pltpu.VMEM((t, 2 * HEAD_DIM), BF16), pltpu.VMEM((t, 2 * HEAD_DIM), BF16)],
        grid=(b, hp, nk),
        in_specs=[pl.BlockSpec((1, tq, 2 * HEAD_DIM), lambda i, h, j: (i, j, h)), kv_spec, kv_spec,
                  pl.BlockSpec((1, 1, nk, 2, tq), lambda i, h, j: (i, h, 0, 0, 0))],
        out_specs=pl.BlockSpec((1, tq, 2 * HEAD_DIM), lambda i, h, j: (i, j, h)),
        out_shape=jax.ShapeDtypeStruct((b, t, dm), F32),
        compiler_params=_params(("parallel", "parallel", "arbitrary")),
        name="fox_prompt",
    )(q, k, v, f)


def _band_kernel(q_ref, k_ref, v_ref, bias_ref, o_ref, *, tq):
    i = pl.program_id(2)
    q = q_ref[0] * ATTN_SCALE
    lane = _iota((1, 2 * HEAD_DIM), 1)
    n_piece = BAND // tq + 1
    outs = []
    for hh in range(2):
        qm = jnp.where((lane // HEAD_DIM) == hh, q, 0.0).astype(BF16)
        ss, vs = [], []
        for p in range(n_piece):
            blk = i - (n_piece - 1) + p
            start = jnp.maximum(blk, 0) * tq
            kb = k_ref[0, pl.ds(start, tq), :].astype(BF16)
            s = _dot_nt(qm, kb) + bias_ref[0, hh, :, p * tq:(p + 1) * tq]
            ss.append(jnp.where(blk >= 0, s, NEG))
            vs.append(v_ref[0, pl.ds(start, tq), :].astype(BF16))
        m = jnp.max(ss[0], axis=-1, keepdims=True)
        for s in ss[1:]:
            m = jnp.maximum(m, jnp.max(s, axis=-1, keepdims=True))
        l = jnp.zeros((tq, 1), F32)
        acc = jnp.zeros((tq, 2 * HEAD_DIM), F32)
        for s, vb in zip(ss, vs):
            p_ = jnp.exp(s - m)
            l = l + jnp.sum(p_, axis=-1, keepdims=True)
            acc = acc + _dot(p_.astype(BF16), vb)
        outs.append(acc / l)
    o_ref[0] = jnp.where((lane // HEAD_DIM) == 0, outs[0], outs[1])


def _toeplitz_bias(rel_bias, nq, nk, offset):
    h = rel_bias.shape[0]
    m = jnp.arange(nq + nk - 1)
    u = rel_bias[:, jnp.clip(offset + nq - 1 - m, -REL_CLIP, REL_CLIP) + REL_CLIP].astype(F32)
    period = nq + nk
    up = jnp.pad(u, ((0, 0), (0, 1)))
    skew = jnp.tile(up, (1, nq))[:, :nq * (period - 1)].reshape(h, nq, period - 1)
    return skew[:, :, nq - 1:]


def _band_bias_prompt(rel_bias, tq):
    nk = BAND + tq
    bias = _toeplitz_bias(rel_bias, tq, nk, BAND)
    qc = jnp.arange(tq)[:, None] // CHUNK
    kc = jnp.arange(nk)[None, :] // CHUNK
    valid = (kc >= qc) & (kc <= qc + BAND // CHUNK)
    return jnp.where(valid[None], bias, NEG)


def _band_prompt(q, k, v, rel_bias, tq):
    b, t, dm = q.shape
    hp = dm // (2 * HEAD_DIM)
    bias = _band_bias_prompt(rel_bias, tq).reshape(hp, 2, tq, BAND + tq)
    kv_spec = pl.BlockSpec((1, t, 2 * HEAD_DIM), lambda i, h, j: (i, 0, h))
    return pl.pallas_call(
        functools.partial(_band_kernel, tq=tq),
        grid=(b, hp, t // tq),
        in_specs=[pl.BlockSpec((1, tq, 2 * HEAD_DIM), lambda i, h, j: (i, j, h)), kv_spec, kv_spec,
                  pl.BlockSpec((1, 2, tq, BAND + tq), lambda i, h, j: (h, 0, 0, 0))],
        out_specs=pl.BlockSpec((1, tq, 2 * HEAD_DIM), lambda i, h, j: (i, j, h)),
        out_shape=jax.ShapeDtypeStruct((b, t, dm), F32),
        compiler_params=_params(("parallel", "parallel", "arbitrary")),
        name="band_prompt",
    )(q, k, v, bias)


def _cached_attn_kernel(*refs, fox):
    if fox:
        q_ref, kn_ref, vn_ref, ck_ref, cv_ref, fc_ref, fn_ref, o_ref = refs
    else:
        q_ref, kn_ref, vn_ref, ck_ref, cv_ref, bc_ref, bn_ref, o_ref = refs
    tn, dm = q_ref.shape[1], q_ref.shape[2]
    nst = N_HEADS * tn
    stackmask = (_iota((nst, dm), 0) // tn) == (_iota((nst, dm), 1) // HEAD_DIM)
    q = q_ref[0] * ATTN_SCALE
    qst = jnp.where(stackmask, jnp.concatenate([q] * N_HEADS, axis=0), 0.0).astype(BF16)
    s_c = _dot_nt(qst, ck_ref[0].astype(BF16))
    s_n = _dot_nt(qst, kn_ref[0].astype(BF16))
    if fox:
        expand = ((_iota((nst, N_HEADS), 0) // tn) == _iota((nst, N_HEADS), 1)).astype(F32)
        fc = fc_ref[0]
        upper = (_iota((tn, tn), 0) <= _iota((tn, tn), 1)).astype(F32)
        fnew = fc[:, fc.shape[1] - 1:] + _dot(fn_ref[0], upper, HI)
        s_c = s_c - _dot(expand, fc, HI)
        s_n = s_n - _dot(expand, fnew, HI)
        tq = _iota((nst, tn), 0) % tn
        s_n = jnp.where(_iota((nst, tn), 1) <= tq, s_n, NEG)
    else:
        s_c = s_c + bc_ref[...]
        s_n = s_n + bn_ref[...]
    m = jnp.maximum(jnp.max(s_c, axis=-1, keepdims=True), jnp.max(s_n, axis=-1, keepdims=True))
    p_c = jnp.exp(s_c - m)
    p_n = jnp.exp(s_n - m)
    l = jnp.sum(p_c, axis=-1, keepdims=True) + jnp.sum(p_n, axis=-1, keepdims=True)
    ov = _dot(p_c.astype(BF16), cv_ref[0].astype(BF16)) + _dot(p_n.astype(BF16), vn_ref[0].astype(BF16))
    ov = jnp.where(stackmask, ov / l, 0.0)
    out = ov[0:tn]
    for hh in range(1, N_HEADS):
        out = out + ov[hh * tn:(hh + 1) * tn]
    o_ref[0] = out


def _cached_attn(q, kn, vn, ck, cv, extra_c, extra_n, fox):
    b, tn, dm = q.shape
    p = ck.shape[1]
    new_spec = pl.BlockSpec((1, tn, dm), lambda i: (i, 0, 0))
    cache_spec = pl.BlockSpec((1, p, dm), lambda i: (i, 0, 0))
    if fox:
        ex_specs = [pl.BlockSpec((1, N_HEADS, p), lambda i: (i, 0, 0)),
                    pl.BlockSpec((1, N_HEADS, tn), lambda i: (i, 0, 0))]
    else:
        ex_specs = [_const_spec(extra_c.shape), _const_spec(extra_n.shape)]
    return pl.pallas_call(
        functools.partial(_cached_attn_kernel, fox=fox),
        grid=(b,),
        in_specs=[new_spec, new_spec, new_spec, cache_spec, cache_spec] + ex_specs,
        out_specs=new_spec,
        out_shape=jax.ShapeDtypeStruct((b, tn, dm), F32),
        compiler_params=_params(("parallel",)),
        name="fox_sample" if fox else "band_sample",
    )(q, kn, vn, ck, cv, extra_c, extra_n)


def _to_cat(s):
    b = s.shape[0]
    s5 = s.reshape(b, N_GROUPS, HEADS_PER_GROUP, HEAD_DIM, HEAD_DIM)
    return s5.transpose(0, 1, 3, 2, 4).reshape(b, N_GROUPS, HEAD_DIM, GROUP)


def _from_cat(c):
    b = c.shape[0]
    c5 = c.reshape(b, N_GROUPS, HEAD_DIM, HEADS_PER_GROUP, HEAD_DIM)
    return c5.transpose(0, 1, 3, 2, 4).reshape(b, N_HEADS, HEAD_DIM, HEAD_DIM)


def _trunk(x, mods, P, cache, tm, tq, L, nc):
    b, t, d = x.shape
    dm = D_MIX
    sample = cache is not None
    new = {}
    za, q, k, v, logf = _proj(x, mods[0], P["norm_mix_g"][0], P["ab_w"], (P["a_cols"], dm, dm, dm), tm,
                              wf=P["ab_wf"], bf=P["fox_bf"])
    ac = P["a_cols"]
    if sample:
        nb, tn = cache["nb"], cache["tn"]
        za_b = za.reshape(nb, tn, ac)
        shift0 = cache["rwkv_shift"][0]
        s0 = _to_cat(cache["rwkv"][0])
    else:
        nb, tn = b, t
        za_b = za
        shift0 = jnp.zeros((nb, ac), F32)
        s0 = jnp.zeros((nb, N_GROUPS, HEAD_DIM, GROUP), F32)
    ya, st = _rwkv(za_b, shift0, s0, P["rwkv"], L, nc)
    new["rwkv"] = _from_cat(st)[None]
    new["rwkv_shift"] = za_b[:, -1][None]
    qb, kb, vb = (a.reshape(nb, tn, dm) for a in (q, k, v))
    logf_b = logf.reshape(nb, tn, N_HEADS)
    logf_t = jnp.swapaxes(logf_b, 1, 2)
    if sample:
        ck = cache["fox_k"][0].reshape(nb, -1, dm)
        cv = cache["fox_v"][0].reshape(nb, -1, dm)
        fc = _cumsum_time(jnp.swapaxes(cache["fox_logf"][0], 1, 2))
        yb = _cached_attn(qb, kb, vb, ck, cv, fc, logf_t, fox=True)
    else:
        yb = _fox_prompt(qb, kb, vb, _cumsum_time(logf_t), tq)
    new["fox_k"] = kb.reshape(1, nb, tn, N_HEADS, HEAD_DIM)
    new["fox_v"] = vb.reshape(1, nb, tn, N_HEADS, HEAD_DIM)
    new["fox_logf"] = logf_b[None]
    fix = None
    if sample:
        buf = cache["ffn_conv"][0]
        zero = jnp.zeros((nb, tn - 2, buf.shape[-1]), F32)
        fix = (jnp.concatenate([buf[:, 1:2], buf[:, 0:1] * 0, zero], axis=1).reshape(1, nb * tn, -1),
               jnp.concatenate([buf, zero], axis=1).reshape(1, nb * tn, -1))
    res = _tail(x, ya.reshape(b, t, dm), yb.reshape(b, t, dm), mods[0], mods[1], P["ab_wo"],
                P["norm_ffn_g"][0], P["wup"][0], P["conv_w"][0], P["conv_b"][0], P["wdn"][0],
                P["final_g"], False, tm, fix=fix, period=tn)
    x = res[0]
    conv0 = res[1].reshape(nb, tn, -1)[:, tn - 2:] if sample else res[1]
    q, k, v, zq, zf, zi, zg = _proj(x, mods[2], P["norm_mix_g"][1], P["cd_w"], (dm,) * 7, tm)
    qb, kb, vb, zq, zf, zi, zg = (a.reshape(nb, tn, dm) for a in (q, k, v, zq, zf, zi, zg))
    if sample:
        ck = cache["chunk_k"][0].reshape(nb, -1, dm)
        cv = cache["chunk_v"][0].reshape(nb, -1, dm)
        yc = _cached_attn(qb, kb, vb, ck, cv, P["band_bias_c"], P["band_bias_n"], fox=False)
        new["chunk_k"] = kb.reshape(1, nb, tn, N_HEADS, HEAD_DIM)
        new["chunk_v"] = vb.reshape(1, nb, tn, N_HEADS, HEAD_DIM)
        s0 = _to_cat(jnp.swapaxes(cache["hgrn"][0], -1, -2))
    else:
        yc = _band_prompt(qb, kb, vb, P["rel_bias"], 4 * CHUNK)
        keep = min(BAND, tn)
        new["chunk_k"] = kb[:, tn - keep:].reshape(1, nb, keep, N_HEADS, HEAD_DIM)
        new["chunk_v"] = vb[:, tn - keep:].reshape(1, nb, keep, N_HEADS, HEAD_DIM)
        s0 = jnp.zeros((nb, N_GROUPS, HEAD_DIM, GROUP), F32)
    yd, st = _hgrn(zq, zf, zi, zg, s0, P["hgrn_lb"], P["hgrn_norm_g"], L, nc)
    new["hgrn"] = jnp.swapaxes(_from_cat(st), -1, -2)[None]
    if sample:
        buf = cache["ffn_conv"][1]
        fix = (jnp.concatenate([buf[:, 1:2], buf[:, 0:1] * 0, zero], axis=1).reshape(1, nb * tn, -1),
               jnp.concatenate([buf, zero], axis=1).reshape(1, nb * tn, -1))
    res = _tail(x, yc.reshape(b, t, dm), yd.reshape(b, t, dm), mods[2], mods[3], P["cd_wo"],
                P["norm_ffn_g"][1], P["wup"][1], P["conv_w"][1], P["conv_b"][1], P["wdn"][1],
                P["final_g"], True, tm, fix=fix, period=tn)
    conv1 = res[1].reshape(nb, tn, -1)[:, tn - 2:] if sample else res[1]
    new["ffn_conv"] = jnp.stack([conv0, conv1])
    y = res[0].reshape(nb, tn, d)
    return y, new


def kernel(x_prompt, x_sample, c_prompt, c_sample, cache_fox_k, cache_fox_v, cache_fox_logf, state_rwkv,
           state_rwkv_shift, cache_chunk_k, cache_chunk_v, state_hgrn, state_ffn_conv, ada_w, ada_b,
           norm_mix_g, norm_ffn_g, ab_w_in, rwkv_mu, rwkv_w0, rwkv_w2, rwkv_a0, rwkv_a2, rwkv_g2,
           rwkv_k_k, rwkv_k_a, rwkv_r_k, rwkv_lnx_g, rwkv_lnx_b, fox_b_f, ab_w_out, cd_w_in,
           chunk_rel_bias, hgrn_lb_table, hgrn_norm_g, cd_w_out, ffn_w_up, ffn_conv_w, ffn_conv_b,
           ffn_w_down, final_norm_g):
    bp, tp, d = x_prompt.shape
    bs, ts, _ = x_sample.shape
    depth = ada_w.shape[0]
    dm = D_MIX
    a_cols = rwkv_mu.shape[1]
    n_lw, n_la, n_lg = rwkv_w2.shape[1], rwkv_a2.shape[1], rwkv_g2.shape[1]

    c_all = jnp.concatenate([c_prompt, c_sample], axis=0)
    mods = _adaln(c_all, ada_w.reshape(depth * 2, d, 3 * d), ada_b.reshape(depth * 2, 3 * d))
    mods_p = [mods[i, :bp].reshape(bp, 1, 3 * d) for i in range(depth * 2)]
    mods_s = [jnp.repeat(mods[i, bp:], ts, axis=0).reshape(1, bs * ts, 3 * d) for i in range(depth * 2)]

    wl = jnp.zeros((n_lw + n_la + n_lg, 3 * dm), F32)
    wl = wl.at[:n_lw, 0:dm].set(rwkv_w2[0])
    wl = wl.at[n_lw:n_lw + n_la, dm:2 * dm].set(rwkv_a2[0])
    wl = wl.at[n_lw + n_la:, 2 * dm:].set(rwkv_g2[0])
    sm = jax.nn.softmax(hgrn_lb_table.astype(F32), axis=0)
    lb = (jnp.cumsum(sm, axis=0) - sm[0])[1]
    wf = jnp.zeros((d, 128), F32).at[:, :N_HEADS].set(ab_w_in[0][:, a_cols + 3 * dm:]).astype(BF16)
    bf = jnp.zeros((1, 128), F32).at[0, :N_HEADS].set(fox_b_f[0])
    p_c = cache_chunk_k.shape[2]
    bias_s = _toeplitz_bias(chunk_rel_bias[0], ts, p_c + ts, p_c).reshape(N_HEADS * ts, p_c + ts)

    P = {
        "a_cols": a_cols,
        "norm_mix_g": norm_mix_g, "norm_ffn_g": norm_ffn_g, "final_g": final_norm_g,
        "ab_w": ab_w_in[0][:, :a_cols + 3 * dm].astype(BF16), "ab_wf": wf, "fox_bf": bf,
        "ab_wo": ab_w_out[0].astype(BF16),
        "cd_w": cd_w_in[0].astype(BF16), "cd_wo": cd_w_out[0].astype(BF16),
        "wup": [ffn_w_up[i].astype(BF16) for i in range(depth)],
        "wdn": [ffn_w_down[i].astype(BF16) for i in range(depth)],
        "conv_w": ffn_conv_w, "conv_b": ffn_conv_b,
        "rwkv": {"mu": rwkv_mu[0], "w0": rwkv_w0[0], "a0": rwkv_a0[0], "wl": wl.astype(BF16), "k_k": rwkv_k_k[0],
                 "k_a": rwkv_k_a[0], "r_k": rwkv_r_k[0], "lnx_g": rwkv_lnx_g[0], "lnx_b": rwkv_lnx_b[0]},
        "rel_bias": chunk_rel_bias[0], "band_bias_c": bias_s[:, :p_c], "band_bias_n": bias_s[:, p_c:],
        "hgrn_lb": lb, "hgrn_norm_g": hgrn_norm_g[0],
    }
    cache = {"nb": bs, "tn": ts, "fox_k": cache_fox_k, "fox_v": cache_fox_v, "fox_logf": cache_fox_logf,
             "rwkv": state_rwkv, "rwkv_shift": state_rwkv_shift, "chunk_k": cache_chunk_k,
             "chunk_v": cache_chunk_v, "hgrn": state_hgrn, "ffn_conv": state_ffn_conv}

    y_p, sp = _trunk(x_prompt, mods_p, P, None, tm=512, tq=512, L=CHUNK, nc=4)
    y_s, ss = _trunk(x_sample.reshape(1, bs * ts, d), mods_s, P, cache, tm=bs * ts, tq=None, L=ts, nc=1)
    names = ("fox_k", "fox_v", "fox_logf", "rwkv", "rwkv_shift", "chunk_k", "chunk_v", "hgrn", "ffn_conv")
    return (y_p, y_s) + tuple(sp[n] for n in names) + tuple(ss[n] for n in names)
```

```python
import functools
import math

import jax
import jax.numpy as jnp
from jax import lax
from jax.experimental import pallas as pl
from jax.experimental.pallas import tpu as pltpu

F32 = jnp.float32
BF16 = jnp.bfloat16
HI = lax.Precision.HIGHEST

HEAD_DIM = 64
N_HEADS = 8
D_MIX = N_HEADS * HEAD_DIM
GROUP = 256
N_GROUPS = D_MIX // GROUP
HEADS_PER_GROUP = GROUP // HEAD_DIM
SUB = 16
CHUNK = 64
BAND = 512
REL_CLIP = 128
RMS_EPS = 1e-6
GN_EPS = 64e-5
ATTN_SCALE = HEAD_DIM ** -0.5
NEG = -1e30
FACTORED_MAX_DECAY = 60.0
FOX_GROUP = 4
VMEM_LIMIT = 56 * 1024 * 1024


def _dot(a, b, prec=None):
    return jnp.dot(a, b, preferred_element_type=F32, precision=prec)


def _dot_nt(a, b, prec=None):
    return lax.dot_general(a, b, (((1,), (1,)), ((), ())), preferred_element_type=F32, precision=prec)


def _dot_tn(a, b, prec=None):
    return lax.dot_general(a, b, (((0,), (0,)), ((), ())), preferred_element_type=F32, precision=prec)


_NN = ((1,), (0,))
_NT = ((1,), (1,))
_TN = ((0,), (0,))


def _sp(x, n):
    hi = x.astype(BF16)
    if n == 1:
        return (hi,)
    r = x - hi.astype(F32)
    mid = r.astype(BF16)
    if n == 2:
        return (hi, mid)
    return (hi, mid, (r - mid.astype(F32)).astype(BF16))


def bf(x):
    return (x.astype(BF16),)


def _mm(a, b, dims=_NN):
    n = max(len(a), len(b))
    out = None
    for i, ai in enumerate(a):
        for j, bj in enumerate(b):
            if i + j < n:
                d = lax.dot_general(ai, bj, (dims, ((), ())), preferred_element_type=F32)
                out = d if out is None else out + d
    return out


def _head_sum(x, bo_group, terms=2):
    return jnp.concatenate(
        [_mm(_sp(x[:, gi * GROUP:(gi + 1) * GROUP], terms), (bo_group,)) for gi in range(N_GROUPS)], axis=1)


def _sigmoid(x):
    return jax.nn.sigmoid(x)


def _silu(x):
    return x * jax.nn.sigmoid(x)


def _softplus(x):
    return jnp.maximum(x, 0.0) + jnp.log1p(jnp.exp(-jnp.abs(x)))


def _log_sigmoid(x):
    return -_softplus(-x)


def _iota(shape, dim):
    return lax.broadcasted_iota(jnp.int32, shape, dim)


def _const_spec(shape):
    nd = len(shape)
    return pl.BlockSpec(shape, lambda *_: (0,) * nd, pipeline_mode=pl.Buffered(1))


def _params(sem):
    return pltpu.CompilerParams(dimension_semantics=sem, vmem_limit_bytes=VMEM_LIMIT)


def _norm_mod(x, g, shift, scale):
    xn = x * lax.rsqrt(jnp.mean(x * x, axis=-1, keepdims=True) + RMS_EPS) * g
    return xn * (1.0 + scale) + shift


def _block_ones(n, blk):
    i = jnp.arange(n) // blk
    return (i[:, None] == i[None, :]).astype(BF16)


def _bd(x, rows_per_head, cols_per_head):
    n = HEADS_PER_GROUP
    t = jnp.concatenate([x] * n, axis=0)
    r = _iota(t.shape, 0) // rows_per_head
    c = _iota(t.shape, 1) // cols_per_head
    return jnp.where(r == c, t, 0.0)


def _unbd(m):
    out = m[0:HEAD_DIM]
    for h in range(1, HEADS_PER_GROUP):
        out = out + m[h * HEAD_DIM:(h + 1) * HEAD_DIM]
    return out


def _adaln_kernel(c_ref, w_ref, b_ref, o_ref):
    o_ref[0] = _dot(_silu(c_ref[...]), w_ref[0], HI) + b_ref[0]


def _adaln(c_all, ada_w, ada_b):
    n, d, d3 = ada_w.shape
    bt = c_all.shape[0]
    nt = d3 // d
    return pl.pallas_call(
        _adaln_kernel,
        grid=(n, nt),
        in_specs=[pl.BlockSpec((bt, d), lambda i, j: (0, 0)),
                  pl.BlockSpec((1, d, d), lambda i, j: (i, 0, j)),
                  pl.BlockSpec((1, 1, d), lambda i, j: (i, 0, j))],
        out_specs=pl.BlockSpec((1, bt, d), lambda i, j: (i, 0, j)),
        out_shape=jax.ShapeDtypeStruct((n, bt, d3), F32),
        compiler_params=_params(("parallel", "parallel")),
        name="adaln",
    )(c_all, ada_w, ada_b.reshape(n, 1, d3))


def _proj_kernel(*refs, splits, fox):
    x_ref, mod_ref, g_ref, w_ref = refs[:4]
    pos = 4
    if fox:
        wf_ref, bf_ref = refs[4:6]
        pos = 6
    outs = refs[pos:]
    d = x_ref.shape[-1]
    mod = mod_ref[0]
    h = _norm_mod(x_ref[0], g_ref[...], mod[:, :d], mod[:, d:2 * d]).astype(BF16)
    off = 0
    for o_ref, n in zip(outs, splits):
        o_ref[0] = _dot(h, w_ref[:, off:off + n])
        off += n
    if fox:
        zf = _dot(h, wf_ref[...]) + bf_ref[...]
        outs[len(splits)][0] = _log_sigmoid(zf)[:, :N_HEADS]


def _proj(x, mod, g, w_bf, splits, tm, wf=None, bf=None):
    b, t, d = x.shape
    r = mod.shape[1]
    fox = wf is not None
    mod_spec = (pl.BlockSpec((1, 1, 3 * d), lambda i, j: (i, 0, 0)) if r == 1
                else pl.BlockSpec((1, tm, 3 * d), lambda i, j: (i, j, 0)))
    in_specs = [pl.BlockSpec((1, tm, d), lambda i, j: (i, j, 0)), mod_spec,
                _const_spec((1, d)), _const_spec(w_bf.shape)]
    args = [x, mod, g.reshape(1, d), w_bf]
    widths = list(splits)
    if fox:
        in_specs += [_const_spec(wf.shape), _const_spec(bf.shape)]
        args += [wf, bf]
        widths.append(N_HEADS)
    return pl.pallas_call(
        functools.partial(_proj_kernel, splits=tuple(splits), fox=fox),
        grid=(b, t // tm),
        in_specs=in_specs,
        out_specs=[pl.BlockSpec((1, tm, n), lambda i, j: (i, j, 0)) for n in widths],
        out_shape=[jax.ShapeDtypeStruct((b, t, n), F32) for n in widths],
        compiler_params=_params(("parallel", "parallel")),
        name="proj",
    )(*args)


def _tail_kernel(*refs, sample, final, period, cb, down_group):
    (x_ref, ya_ref, yb_ref, m1_ref, m2_ref, wo_ref, g_ref, wup_ref, cw_ref, cbias_ref,
     wdn_ref, fg_ref) = refs[:12]
    if sample:
        fix1_ref, fix2_ref, o_ref, u_ref, g_sc = refs[12:]
    else:
        o_ref, cs_ref, g_sc, tail_sc = refs[12:]
    tm, d = x_ref.shape[1], x_ref.shape[2]
    ff = wdn_ref.shape[0]
    dm = ya_ref.shape[2]
    m1 = m1_ref[0]
    m2 = m2_ref[0]
    ymix = (_dot(ya_ref[0].astype(BF16), wo_ref[0:dm, :])
            + _dot(yb_ref[0].astype(BF16), wo_ref[dm:2 * dm, :]))
    x1 = x_ref[0] + m1[:, 2 * d:3 * d] * ymix
    h = _norm_mod(x1, g_ref[...], m2[:, :d], m2[:, d:2 * d]).astype(BF16)
    rows = _iota((tm, 1), 0)
    if sample:
        tpos = rows & (period - 1)
    else:
        t = pl.program_id(1)

        @pl.when(t == 0)
        def _():
            tail_sc[...] = jnp.zeros_like(tail_sc)

    def up(j):
        return [_dot(h, wup_ref[:, half * ff + j * cb:half * ff + (j + 1) * cb]) for half in range(2)]

    def conv(u, u1, u2, c0):
        cw = cw_ref[:, c0:c0 + cb]
        return cbias_ref[:, c0:c0 + cb] + cw[0:1] * u2 + cw[1:2] * u1 + cw[2:3] * u

    def gated(j, us):
        halves = []
        for half, u in enumerate(us):
            c0 = half * ff + j * cb
            if sample:
                u1 = jnp.where(tpos == 0, fix1_ref[0, :, c0:c0 + cb], pltpu.roll(u, 1, 0))
                u2 = jnp.where(tpos < 2, fix2_ref[0, :, c0:c0 + cb], pltpu.roll(u, 2, 0))
                u_ref[0, :, c0:c0 + cb] = u
                halves.append(conv(u, u1, u2, c0))
            else:
                edge = jnp.concatenate([tail_sc[:, c0:c0 + cb], u[0:8]], axis=0)
                first = conv(edge, pltpu.roll(edge, 1, 0), pltpu.roll(edge, 2, 0), c0)[8:16]
                tail_sc[:, c0:c0 + cb] = u[tm - 8:tm, :]
                main = conv(u, pltpu.roll(u, 1, 0), pltpu.roll(u, 2, 0), c0)
                halves.append(jnp.concatenate([first, main[8:]], axis=0))
        return (_silu(halves[0]) * halves[1]).astype(BF16)

    acc = None
    n_blk = ff // cb
    us = up(0)
    j0 = 0
    for j in range(n_blk):
        us_next = up(j + 1) if j + 1 < n_blk else None
        g_sc[:, j * cb:(j + 1) * cb] = gated(j, us)
        if (j + 1 - j0) == down_group or j + 1 == n_blk:
            part = _dot(g_sc[:, j0 * cb:(j + 1) * cb], wdn_ref[j0 * cb:(j + 1) * cb, :])
            acc = part if acc is None else acc + part
            j0 = j + 1
        us = us_next
    out = x1 + m2[:, 2 * d:3 * d] * acc
    if final:
        out = out * lax.rsqrt(jnp.mean(out * out, axis=-1, keepdims=True) + RMS_EPS) * fg_ref[...]
    o_ref[0] = out
    if not sample:
        @pl.when(t == pl.num_programs(1) - 1)
        def _():
            cs_ref[0] = tail_sc[6:8, :]


def _tail(x, ya, yb, mod1, mod2, wo_bf, g, wup_bf, conv_w, conv_b, wdn_bf, fg, final, tm,
          fix=None, period=None):
    b, t, d = x.shape
    ff = wdn_bf.shape[0]
    dm = ya.shape[2]
    sample = fix is not None
    r = mod1.shape[1]
    mod_spec = (pl.BlockSpec((1, 1, 3 * d), lambda i, j: (i, 0, 0)) if r == 1
                else pl.BlockSpec((1, tm, 3 * d), lambda i, j: (i, j, 0)))
    row_spec = lambda n: pl.BlockSpec((1, tm, n), lambda i, j: (i, j, 0))
    in_specs = [row_spec(d), row_spec(dm), row_spec(dm), mod_spec, mod_spec,
                _const_spec(wo_bf.shape), _const_spec((1, d)), _const_spec(wup_bf.shape),
                _const_spec(conv_w.shape), _const_spec((1, 2 * ff)), _const_spec(wdn_bf.shape),
                _const_spec((1, d))]
    args = [x, ya, yb, mod1, mod2, wo_bf, g.reshape(1, d), wup_bf, conv_w, conv_b.reshape(1, 2 * ff),
            wdn_bf, fg.reshape(1, d)]
    if sample:
        in_specs += [row_spec(2 * ff), row_spec(2 * ff)]
        args += list(fix)
        out_specs = [row_spec(d), row_spec(2 * ff)]
        out_shape = [jax.ShapeDtypeStruct((b, t, d), F32), jax.ShapeDtypeStruct((b, t, 2 * ff), F32)]
        scratch = [pltpu.VMEM((tm, ff), BF16)]
    else:
        out_specs = [row_spec(d), pl.BlockSpec((1, 2, 2 * ff), lambda i, j: (i, 0, 0))]
        out_shape = [jax.ShapeDtypeStruct((b, t, d), F32), jax.ShapeDtypeStruct((b, 2, 2 * ff), F32)]
        scratch = [pltpu.VMEM((tm, ff), BF16), pltpu.VMEM((8, 2 * ff), F32)]
    return pl.pallas_call(
        functools.partial(_tail_kernel, sample=sample, final=final, period=period, cb=256, down_group=4),
        grid=(b, t // tm),
        in_specs=in_specs,
        out_specs=out_specs,
        out_shape=out_shape,
        scratch_shapes=scratch,
        compiler_params=_params(("parallel", "arbitrary")),
        name="tail",
    )(*args)


def _rwkv_kernel(z_ref, shift0_ref, s0_ref, mu_ref, w0_ref, a0_ref, wl_ref, kk_ref, ka_ref, rk_ref,
                 lng_ref, lnb_ref, bo_ref, y_ref, st_ref, prev_sc, st_sc, *, L, nc):
    t = pl.program_id(1)
    tb = nc * L

    @pl.when(t == 0)
    def _():
        prev_sc[...] = shift0_ref[0]
        for gi in range(N_GROUPS):
            st_sc[gi] = _bd(s0_ref[0, gi], HEAD_DIM, HEAD_DIM)

    z = z_ref[0]
    rows = _iota((tb, 1), 0)
    z_prev = jnp.where(rows == 0, prev_sc[...], pltpu.roll(z, 1, 0))
    prev_sc[...] = z[tb - 1:tb, :]
    zs = z + mu_ref[...] * (z_prev - z)
    dm = D_MIX
    r = zs[:, 0:dm]
    k = zs[:, dm:2 * dm]
    v = zs[:, 2 * dm:3 * dm]
    lr = zs[:, 3 * dm:]
    nl = lr.shape[1]
    lane = _iota((1, nl), 1)
    act = jnp.where(lane < 64, jnp.tanh(lr), jnp.where(lane < 128, lr, _sigmoid(lr)))
    lo = _mm(bf(act), (wl_ref[...],))
    w = -_softplus(-(w0_ref[...] + lo[:, 0:dm])) - 0.5
    lw = -jnp.exp(w)
    a = _sigmoid(a0_ref[...] + lo[:, dm:2 * dm])
    g = lo[:, 2 * dm:3 * dm]
    bo = bo_ref[...]
    kk = k * kk_ref[...]
    k2 = k * (1.0 + (a - 1.0) * ka_ref[...])
    sums = _head_sum(jnp.concatenate([kk * kk, r * k2 * rk_ref[...]], axis=0), bo)
    kk = kk / jnp.maximum(jnp.sqrt(sums[:tb]), 1e-12)
    bonus = sums[tb:]
    am = -kk
    bm = kk * a

    ri = _iota((tb, tb), 0)
    ci_ = _iota((tb, tb), 1)
    tri = ((ci_ <= ri) & (ci_ // L == ri // L)).astype(BF16)
    c = _mm((tri,), _sp(lw, 3))
    c_last = jnp.concatenate(
        [jnp.broadcast_to(c[(j + 1) * L - 1:(j + 1) * L, :], (L, c.shape[1])) for j in range(nc)], axis=0)
    a_t = am * jnp.exp(c - lw)
    r_t = r * jnp.exp(c)
    g_inv = jnp.exp(-c)
    k_t = k2 * g_inv
    b_t = bm * g_inv
    g_end = jnp.exp(c_last - c)
    k_end = k2 * g_end
    b_end = bm * g_end
    g_last = jnp.exp(c_last)

    cw = HEADS_PER_GROUP * L
    e_tile = ((_iota((L, cw), 1) % L) == _iota((L, cw), 0)).astype(BF16)
    tmask = (_iota((GROUP, cw), 0) // HEAD_DIM) == (_iota((GROUP, cw), 1) // L)
    col_s = _iota((L, cw), 1) % L
    row_t = _iota((L, cw), 0)
    strict = col_s < row_t
    incl = col_s <= row_t
    bdmask = (_iota((GROUP, GROUP), 0) // HEAD_DIM) == (_iota((GROUP, GROUP), 1) // HEAD_DIM)
    n_dbl = int(math.log2(L))

    bd_masks = {c: (_iota((cw, 4 * c), 0) // L) == (_iota((cw, 4 * c), 1) // c) for c in {HEAD_DIM, L}}

    def bd(x, n, cols_per_head=HEAD_DIM):
        return tuple(jnp.where(bd_masks[cols_per_head], jnp.concatenate([p] * HEADS_PER_GROUP, axis=0), 0.0)
                     for p in _sp(x, n))

    eye_cat = (col_s == row_t).astype(F32)

    insts = [(cj, gi) for cj in range(nc) for gi in range(N_GROUPS)]
    rsl = {cj: slice(cj * L, (cj + 1) * L) for cj in range(nc)}
    gsl = {gi: slice(gi * GROUP, (gi + 1) * GROUP) for gi in range(N_GROUPS)}
    bk_t = {cj: _mm(bf(jnp.concatenate([b_t[rsl[cj]], k_t[rsl[cj]]], axis=1)), (e_tile,), _TN) for cj in range(nc)}
    ar = {(cj, gi): bf(jnp.concatenate([a_t[rsl[cj], gsl[gi]], r_t[rsl[cj], gsl[gi]]], axis=0)) for cj, gi in insts}
    arb = {(cj, gi): _mm(ar[cj, gi], bf(jnp.where(tmask, bk_t[cj][gsl[gi]], 0.0))) for cj, gi in insts}
    ark = {(cj, gi): _mm(ar[cj, gi], bf(jnp.where(tmask, bk_t[cj][dm + gi * GROUP:dm + (gi + 1) * GROUP], 0.0)))
           for cj, gi in insts}
    p = {k_: jnp.where(strict, arb[k_][:L], 0.0) for k_ in insts}
    m_rb = {k_: bf(jnp.where(incl, arb[k_][L:], 0.0)) for k_ in insts}
    nmv = {(cj, gi): _mm(bf(jnp.concatenate([jnp.where(strict, ark[cj, gi][:L], 0.0),
                                              jnp.where(incl, ark[cj, gi][L:], 0.0)], axis=0)),
                         bd(v[rsl[cj], gsl[gi]], 1)) for cj, gi in insts}
    tm = {k_: eye_cat + p[k_] for k_ in insts}
    for i in range(n_dbl - 1):
        for k_ in insts:
            w = bd(p[k_], 1, L)
            if i == 0:
                p[k_] = _mm(bf(p[k_]), w)
            else:
                both = _mm(bf(jnp.concatenate([p[k_], tm[k_]], axis=0)), w)
                p[k_] = both[:L]
                tm[k_] = tm[k_] + both[L:]
    tm = {k_: bf(tm[k_] + _mm(bf(tm[k_]), bd(p[k_], 1, L))) for k_ in insts}

    groups = range(N_GROUPS)
    state = [st_sc[gi] for gi in groups]
    y_rows = []
    for cj in range(nc):
        rs = rsl[cj]
        ars = [_mm(ar[cj, gi], bf(state[gi]), _NT) for gi in groups]
        u = [_mm(tm[cj, gi], bd(ars[gi][:L] + nmv[cj, gi][:L], 1)) for gi in groups]
        ys = [ars[gi][L:] + nmv[cj, gi][L:] + _mm(m_rb[cj, gi], bd(u[gi], 1)) for gi in groups]
        upd = [_mm(bf(jnp.concatenate([v[rs, gsl[gi]], u[gi]], axis=0)),
                   bf(jnp.concatenate([k_end[rs, gsl[gi]], b_end[rs, gsl[gi]]], axis=0)), _TN) for gi in groups]
        state = [state[gi] * g_last[rs, gsl[gi]][0:1] + jnp.where(bdmask, upd[gi], 0.0) for gi in groups]
        y_rows.append(jnp.concatenate(ys, axis=1))
    for gi in groups:
        st_sc[gi] = state[gi]

    y = jnp.concatenate(y_rows, axis=0) if nc > 1 else y_rows[0]
    inv = 1.0 / HEAD_DIM
    mean = _head_sum(y, bo) * inv
    dlt = y - mean
    var = _head_sum(dlt * dlt, bo) * inv
    yn = dlt * lax.rsqrt(var + GN_EPS) * lng_ref[...] + lnb_ref[...]
    y_ref[0] = (yn + bonus * v) * g

    @pl.when(t == pl.num_programs(1) - 1)
    def _():
        for gi in range(N_GROUPS):
            st_ref[0, gi] = _unbd(st_sc[gi])


def _state_spec():
    return pl.BlockSpec((1, N_GROUPS, HEAD_DIM, GROUP), lambda i, j: (i, 0, 0, 0))


def _rwkv(za, shift0, s0_cat, pr, L, nc):
    b, t, ac = za.shape
    dm = D_MIX
    tb = L * nc
    row = lambda a: a.reshape(1, -1)
    consts = [row(pr["mu"]), row(pr["w0"]), row(pr["a0"]), pr["wl"], row(pr["k_k"]), row(pr["k_a"]),
              row(pr["r_k"]), row(pr["lnx_g"]), row(pr["lnx_b"]), _block_ones(GROUP, HEAD_DIM)]
    return pl.pallas_call(
        functools.partial(_rwkv_kernel, L=L, nc=nc),
        grid=(b, t // tb),
        in_specs=[pl.BlockSpec((1, tb, ac), lambda i, j: (i, j, 0)),
                  pl.BlockSpec((1, 1, ac), lambda i, j: (i, 0, 0)), _state_spec()]
                 + [_const_spec(c.shape) for c in consts],
        out_specs=[pl.BlockSpec((1, tb, dm), lambda i, j: (i, j, 0)), _state_spec()],
        out_shape=[jax.ShapeDtypeStruct((b, t, dm), F32),
                   jax.ShapeDtypeStruct((b, N_GROUPS, HEAD_DIM, GROUP), F32)],
        scratch_shapes=[pltpu.VMEM((1, ac), F32), pltpu.VMEM((N_GROUPS, GROUP, GROUP), F32)],
        compiler_params=_params(("parallel", "arbitrary")),
        name="rwkv7",
    )(za, shift0.reshape(b, 1, ac), s0_cat, *consts)


def _hgrn_kernel(zq_ref, zf_ref, zi_ref, zg_ref, s0_ref, lb_ref, ng_ref, bo_ref, y_ref, st_ref, st_sc, *, L, nc):
    t = pl.program_id(1)
    tb = nc * L

    @pl.when(t == 0)
    def _():
        for gi in range(N_GROUPS):
            st_sc[gi] = _bd(s0_ref[0, gi], HEAD_DIM, HEAD_DIM)

    dm = D_MIX
    q = _silu(zq_ref[0])
    xf = zf_ref[0]
    v = zi_ref[0]
    lb = lb_ref[...]
    la = jnp.log(lb)
    lc = jnp.log1p(-lb) + _log_sigmoid(xf)
    logf = jnp.maximum(la, lc) + jnp.log1p(jnp.exp(-jnp.abs(la - lc)))
    k = (1.0 - lb) * _sigmoid(-xf)
    bo = bo_ref[...]

    ri = _iota((tb, tb), 0)
    ci = _iota((tb, tb), 1)
    tri = ((ci <= ri) & (ci // L == ri // L)).astype(BF16)
    b = _mm((tri,), _sp(logf, 3))
    b_last = jnp.concatenate(
        [jnp.broadcast_to(b[(j + 1) * L - 1:(j + 1) * L, :], (L, dm)) for j in range(nc)], axis=0)
    qe = q * jnp.exp(b)
    k_end = k * jnp.exp(b_last - b)
    g_last = jnp.exp(b_last)

    n_sub = L // SUB
    nst = N_HEADS * SUB
    stackmask = (_iota((nst, dm), 0) // SUB) == (_iota((nst, dm), 1) // HEAD_DIM)
    sub_rows = _iota((SUB, 1), 0)
    subs = [(cj * L, cj * L + i * SUB, i) for cj in range(nc) for i in range(n_sub)]
    rows_of = {sb: slice(sb[1], sb[1] + SUB) for sb in subs}
    brefs = {sb: jnp.zeros((1, dm), F32) if sb[2] == 0 else b[sb[1] - 1:sb[1]] for sb in subs}

    def fold_heads(ov):
        out = ov[0:SUB]
        for hh in range(1, N_HEADS):
            out = out + ov[hh * SUB:(hh + 1) * SUB]
        return out

    def stacked_q(sb):
        qh = q[rows_of[sb]] * jnp.exp(b[rows_of[sb]] - brefs[sb])
        return bf(jnp.where(stackmask, jnp.concatenate([qh] * N_HEADS, axis=0), 0.0))

    def intra_exact():
        xs = {sb: [] for sb in subs}
        for s in range(SUB):
            for sb in subs:
                bi = b[rows_of[sb]]
                e = jnp.exp(jnp.where(sub_rows >= s, bi - bi[s:s + 1], NEG))
                xs[sb].append(q[rows_of[sb]] * e * k[rows_of[sb]][s:s + 1])
        att = {sb: _head_sum(jnp.concatenate(xs[sb], axis=0), bo, terms=1) for sb in subs}
        later = [sb for sb in subs if sb[2] > 0]
        qst = {sb: stacked_q(sb) for sb in later}
        kh = {sb: bf(k[sb[0]:sb[1]] * jnp.exp(brefs[sb] - b[sb[0]:sb[1]])) for sb in later}
        att2 = {sb: _mm(qst[sb], kh[sb], _NT) for sb in later}
        ov = {sb: jnp.where(stackmask, _mm(bf(att2[sb]), bf(v[sb[0]:sb[1]])), 0.0) for sb in later}
        o_subs = []
        for sb in subs:
            vi = v[rows_of[sb]]
            oi = att[sb][0:SUB] * vi[0:1]
            for s in range(1, SUB):
                oi = oi + att[sb][s * SUB:(s + 1) * SUB] * vi[s:s + 1]
            if sb[2] > 0:
                oi = oi + fold_heads(ov[sb])
            o_subs.append(oi)
        return jnp.concatenate(o_subs, axis=0) if len(o_subs) > 1 else o_subs[0]

    def intra_factored():
        qst = {sb: stacked_q(sb) for sb in subs}
        kh = {sb: bf(k[sb[0]:sb[1] + SUB] * jnp.exp(brefs[sb] - b[sb[0]:sb[1] + SUB])) for sb in subs}
        att2 = {}
        for sb in subs:
            n_keys = sb[1] + SUB - sb[0]
            visible = _iota((nst, n_keys), 1) <= sb[2] * SUB + _iota((nst, n_keys), 0) % SUB
            att2[sb] = jnp.where(visible, _mm(qst[sb], kh[sb], _NT), 0.0)
        ov = {sb: jnp.where(stackmask, _mm(bf(att2[sb]), bf(v[sb[0]:sb[1] + SUB])), 0.0) for sb in subs}
        o_subs = [fold_heads(ov[sb]) for sb in subs]
        return jnp.concatenate(o_subs, axis=0) if len(o_subs) > 1 else o_subs[0]

    drop = jnp.concatenate([b[sb[1] + SUB - 1:sb[1] + SUB] - brefs[sb] for sb in subs], axis=0)
    o_intra = lax.cond(jnp.min(drop) >= -FACTORED_MAX_DECAY, intra_factored, intra_exact)

    groups = range(N_GROUPS)
    gsl = {gi: slice(gi * GROUP, (gi + 1) * GROUP) for gi in groups}
    bdmask = (_iota((GROUP, GROUP), 0) // HEAD_DIM) == (_iota((GROUP, GROUP), 1) // HEAD_DIM)
    upd = {(cj, gi): jnp.where(bdmask, _mm(bf(v[cj * L:(cj + 1) * L, gsl[gi]]),
                                          bf(k_end[cj * L:(cj + 1) * L, gsl[gi]]), _TN), 0.0)
           for cj in range(nc) for gi in groups}
    state = [st_sc[gi] for gi in groups]
    o_rows = []
    for cj in range(nc):
        rs = slice(cj * L, (cj + 1) * L)
        o_rows.append(jnp.concatenate([_mm(bf(qe[rs, gsl[gi]]), bf(state[gi]), _NT) for gi in groups], axis=1))
        state = [state[gi] * g_last[rs, gsl[gi]][0:1] + upd[cj, gi] for gi in groups]
    for gi in groups:
        st_sc[gi] = state[gi]
    o = o_intra + (jnp.concatenate(o_rows, axis=0) if nc > 1 else o_rows[0])

    ms = _head_sum(o * o, bo) * (1.0 / HEAD_DIM)
    y_ref[0] = o * lax.rsqrt(ms + RMS_EPS) * ng_ref[...] * _silu(zg_ref[0])

    @pl.when(t == pl.num_programs(1) - 1)
    def _():
        for gi in range(N_GROUPS):
            st_ref[0, gi] = _unbd(st_sc[gi])


def _hgrn(zq, zf, zi, zg, s0_cat, lb, norm_g, L, nc):
    b, t, dm = zq.shape
    tb = L * nc
    consts = [lb.reshape(1, dm), norm_g.reshape(1, dm), _block_ones(GROUP, HEAD_DIM)]
    row_spec = pl.BlockSpec((1, tb, dm), lambda i, j: (i, j, 0))
    return pl.pallas_call(
        functools.partial(_hgrn_kernel, L=L, nc=nc),
        grid=(b, t // tb),
        in_specs=[row_spec] * 4 + [_state_spec()] + [_const_spec(c.shape) for c in consts],
        out_specs=[row_spec, _state_spec()],
        out_shape=[jax.ShapeDtypeStruct((b, t, dm), F32),
                   jax.ShapeDtypeStruct((b, N_GROUPS, HEAD_DIM, GROUP), F32)],
        scratch_shapes=[pltpu.VMEM((N_GROUPS, GROUP, GROUP), F32)],
        compiler_params=_params(("parallel", "arbitrary")),
        name="hgrn2",
    )(zq, zf, zi, zg, s0_cat, *consts)


def _cumsum_kernel(x_ref, o_ref):
    x = x_ref[0]
    n, w = x.shape
    nblk = n // N_HEADS
    upper = (_iota((w, w), 0) <= _iota((w, w), 1)).astype(F32)
    c = _dot(x, upper, HI)
    tot = jnp.broadcast_to(c[:, w - 1:w], (n, w))
    ri = _iota((n, n), 0)
    ci = _iota((n, n), 1)
    prior = ((ri // nblk == ci // nblk) & (ci < ri)).astype(F32)
    o_ref[0] = c + _dot(prior, tot, HI)


def _cumsum_time(logf_bht):
    b, h, t = logf_bht.shape
    w = 128
    n = h * (t // w)
    x = logf_bht.reshape(b, n, w)
    out = pl.pallas_call(
        _cumsum_kernel,
        grid=(b,),
        in_specs=[pl.BlockSpec((1, n, w), lambda i: (i, 0, 0))],
        out_specs=pl.BlockSpec((1, n, w), lambda i: (i, 0, 0)),
        out_shape=jax.ShapeDtypeStruct((b, n, w), F32),
        compiler_params=_params(("parallel",)),
        name="cumsum",
    )(x)
    return out.reshape(b, h, t)


def _fox_kernel(q_ref, k_ref, v_ref, f_ref, o_ref, kb_sc, vb_sc, *, tq):
    i = pl.program_id(2)

    @pl.when(i == 0)
    def _():
        kb_sc[...] = k_ref[0].astype(BF16)
        vb_sc[...] = v_ref[0].astype(BF16)

    q = q_ref[0] * ATTN_SCALE
    lane = _iota((1, 2 * HEAD_DIM), 1)
    causal = _iota((tq, tq), 1) <= _iota((tq, tq), 0)
    outs = []
    for hh in range(2):
        head_lane = (lane // HEAD_DIM) == hh
        qm = jnp.where(head_lane, q, 0.0).astype(BF16)

        def scores(j, hh=hh, qm=qm):
            return _dot_nt(qm, kb_sc[pl.ds(j * tq, tq), :]) - f_ref[0, 0, j, hh:hh + 1, :]

        def update(j, s, m, acc, head_lane=head_lane):
            m_new = jnp.maximum(m, jnp.max(s, axis=-1, keepdims=True))
            p = jnp.exp(s - m_new).astype(BF16)
            vx = jnp.where(head_lane, vb_sc[pl.ds(j * tq, tq), :], 1.0)
            return m_new, jnp.exp(m - m_new) * acc + _dot(p, vx)

        def run_blocks(first, count, diagonal_last, carry):
            ss = [scores(first + n) for n in range(count)]
            if diagonal_last:
                ss[-1] = jnp.where(causal, ss[-1], NEG)
            for n, s in enumerate(ss):
                carry = update(first + n, s, *carry)
            return carry

        init = (jnp.full((tq, 1), NEG, F32), jnp.zeros((tq, 2 * HEAD_DIM), F32))
        n_full = i // FOX_GROUP
        carry = lax.fori_loop(0, n_full, lambda g, cr: run_blocks(g * FOX_GROUP, FOX_GROUP, False, cr), init)
        tails = [functools.partial(run_blocks, n_full * FOX_GROUP, r + 1, True) for r in range(FOX_GROUP)]
        _, acc = lax.switch(i % FOX_GROUP, tails, carry)
        outs.append(acc / pltpu.roll(acc, HEAD_DIM, 1))
    o_ref[0] = jnp.where((lane // HEAD_DIM) == 0, outs[0], outs[1])


def _fox_prompt(q, k, v, cum_bht, tq):
    b, t, dm = q.shape
    hp = dm // (2 * HEAD_DIM)
    nk = t // tq
    f = cum_bht.reshape(b, hp, 2, nk, tq).transpose(0, 1, 3, 2, 4)
    kv_spec = pl.BlockSpec((1, t, 2 * HEAD_DIM), lambda i, h, j: (i, 0, h))
    return pl.pallas_call(
        functools.partial(_fox_kernel, tq=tq),
        scratch_shapes=[pltpu.VMEM((t, 2 * HEAD_DIM), BF16), pltpu.VMEM((t, 2 * HEAD_DIM), BF16)],
        grid=(b, hp, nk),
        in_specs=[pl.BlockSpec((1, tq, 2 * HEAD_DIM), lambda i, h, j: (i, j, h)), kv_spec, kv_spec,
                  pl.BlockSpec((1, 1, nk, 2, tq), lambda i, h, j: (i, h, 0, 0, 0))],
        out_specs=pl.BlockSpec((1, tq, 2 * HEAD_DIM), lambda i, h, j: (i, j, h)),
        out_shape=jax.ShapeDtypeStruct((b, t, dm), F32),
        compiler_params=_params(("parallel", "parallel", "arbitrary")),
        name="fox_prompt",
    )(q, k, v, f)


def _band_kernel(q_ref, k_ref, v_ref, bias_ref, o_ref, *, tq):
    i = pl.program_id(2)
    q = q_ref[0] * ATTN_SCALE
    lane = _iota((1, 2 * HEAD_DIM), 1)
    n_piece = BAND // tq + 1
    outs = []
    for hh in range(2):
        qm = jnp.where((lane // HEAD_DIM) == hh, q, 0.0).astype(BF16)
        ss, vs = [], []
        for p in range(n_piece):
            blk = i - (n_piece - 1) + p
            start = jnp.maximum(blk, 0) * tq
            kb = k_ref[0, pl.ds(start, tq), :].astype(BF16)
            s = _dot_nt(qm, kb) + bias_ref[0, hh, :, p * tq:(p + 1) * tq]
            ss.append(jnp.where(blk >= 0, s, NEG))
            vs.append(v_ref[0, pl.ds(start, tq), :].astype(BF16))
        m = jnp.max(ss[0], axis=-1, keepdims=True)
        for s in ss[1:]:
            m = jnp.maximum(m, jnp.max(s, axis=-1, keepdims=True))
        l = jnp.zeros((tq, 1), F32)
        acc = jnp.zeros((tq, 2 * HEAD_DIM), F32)
        for s, vb in zip(ss, vs):
            p_ = jnp.exp(s - m)
            l = l + jnp.sum(p_, axis=-1, keepdims=True)
            acc = acc + _dot(p_.astype(BF16), vb)
        outs.append(acc / l)
    o_ref[0] = jnp.where((lane // HEAD_DIM) == 0, outs[0], outs[1])


def _toeplitz_bias(rel_bias, nq, nk, offset):
    h = rel_bias.shape[0]
    m = jnp.arange(nq + nk - 1)
    u = rel_bias[:, jnp.clip(offset + nq - 1 - m, -REL_CLIP, REL_CLIP) + REL_CLIP].astype(F32)
    period = nq + nk
    up = jnp.pad(u, ((0, 0), (0, 1)))
    skew = jnp.tile(up, (1, nq))[:, :nq * (period - 1)].reshape(h, nq, period - 1)
    return skew[:, :, nq - 1:]


def _band_bias_prompt(rel_bias, tq):
    nk = BAND + tq
    bias = _toeplitz_bias(rel_bias, tq, nk, BAND)
    qc = jnp.arange(tq)[:, None] // CHUNK
    kc = jnp.arange(nk)[None, :] // CHUNK
    valid = (kc >= qc) & (kc <= qc + BAND // CHUNK)
    return jnp.where(valid[None], bias, NEG)


def _band_prompt(q, k, v, rel_bias, tq):
    b, t, dm = q.shape
    hp = dm // (2 * HEAD_DIM)
    bias = _band_bias_prompt(rel_bias, tq).reshape(hp, 2, tq, BAND + tq)
    kv_spec = pl.BlockSpec((1, t, 2 * HEAD_DIM), lambda i, h, j: (i, 0, h))
    return pl.pallas_call(
        functools.partial(_band_kernel, tq=tq),
        grid=(b, hp, t // tq),
        in_specs=[pl.BlockSpec((1, tq, 2 * HEAD_DIM), lambda i, h, j: (i, j, h)), kv_spec, kv_spec,
                  pl.BlockSpec((1, 2, tq, BAND + tq), lambda i, h, j: (h, 0, 0, 0))],
        out_specs=pl.BlockSpec((1, tq, 2 * HEAD_DIM), lambda i, h, j: (i, j, h)),
        out_shape=jax.ShapeDtypeStruct((b, t, dm), F32),
        compiler_params=_params(("parallel", "parallel", "arbitrary")),
        name="band_prompt",
    )(q, k, v, bias)


def _cached_attn_kernel(*refs, fox):
    if fox:
        q_ref, kn_ref, vn_ref, ck_ref, cv_ref, fc_ref, fn_ref, o_ref = refs
    else:
        q_ref, kn_ref, vn_ref, ck_ref, cv_ref, bc_ref, bn_ref, o_ref = refs
    tn, dm = q_ref.shape[1], q_ref.shape[2]
    nst = N_HEADS * tn
    stackmask = (_iota((nst, dm), 0) // tn) == (_iota((nst, dm), 1) // HEAD_DIM)
    q = q_ref[0] * ATTN_SCALE
    qst = jnp.where(stackmask, jnp.concatenate([q] * N_HEADS, axis=0), 0.0).astype(BF16)
    s_c = _dot_nt(qst, ck_ref[0].astype(BF16))
    s_n = _dot_nt(qst, kn_ref[0].astype(BF16))
    if fox:
        expand = ((_iota((nst, N_HEADS), 0) // tn) == _iota((nst, N_HEADS), 1)).astype(F32)
        fc = fc_ref[0]
        upper = (_iota((tn, tn), 0) <= _iota((tn, tn), 1)).astype(F32)
        fnew = fc[:, fc.shape[1] - 1:] + _dot(fn_ref[0], upper, HI)
        s_c = s_c - _dot(expand, fc, HI)
        s_n = s_n - _dot(expand, fnew, HI)
        tq = _iota((nst, tn), 0) % tn
        s_n = jnp.where(_iota((nst, tn), 1) <= tq, s_n, NEG)
    else:
        s_c = s_c + bc_ref[...]
        s_n = s_n + bn_ref[...]
    m = jnp.maximum(jnp.max(s_c, axis=-1, keepdims=True), jnp.max(s_n, axis=-1, keepdims=True))
    p_c = jnp.exp(s_c - m)
    p_n = jnp.exp(s_n - m)
    l = jnp.sum(p_c, axis=-1, keepdims=True) + jnp.sum(p_n, axis=-1, keepdims=True)
    ov = _dot(p_c.astype(BF16), cv_ref[0].astype(BF16)) + _dot(p_n.astype(BF16), vn_ref[0].astype(BF16))
    ov = jnp.where(stackmask, ov / l, 0.0)
    out = ov[0:tn]
    for hh in range(1, N_HEADS):
        out = out + ov[hh * tn:(hh + 1) * tn]
    o_ref[0] = out


def _cached_attn(q, kn, vn, ck, cv, extra_c, extra_n, fox):
    b, tn, dm = q.shape
    p = ck.shape[1]
    new_spec = pl.BlockSpec((1, tn, dm), lambda i: (i, 0, 0))
    cache_spec = pl.BlockSpec((1, p, dm), lambda i: (i, 0, 0))
    if fox:
        ex_specs = [pl.BlockSpec((1, N_HEADS, p), lambda i: (i, 0, 0)),
                    pl.BlockSpec((1, N_HEADS, tn), lambda i: (i, 0, 0))]
    else:
        ex_specs = [_const_spec(extra_c.shape), _const_spec(extra_n.shape)]
    return pl.pallas_call(
        functools.partial(_cached_attn_kernel, fox=fox),
        grid=(b,),
        in_specs=[new_spec, new_spec, new_spec, cache_spec, cache_spec] + ex_specs,
        out_specs=new_spec,
        out_shape=jax.ShapeDtypeStruct((b, tn, dm), F32),
        compiler_params=_params(("parallel",)),
        name="fox_sample" if fox else "band_sample",
    )(q, kn, vn, ck, cv, extra_c, extra_n)


def _to_cat(s):
    b = s.shape[0]
    s5 = s.reshape(b, N_GROUPS, HEADS_PER_GROUP, HEAD_DIM, HEAD_DIM)
    return s5.transpose(0, 1, 3, 2, 4).reshape(b, N_GROUPS, HEAD_DIM, GROUP)


def _from_cat(c):
    b = c.shape[0]
    c5 = c.reshape(b, N_GROUPS, HEAD_DIM, HEADS_PER_GROUP, HEAD_DIM)
    return c5.transpose(0, 1, 3, 2, 4).reshape(b, N_HEADS, HEAD_DIM, HEAD_DIM)


def _trunk(x, mods, P, cache, tm, tq, L, nc):
    b, t, d = x.shape
    dm = D_MIX
    sample = cache is not None
    new = {}
    za, q, k, v, logf = _proj(x, mods[0], P["norm_mix_g"][0], P["ab_w"], (P["a_cols"], dm, dm, dm), tm,
                              wf=P["ab_wf"], bf=P["fox_bf"])
    ac = P["a_cols"]
    if sample:
        nb, tn = cache["nb"], cache["tn"]
        za_b = za.reshape(nb, tn, ac)
        shift0 = cache["rwkv_shift"][0]
        s0 = _to_cat(cache["rwkv"][0])
    else:
        nb, tn = b, t
        za_b = za
        shift0 = jnp.zeros((nb, ac), F32)
        s0 = jnp.zeros((nb, N_GROUPS, HEAD_DIM, GROUP), F32)
    ya, st = _rwkv(za_b, shift0, s0, P["rwkv"], L, nc)
    new["rwkv"] = _from_cat(st)[None]
    new["rwkv_shift"] = za_b[:, -1][None]
    qb, kb, vb = (a.reshape(nb, tn, dm) for a in (q, k, v))
    logf_b = logf.reshape(nb, tn, N_HEADS)
    logf_t = jnp.swapaxes(logf_b, 1, 2)
    if sample:
        ck = cache["fox_k"][0].reshape(nb, -1, dm)
        cv = cache["fox_v"][0].reshape(nb, -1, dm)
        fc = _cumsum_time(jnp.swapaxes(cache["fox_logf"][0], 1, 2))
        yb = _cached_attn(qb, kb, vb, ck, cv, fc, logf_t, fox=True)
    else:
        yb = _fox_prompt(qb, kb, vb, _cumsum_time(logf_t), tq)
    new["fox_k"] = kb.reshape(1, nb, tn, N_HEADS, HEAD_DIM)
    new["fox_v"] = vb.reshape(1, nb, tn, N_HEADS, HEAD_DIM)
    new["fox_logf"] = logf_b[None]
    fix = None
    if sample:
        buf = cache["ffn_conv"][0]
        zero = jnp.zeros((nb, tn - 2, buf.shape[-1]), F32)
        fix = (jnp.concatenate([buf[:, 1:2], buf[:, 0:1] * 0, zero], axis=1).reshape(1, nb * tn, -1),
               jnp.concatenate([buf, zero], axis=1).reshape(1, nb * tn, -1))
    res = _tail(x, ya.reshape(b, t, dm), yb.reshape(b, t, dm), mods[0], mods[1], P["ab_wo"],
                P["norm_ffn_g"][0], P["wup"][0], P["conv_w"][0], P["conv_b"][0], P["wdn"][0],
                P["final_g"], False, tm, fix=fix, period=tn)
    x = res[0]
    conv0 = res[1].reshape(nb, tn, -1)[:, tn - 2:] if sample else res[1]
    q, k, v, zq, zf, zi, zg = _proj(x, mods[2], P["norm_mix_g"][1], P["cd_w"], (dm,) * 7, tm)
    qb, kb, vb, zq, zf, zi, zg = (a.reshape(nb, tn, dm) for a in (q, k, v, zq, zf, zi, zg))
    if sample:
        ck = cache["chunk_k"][0].reshape(nb, -1, dm)
        cv = cache["chunk_v"][0].reshape(nb, -1, dm)
        yc = _cached_attn(qb, kb, vb, ck, cv, P["band_bias_c"], P["band_bias_n"], fox=False)
        new["chunk_k"] = kb.reshape(1, nb, tn, N_HEADS, HEAD_DIM)
        new["chunk_v"] = vb.reshape(1, nb, tn, N_HEADS, HEAD_DIM)
        s0 = _to_cat(jnp.swapaxes(cache["hgrn"][0], -1, -2))
    else:
        yc = _band_prompt(qb, kb, vb, P["rel_bias"], 4 * CHUNK)
        keep = min(BAND, tn)
        new["chunk_k"] = kb[:, tn - keep:].reshape(1, nb, keep, N_HEADS, HEAD_DIM)
        new["chunk_v"] = vb[:, tn - keep:].reshape(1, nb, keep, N_HEADS, HEAD_DIM)
        s0 = jnp.zeros((nb, N_GROUPS, HEAD_DIM, GROUP), F32)
    yd, st = _hgrn(zq, zf, zi, zg, s0, P["hgrn_lb"], P["hgrn_norm_g"], L, nc)
    new["hgrn"] = jnp.swapaxes(_from_cat(st), -1, -2)[None]
    if sample:
        buf = cache["ffn_conv"][1]
        fix = (jnp.concatenate([buf[:, 1:2], buf[:, 0:1] * 0, zero], axis=1).reshape(1, nb * tn, -1),
               jnp.concatenate([buf, zero], axis=1).reshape(1, nb * tn, -1))
    res = _tail(x, yc.reshape(b, t, dm), yd.reshape(b, t, dm), mods[2], mods[3], P["cd_wo"],
                P["norm_ffn_g"][1], P["wup"][1], P["conv_w"][1], P["conv_b"][1], P["wdn"][1],
                P["final_g"], True, tm, fix=fix, period=tn)
    conv1 = res[1].reshape(nb, tn, -1)[:, tn - 2:] if sample else res[1]
    new["ffn_conv"] = jnp.stack([conv0, conv1])
    y = res[0].reshape(nb, tn, d)
    return y, new


def kernel(x_prompt, x_sample, c_prompt, c_sample, cache_fox_k, cache_fox_v, cache_fox_logf, state_rwkv,
           state_rwkv_shift, cache_chunk_k, cache_chunk_v, state_hgrn, state_ffn_conv, ada_w, ada_b,
           norm_mix_g, norm_ffn_g, ab_w_in, rwkv_mu, rwkv_w0, rwkv_w2, rwkv_a0, rwkv_a2, rwkv_g2,
           rwkv_k_k, rwkv_k_a, rwkv_r_k, rwkv_lnx_g, rwkv_lnx_b, fox_b_f, ab_w_out, cd_w_in,
           chunk_rel_bias, hgrn_lb_table, hgrn_norm_g, cd_w_out, ffn_w_up, ffn_conv_w, ffn_conv_b,
           ffn_w_down, final_norm_g):
    bp, tp, d = x_prompt.shape
    bs, ts, _ = x_sample.shape
    depth = ada_w.shape[0]
    dm = D_MIX
    a_cols = rwkv_mu.shape[1]
    n_lw, n_la, n_lg = rwkv_w2.shape[1], rwkv_a2.shape[1], rwkv_g2.shape[1]

    c_all = jnp.concatenate([c_prompt, c_sample], axis=0)
    mods = _adaln(c_all, ada_w.reshape(depth * 2, d, 3 * d), ada_b.reshape(depth * 2, 3 * d))
    mods_p = [mods[i, :bp].reshape(bp, 1, 3 * d) for i in range(depth * 2)]
    mods_s = [jnp.repeat(mods[i, bp:], ts, axis=0).reshape(1, bs * ts, 3 * d) for i in range(depth * 2)]

    wl = jnp.zeros((n_lw + n_la + n_lg, 3 * dm), F32)
    wl = wl.at[:n_lw, 0:dm].set(rwkv_w2[0])
    wl = wl.at[n_lw:n_lw + n_la, dm:2 * dm].set(rwkv_a2[0])
    wl = wl.at[n_lw + n_la:, 2 * dm:].set(rwkv_g2[0])
    sm = jax.nn.softmax(hgrn_lb_table.astype(F32), axis=0)
    lb = (jnp.cumsum(sm, axis=0) - sm[0])[1]
    wf = jnp.zeros((d, 128), F32).at[:, :N_HEADS].set(ab_w_in[0][:, a_cols + 3 * dm:]).astype(BF16)
    bf = jnp.zeros((1, 128), F32).at[0, :N_HEADS].set(fox_b_f[0])
    p_c = cache_chunk_k.shape[2]
    bias_s = _toeplitz_bias(chunk_rel_bias[0], ts, p_c + ts, p_c).reshape(N_HEADS * ts, p_c + ts)

    P = {
        "a_cols": a_cols,
        "norm_mix_g": norm_mix_g, "norm_ffn_g": norm_ffn_g, "final_g": final_norm_g,
        "ab_w": ab_w_in[0][:, :a_cols + 3 * dm].astype(BF16), "ab_wf": wf, "fox_bf": bf,
        "ab_wo": ab_w_out[0].astype(BF16),
        "cd_w": cd_w_in[0].astype(BF16), "cd_wo": cd_w_out[0].astype(BF16),
        "wup": [ffn_w_up[i].astype(BF16) for i in range(depth)],
        "wdn": [ffn_w_down[i].astype(BF16) for i in range(depth)],
        "conv_w": ffn_conv_w, "conv_b": ffn_conv_b,
        "rwkv": {"mu": rwkv_mu[0], "w0": rwkv_w0[0], "a0": rwkv_a0[0], "wl": wl.astype(BF16), "k_k": rwkv_k_k[0],
                 "k_a": rwkv_k_a[0], "r_k": rwkv_r_k[0], "lnx_g": rwkv_lnx_g[0], "lnx_b": rwkv_lnx_b[0]},
        "rel_bias": chunk_rel_bias[0], "band_bias_c": bias_s[:, :p_c], "band_bias_n": bias_s[:, p_c:],
        "hgrn_lb": lb, "hgrn_norm_g": hgrn_norm_g[0],
    }
    cache = {"nb": bs, "tn": ts, "fox_k": cache_fox_k, "fox_v": cache_fox_v, "fox_logf": cache_fox_logf,
             "rwkv": state_rwkv, "rwkv_shift": state_rwkv_shift, "chunk_k": cache_chunk_k,
             "chunk_v": cache_chunk_v, "hgrn": state_hgrn, "ffn_conv": state_ffn_conv}

    y_p, sp = _trunk(x_prompt, mods_p, P, None, tm=512, tq=512, L=CHUNK, nc=4)
    y_s, ss = _trunk(x_sample.reshape(1, bs * ts, d), mods_s, P, cache, tm=bs * ts, tq=None, L=ts, nc=1)
    names = ("fox_k", "fox_v", "fox_logf", "rwkv", "rwkv_shift", "chunk_k", "chunk_v", "hgrn", "ffn_conv")
    return (y_p, y_s) + tuple(sp[n] for n in names) + tuple(ss[n] for n in names)
```

```python
import functools
import math

import jax
import jax.numpy as jnp
from jax import lax
from jax.experimental import pallas as pl
from jax.experimental.pallas import tpu as pltpu

F32 = jnp.float32
BF16 = jnp.bfloat16
HI = lax.Precision.HIGHEST

HEAD_DIM = 64
N_HEADS = 8
D_MIX = N_HEADS * HEAD_DIM
GROUP = 256
N_GROUPS = D_MIX // GROUP
HEADS_PER_GROUP = GROUP // HEAD_DIM
SUB = 16
CHUNK = 64
BAND = 512
REL_CLIP = 128
RMS_EPS = 1e-6
GN_EPS = 64e-5
ATTN_SCALE = HEAD_DIM ** -0.5
NEG = -1e30
FACTORED_MAX_DECAY = 60.0
FOX_GROUP = 4
UP_LOOKAHEAD = 4
VMEM_LIMIT = 56 * 1024 * 1024


def _dot(a, b, prec=None):
    return jnp.dot(a, b, preferred_element_type=F32, precision=prec)


def _dot_nt(a, b, prec=None):
    return lax.dot_general(a, b, (((1,), (1,)), ((), ())), preferred_element_type=F32, precision=prec)


def _dot_tn(a, b, prec=None):
    return lax.dot_general(a, b, (((0,), (0,)), ((), ())), preferred_element_type=F32, precision=prec)


_NN = ((1,), (0,))
_NT = ((1,), (1,))
_TN = ((0,), (0,))


def _sp(x, n):
    hi = x.astype(BF16)
    if n == 1:
        return (hi,)
    r = x - hi.astype(F32)
    mid = r.astype(BF16)
    if n == 2:
        return (hi, mid)
    return (hi, mid, (r - mid.astype(F32)).astype(BF16))


def bf(x):
    return (x.astype(BF16),)


def _mm(a, b, dims=_NN):
    n = max(len(a), len(b))
    out = None
    for i, ai in enumerate(a):
        for j, bj in enumerate(b):
            if i + j < n:
                d = lax.dot_general(ai, bj, (dims, ((), ())), preferred_element_type=F32)
                out = d if out is None else out + d
    return out


def _head_sum(x, bo_group, terms=2):
    return jnp.concatenate(
        [_mm(_sp(x[:, gi * GROUP:(gi + 1) * GROUP], terms), (bo_group,)) for gi in range(N_GROUPS)], axis=1)


def _sigmoid(x):
    return jax.nn.sigmoid(x)


def _silu(x):
    return x * jax.nn.sigmoid(x)


def _softplus(x):
    return jnp.maximum(x, 0.0) + jnp.log1p(jnp.exp(-jnp.abs(x)))


def _log_sigmoid(x):
    return -_softplus(-x)


def _iota(shape, dim):
    return lax.broadcasted_iota(jnp.int32, shape, dim)


def _const_spec(shape):
    nd = len(shape)
    return pl.BlockSpec(shape, lambda *_: (0,) * nd, pipeline_mode=pl.Buffered(1))


def _params(sem):
    return pltpu.CompilerParams(dimension_semantics=sem, vmem_limit_bytes=VMEM_LIMIT)


def _norm_mod(x, g, shift, scale):
    xn = x * lax.rsqrt(jnp.mean(x * x, axis=-1, keepdims=True) + RMS_EPS) * g
    return xn * (1.0 + scale) + shift


def _block_ones(n, blk):
    i = jnp.arange(n) // blk
    return (i[:, None] == i[None, :]).astype(BF16)


def _bd(x, rows_per_head, cols_per_head):
    n = HEADS_PER_GROUP
    t = jnp.concatenate([x] * n, axis=0)
    r = _iota(t.shape, 0) // rows_per_head
    c = _iota(t.shape, 1) // cols_per_head
    return jnp.where(r == c, t, 0.0)


def _unbd(m):
    out = m[0:HEAD_DIM]
    for h in range(1, HEADS_PER_GROUP):
        out = out + m[h * HEAD_DIM:(h + 1) * HEAD_DIM]
    return out


def _adaln_kernel(c_ref, w_ref, b_ref, o_ref):
    o_ref[0] = _dot(_silu(c_ref[...]), w_ref[0], HI) + b_ref[0]


def _adaln(c_all, ada_w, ada_b):
    n, d, d3 = ada_w.shape
    bt = c_all.shape[0]
    nt = d3 // d
    return pl.pallas_call(
        _adaln_kernel,
        grid=(n, nt),
        in_specs=[pl.BlockSpec((bt, d), lambda i, j: (0, 0)),
                  pl.BlockSpec((1, d, d), lambda i, j: (i, 0, j)),
                  pl.BlockSpec((1, 1, d), lambda i, j: (i, 0, j))],
        out_specs=pl.BlockSpec((1, bt, d), lambda i, j: (i, 0, j)),
        out_shape=jax.ShapeDtypeStruct((n, bt, d3), F32),
        compiler_params=_params(("parallel", "parallel")),
        name="adaln",
    )(c_all, ada_w, ada_b.reshape(n, 1, d3))


def _proj_kernel(*refs, splits, fox):
    x_ref, mod_ref, g_ref, w_ref = refs[:4]
    pos = 4
    if fox:
        wf_ref, bf_ref = refs[4:6]
        pos = 6
    outs = refs[pos:]
    d = x_ref.shape[-1]
    mod = mod_ref[0]
    h = _norm_mod(x_ref[0], g_ref[...], mod[:, :d], mod[:, d:2 * d]).astype(BF16)
    off = 0
    for o_ref, n in zip(outs, splits):
        o_ref[0] = _dot(h, w_ref[:, off:off + n])
        off += n
    if fox:
        zf = _dot(h, wf_ref[...]) + bf_ref[...]
        outs[len(splits)][0] = _log_sigmoid(zf)[:, :N_HEADS]


def _proj(x, mod, g, w_bf, splits, tm, wf=None, bf=None):
    b, t, d = x.shape
    r = mod.shape[1]
    fox = wf is not None
    mod_spec = (pl.BlockSpec((1, 1, 3 * d), lambda i, j: (i, 0, 0)) if r == 1
                else pl.BlockSpec((1, tm, 3 * d), lambda i, j: (i, j, 0)))
    in_specs = [pl.BlockSpec((1, tm, d), lambda i, j: (i, j, 0)), mod_spec,
                _const_spec((1, d)), _const_spec(w_bf.shape)]
    args = [x, mod, g.reshape(1, d), w_bf]
    widths = list(splits)
    if fox:
        in_specs += [_const_spec(wf.shape), _const_spec(bf.shape)]
        args += [wf, bf]
        widths.append(N_HEADS)
    return pl.pallas_call(
        functools.partial(_proj_kernel, splits=tuple(splits), fox=fox),
        grid=(b, t // tm),
        in_specs=in_specs,
        out_specs=[pl.BlockSpec((1, tm, n), lambda i, j: (i, j, 0)) for n in widths],
        out_shape=[jax.ShapeDtypeStruct((b, t, n), F32) for n in widths],
        compiler_params=_params(("parallel", "parallel")),
        name="proj",
    )(*args)


def _tail_kernel(*refs, sample, final, period, cb, down_group):
    (x_ref, ya_ref, yb_ref, m1_ref, m2_ref, wo_ref, g_ref, wup_ref, cw_ref, cbias_ref,
     wdn_ref, fg_ref) = refs[:12]
    if sample:
        fix1_ref, fix2_ref, o_ref, u_ref, g_sc = refs[12:]
    else:
        o_ref, cs_ref, g_sc, tail_sc = refs[12:]
    tm, d = x_ref.shape[1], x_ref.shape[2]
    ff = wdn_ref.shape[0]
    dm = ya_ref.shape[2]
    m1 = m1_ref[0]
    m2 = m2_ref[0]
    ymix = (_dot(ya_ref[0].astype(BF16), wo_ref[0:dm, :])
            + _dot(yb_ref[0].astype(BF16), wo_ref[dm:2 * dm, :]))
    x1 = x_ref[0] + m1[:, 2 * d:3 * d] * ymix
    h = _norm_mod(x1, g_ref[...], m2[:, :d], m2[:, d:2 * d]).astype(BF16)
    rows = _iota((tm, 1), 0)
    if sample:
        tpos = rows & (period - 1)
    else:
        t = pl.program_id(1)

        @pl.when(t == 0)
        def _():
            tail_sc[...] = jnp.zeros_like(tail_sc)

    def up(j):
        return [_dot(h, wup_ref[:, half * ff + j * cb:half * ff + (j + 1) * cb]) for half in range(2)]

    def conv(u, u1, u2, c0):
        cw = cw_ref[:, c0:c0 + cb]
        return cbias_ref[:, c0:c0 + cb] + cw[0:1] * u2 + cw[1:2] * u1 + cw[2:3] * u

    def gated(j, us):
        halves = []
        for half, u in enumerate(us):
            c0 = half * ff + j * cb
            if sample:
                u1 = jnp.where(tpos == 0, fix1_ref[0, :, c0:c0 + cb], pltpu.roll(u, 1, 0))
                u2 = jnp.where(tpos < 2, fix2_ref[0, :, c0:c0 + cb], pltpu.roll(u, 2, 0))
                u_ref[0, :, c0:c0 + cb] = u
                halves.append(conv(u, u1, u2, c0))
            else:
                edge = jnp.concatenate([tail_sc[:, c0:c0 + cb], u[0:8]], axis=0)
                first = conv(edge, pltpu.roll(edge, 1, 0), pltpu.roll(edge, 2, 0), c0)[8:16]
                tail_sc[:, c0:c0 + cb] = u[tm - 8:tm, :]
                main = conv(u, pltpu.roll(u, 1, 0), pltpu.roll(u, 2, 0), c0)
                halves.append(jnp.concatenate([first, main[8:]], axis=0))
        return (_silu(halves[0]) * halves[1]).astype(BF16)

    acc = None
    n_blk = ff // cb
    pending = [up(j) for j in range(min(UP_LOOKAHEAD, n_blk))]
    j0 = 0
    for j in range(n_blk):
        if j + UP_LOOKAHEAD < n_blk:
            pending.append(up(j + UP_LOOKAHEAD))
        g_sc[:, j * cb:(j + 1) * cb] = gated(j, pending.pop(0))
        if (j + 1 - j0) == down_group or j + 1 == n_blk:
            part = _dot(g_sc[:, j0 * cb:(j + 1) * cb], wdn_ref[j0 * cb:(j + 1) * cb, :])
            acc = part if acc is None else acc + part
            j0 = j + 1
    out = x1 + m2[:, 2 * d:3 * d] * acc
    if final:
        out = out * lax.rsqrt(jnp.mean(out * out, axis=-1, keepdims=True) + RMS_EPS) * fg_ref[...]
    o_ref[0] = out
    if not sample:
        @pl.when(t == pl.num_programs(1) - 1)
        def _():
            cs_ref[0] = tail_sc[6:8, :]


def _tail(x, ya, yb, mod1, mod2, wo_bf, g, wup_bf, conv_w, conv_b, wdn_bf, fg, final, tm,
          fix=None, period=None):
    b, t, d = x.shape
    ff = wdn_bf.shape[0]
    dm = ya.shape[2]
    sample = fix is not None
    r = mod1.shape[1]
    mod_spec = (pl.BlockSpec((1, 1, 3 * d), lambda i, j: (i, 0, 0)) if r == 1
                else pl.BlockSpec((1, tm, 3 * d), lambda i, j: (i, j, 0)))
    row_spec = lambda n: pl.BlockSpec((1, tm, n), lambda i, j: (i, j, 0))
    in_specs = [row_spec(d), row_spec(dm), row_spec(dm), mod_spec, mod_spec,
                _const_spec(wo_bf.shape), _const_spec((1, d)), _const_spec(wup_bf.shape),
                _const_spec(conv_w.shape), _const_spec((1, 2 * ff)), _const_spec(wdn_bf.shape),
                _const_spec((1, d))]
    args = [x, ya, yb, mod1, mod2, wo_bf, g.reshape(1, d), wup_bf, conv_w, conv_b.reshape(1, 2 * ff),
            wdn_bf, fg.reshape(1, d)]
    if sample:
        in_specs += [row_spec(2 * ff), row_spec(2 * ff)]
        args += list(fix)
        out_specs = [row_spec(d), row_spec(2 * ff)]
        out_shape = [jax.ShapeDtypeStruct((b, t, d), F32), jax.ShapeDtypeStruct((b, t, 2 * ff), F32)]
        scratch = [pltpu.VMEM((tm, ff), BF16)]
    else:
        out_specs = [row_spec(d), pl.BlockSpec((1, 2, 2 * ff), lambda i, j: (i, 0, 0))]
        out_shape = [jax.ShapeDtypeStruct((b, t, d), F32), jax.ShapeDtypeStruct((b, 2, 2 * ff), F32)]
        scratch = [pltpu.VMEM((tm, ff), BF16), pltpu.VMEM((8, 2 * ff), F32)]
    return pl.pallas_call(
        functools.partial(_tail_kernel, sample=sample, final=final, period=period, cb=256, down_group=4),
        grid=(b, t // tm),
        in_specs=in_specs,
        out_specs=out_specs,
        out_shape=out_shape,
        scratch_shapes=scratch,
        compiler_params=_params(("parallel", "arbitrary")),
        name="tail",
    )(*args)


def _rwkv_kernel(z_ref, shift0_ref, s0_ref, mu_ref, w0_ref, a0_ref, wl_ref, kk_ref, ka_ref, rk_ref,
                 lng_ref, lnb_ref, bo_ref, y_ref, st_ref, prev_sc, st_sc, *, L, nc):
    t = pl.program_id(1)
    tb = nc * L

    @pl.when(t == 0)
    def _():
        prev_sc[...] = shift0_ref[0]
        for gi in range(N_GROUPS):
            st_sc[gi] = _bd(s0_ref[0, gi], HEAD_DIM, HEAD_DIM)

    z = z_ref[0]
    rows = _iota((tb, 1), 0)
    z_prev = jnp.where(rows == 0, prev_sc[...], pltpu.roll(z, 1, 0))
    prev_sc[...] = z[tb - 1:tb, :]
    zs = z + mu_ref[...] * (z_prev - z)
    dm = D_MIX
    r = zs[:, 0:dm]
    k = zs[:, dm:2 * dm]
    v = zs[:, 2 * dm:3 * dm]
    lr = zs[:, 3 * dm:]
    nl = lr.shape[1]
    lane = _iota((1, nl), 1)
    act = jnp.where(lane < 64, jnp.tanh(lr), jnp.where(lane < 128, lr, _sigmoid(lr)))
    lo = _mm(bf(act), (wl_ref[...],))
    w = -_softplus(-(w0_ref[...] + lo[:, 0:dm])) - 0.5
    lw = -jnp.exp(w)
    a = _sigmoid(a0_ref[...] + lo[:, dm:2 * dm])
    g = lo[:, 2 * dm:3 * dm]
    bo = bo_ref[...]
    kk = k * kk_ref[...]
    k2 = k * (1.0 + (a - 1.0) * ka_ref[...])
    sums = _head_sum(jnp.concatenate([kk * kk, r * k2 * rk_ref[...]], axis=0), bo)
    kk = kk / jnp.maximum(jnp.sqrt(sums[:tb]), 1e-12)
    bonus = sums[tb:]
    am = -kk
    bm = kk * a

    ri = _iota((tb, tb), 0)
    ci_ = _iota((tb, tb), 1)
    tri = ((ci_ <= ri) & (ci_ // L == ri // L)).astype(BF16)
    c = _mm((tri,), _sp(lw, 3))
    c_last = jnp.concatenate(
        [jnp.broadcast_to(c[(j + 1) * L - 1:(j + 1) * L, :], (L, c.shape[1])) for j in range(nc)], axis=0)
    a_t = am * jnp.exp(c - lw)
    r_t = r * jnp.exp(c)
    g_inv = jnp.exp(-c)
    k_t = k2 * g_inv
    b_t = bm * g_inv
    g_end = jnp.exp(c_last - c)
    k_end = k2 * g_end
    b_end = bm * g_end
    g_last = jnp.exp(c_last)

    cw = HEADS_PER_GROUP * L
    e_tile = ((_iota((L, cw), 1) % L) == _iota((L, cw), 0)).astype(BF16)
    tmask = (_iota((GROUP, cw), 0) // HEAD_DIM) == (_iota((GROUP, cw), 1) // L)
    col_s = _iota((L, cw), 1) % L
    row_t = _iota((L, cw), 0)
    strict = col_s < row_t
    incl = col_s <= row_t
    bdmask = (_iota((GROUP, GROUP), 0) // HEAD_DIM) == (_iota((GROUP, GROUP), 1) // HEAD_DIM)
    n_dbl = int(math.log2(L))

    bd_masks = {c: (_iota((cw, 4 * c), 0) // L) == (_iota((cw, 4 * c), 1) // c) for c in {HEAD_DIM, L}}

    def bd(x, n, cols_per_head=HEAD_DIM):
        return tuple(jnp.where(bd_masks[cols_per_head], jnp.concatenate([p] * HEADS_PER_GROUP, axis=0), 0.0)
                     for p in _sp(x, n))

    eye_cat = (col_s == row_t).astype(F32)

    insts = [(cj, gi) for cj in range(nc) for gi in range(N_GROUPS)]
    rsl = {cj: slice(cj * L, (cj + 1) * L) for cj in range(nc)}
    gsl = {gi: slice(gi * GROUP, (gi + 1) * GROUP) for gi in range(N_GROUPS)}
    bk_t = {cj: _mm(bf(jnp.concatenate([b_t[rsl[cj]], k_t[rsl[cj]]], axis=1)), (e_tile,), _TN) for cj in range(nc)}
    ar = {(cj, gi): bf(jnp.concatenate([a_t[rsl[cj], gsl[gi]], r_t[rsl[cj], gsl[gi]]], axis=0)) for cj, gi in insts}
    arb = {(cj, gi): _mm(ar[cj, gi], bf(jnp.where(tmask, bk_t[cj][gsl[gi]], 0.0))) for cj, gi in insts}
    ark = {(cj, gi): _mm(ar[cj, gi], bf(jnp.where(tmask, bk_t[cj][dm + gi * GROUP:dm + (gi + 1) * GROUP], 0.0)))
           for cj, gi in insts}
    p = {k_: jnp.where(strict, arb[k_][:L], 0.0) for k_ in insts}
    m_rb = {k_: bf(jnp.where(incl, arb[k_][L:], 0.0)) for k_ in insts}
    nmv = {(cj, gi): _mm(bf(jnp.concatenate([jnp.where(strict, ark[cj, gi][:L], 0.0),
                                              jnp.where(incl, ark[cj, gi][L:], 0.0)], axis=0)),
                         bd(v[rsl[cj], gsl[gi]], 1)) for cj, gi in insts}
    tm = {k_: eye_cat + p[k_] for k_ in insts}
    for i in range(n_dbl - 1):
        for k_ in insts:
            w = bd(p[k_], 1, L)
            if i == 0:
                p[k_] = _mm(bf(p[k_]), w)
            else:
                both = _mm(bf(jnp.concatenate([p[k_], tm[k_]], axis=0)), w)
                p[k_] = both[:L]
                tm[k_] = tm[k_] + both[L:]
    tm = {k_: bf(tm[k_] + _mm(bf(tm[k_]), bd(p[k_], 1, L))) for k_ in insts}

    groups = range(N_GROUPS)
    state = [st_sc[gi] for gi in groups]
    y_rows = []
    for cj in range(nc):
        rs = rsl[cj]
        ars = [_mm(ar[cj, gi], bf(state[gi]), _NT) for gi in groups]
        u = [_mm(tm[cj, gi], bd(ars[gi][:L] + nmv[cj, gi][:L], 1)) for gi in groups]
        ys = [ars[gi][L:] + nmv[cj, gi][L:] + _mm(m_rb[cj, gi], bd(u[gi], 1)) for gi in groups]
        upd = [_mm(bf(jnp.concatenate([v[rs, gsl[gi]], u[gi]], axis=0)),
                   bf(jnp.concatenate([k_end[rs, gsl[gi]], b_end[rs, gsl[gi]]], axis=0)), _TN) for gi in groups]
        state = [state[gi] * g_last[rs, gsl[gi]][0:1] + jnp.where(bdmask, upd[gi], 0.0) for gi in groups]
        y_rows.append(jnp.concatenate(ys, axis=1))
    for gi in groups:
        st_sc[gi] = state[gi]

    y = jnp.concatenate(y_rows, axis=0) if nc > 1 else y_rows[0]
    inv = 1.0 / HEAD_DIM
    mean = _head_sum(y, bo) * inv
    dlt = y - mean
    var = _head_sum(dlt * dlt, bo) * inv
    yn = dlt * lax.rsqrt(var + GN_EPS) * lng_ref[...] + lnb_ref[...]
    y_ref[0] = (yn + bonus * v) * g

    @pl.when(t == pl.num_programs(1) - 1)
    def _():
        for gi in range(N_GROUPS):
            st_ref[0, gi] = _unbd(st_sc[gi])


def _state_spec():
    return pl.BlockSpec((1, N_GROUPS, HEAD_DIM, GROUP), lambda i, j: (i, 0, 0, 0))


def _rwkv(za, shift0, s0_cat, pr, L, nc):
    b, t, ac = za.shape
    dm = D_MIX
    tb = L * nc
    row = lambda a: a.reshape(1, -1)
    consts = [row(pr["mu"]), row(pr["w0"]), row(pr["a0"]), pr["wl"], row(pr["k_k"]), row(pr["k_a"]),
              row(pr["r_k"]), row(pr["lnx_g"]), row(pr["lnx_b"]), _block_ones(GROUP, HEAD_DIM)]
    return pl.pallas_call(
        functools.partial(_rwkv_kernel, L=L, nc=nc),
        grid=(b, t // tb),
        in_specs=[pl.BlockSpec((1, tb, ac), lambda i, j: (i, j, 0)),
                  pl.BlockSpec((1, 1, ac), lambda i, j: (i, 0, 0)), _state_spec()]
                 + [_const_spec(c.shape) for c in consts],
        out_specs=[pl.BlockSpec((1, tb, dm), lambda i, j: (i, j, 0)), _state_spec()],
        out_shape=[jax.ShapeDtypeStruct((b, t, dm), F32),
                   jax.ShapeDtypeStruct((b, N_GROUPS, HEAD_DIM, GROUP), F32)],
        scratch_shapes=[pltpu.VMEM((1, ac), F32), pltpu.VMEM((N_GROUPS, GROUP, GROUP), F32)],
        compiler_params=_params(("parallel", "arbitrary")),
        name="rwkv7",
    )(za, shift0.reshape(b, 1, ac), s0_cat, *consts)


def _hgrn_kernel(zq_ref, zf_ref, zi_ref, zg_ref, s0_ref, lb_ref, ng_ref, bo_ref, y_ref, st_ref, st_sc, *, L, nc):
    t = pl.program_id(1)
    tb = nc * L

    @pl.when(t == 0)
    def _():
        for gi in range(N_GROUPS):
            st_sc[gi] = _bd(s0_ref[0, gi], HEAD_DIM, HEAD_DIM)

    dm = D_MIX
    q = _silu(zq_ref[0])
    xf = zf_ref[0]
    v = zi_ref[0]
    lb = lb_ref[...]
    la = jnp.log(lb)
    lc = jnp.log1p(-lb) + _log_sigmoid(xf)
    logf = jnp.maximum(la, lc) + jnp.log1p(jnp.exp(-jnp.abs(la - lc)))
    k = (1.0 - lb) * _sigmoid(-xf)
    bo = bo_ref[...]

    ri = _iota((tb, tb), 0)
    ci = _iota((tb, tb), 1)
    tri = ((ci <= ri) & (ci // L == ri // L)).astype(BF16)
    b = _mm((tri,), _sp(logf, 3))
    b_last = jnp.concatenate(
        [jnp.broadcast_to(b[(j + 1) * L - 1:(j + 1) * L, :], (L, dm)) for j in range(nc)], axis=0)
    qe = q * jnp.exp(b)
    k_end = k * jnp.exp(b_last - b)
    g_last = jnp.exp(b_last)

    n_sub = L // SUB
    nst = N_HEADS * SUB
    stackmask = (_iota((nst, dm), 0) // SUB) == (_iota((nst, dm), 1) // HEAD_DIM)
    sub_rows = _iota((SUB, 1), 0)
    subs = [(cj * L, cj * L + i * SUB, i) for cj in range(nc) for i in range(n_sub)]
    rows_of = {sb: slice(sb[1], sb[1] + SUB) for sb in subs}
    brefs = {sb: jnp.zeros((1, dm), F32) if sb[2] == 0 else b[sb[1] - 1:sb[1]] for sb in subs}

    def fold_heads(ov):
        out = ov[0:SUB]
        for hh in range(1, N_HEADS):
            out = out + ov[hh * SUB:(hh + 1) * SUB]
        return out

    def stacked_q(sb):
        qh = q[rows_of[sb]] * jnp.exp(b[rows_of[sb]] - brefs[sb])
        return bf(jnp.where(stackmask, jnp.concatenate([qh] * N_HEADS, axis=0), 0.0))

    def intra_exact():
        xs = {sb: [] for sb in subs}
        for s in range(SUB):
            for sb in subs:
                bi = b[rows_of[sb]]
                e = jnp.exp(jnp.where(sub_rows >= s, bi - bi[s:s + 1], NEG))
                xs[sb].append(q[rows_of[sb]] * e * k[rows_of[sb]][s:s + 1])
        att = {sb: _head_sum(jnp.concatenate(xs[sb], axis=0), bo, terms=1) for sb in subs}
        later = [sb for sb in subs if sb[2] > 0]
        qst = {sb: stacked_q(sb) for sb in later}
        kh = {sb: bf(k[sb[0]:sb[1]] * jnp.exp(brefs[sb] - b[sb[0]:sb[1]])) for sb in later}
        att2 = {sb: _mm(qst[sb], kh[sb], _NT) for sb in later}
        ov = {sb: jnp.where(stackmask, _mm(bf(att2[sb]), bf(v[sb[0]:sb[1]])), 0.0) for sb in later}
        o_subs = []
        for sb in subs:
            vi = v[rows_of[sb]]
            oi = att[sb][0:SUB] * vi[0:1]
            for s in range(1, SUB):
                oi = oi + att[sb][s * SUB:(s + 1) * SUB] * vi[s:s + 1]
            if sb[2] > 0:
                oi = oi + fold_heads(ov[sb])
            o_subs.append(oi)
        return jnp.concatenate(o_subs, axis=0) if len(o_subs) > 1 else o_subs[0]

    def intra_factored():
        qst = {sb: stacked_q(sb) for sb in subs}
        kh = {sb: bf(k[sb[0]:sb[1] + SUB] * jnp.exp(brefs[sb] - b[sb[0]:sb[1] + SUB])) for sb in subs}
        att2 = {}
        for sb in subs:
            n_keys = sb[1] + SUB - sb[0]
            visible = _iota((nst, n_keys), 1) <= sb[2] * SUB + _iota((nst, n_keys), 0) % SUB
            att2[sb] = jnp.where(visible, _mm(qst[sb], kh[sb], _NT), 0.0)
        ov = {sb: jnp.where(stackmask, _mm(bf(att2[sb]), bf(v[sb[0]:sb[1] + SUB])), 0.0) for sb in subs}
        o_subs = [fold_heads(ov[sb]) for sb in subs]
        return jnp.concatenate(o_subs, axis=0) if len(o_subs) > 1 else o_subs[0]

    drop = jnp.concatenate([b[sb[1] + SUB - 1:sb[1] + SUB] - brefs[sb] for sb in subs], axis=0)
    o_intra = lax.cond(jnp.min(drop) >= -FACTORED_MAX_DECAY, intra_factored, intra_exact)

    groups = range(N_GROUPS)
    gsl = {gi: slice(gi * GROUP, (gi + 1) * GROUP) for gi in groups}
    bdmask = (_iota((GROUP, GROUP), 0) // HEAD_DIM) == (_iota((GROUP, GROUP), 1) // HEAD_DIM)
    upd = {(cj, gi): jnp.where(bdmask, _mm(bf(v[cj * L:(cj + 1) * L, gsl[gi]]),
                                          bf(k_end[cj * L:(cj + 1) * L, gsl[gi]]), _TN), 0.0)
           for cj in range(nc) for gi in groups}
    state = [st_sc[gi] for gi in groups]
    o_rows = []
    for cj in range(nc):
        rs = slice(cj * L, (cj + 1) * L)
        o_rows.append(jnp.concatenate([_mm(bf(qe[rs, gsl[gi]]), bf(state[gi]), _NT) for gi in groups], axis=1))
        state = [state[gi] * g_last[rs, gsl[gi]][0:1] + upd[cj, gi] for gi in groups]
    for gi in groups:
        st_sc[gi] = state[gi]
    o = o_intra + (jnp.concatenate(o_rows, axis=0) if nc > 1 else o_rows[0])

    ms = _head_sum(o * o, bo) * (1.0 / HEAD_DIM)
    y_ref[0] = o * lax.rsqrt(ms + RMS_EPS) * ng_ref[...] * _silu(zg_ref[0])

    @pl.when(t == pl.num_programs(1) - 1)
    def _():
        for gi in range(N_GROUPS):
            st_ref[0, gi] = _unbd(st_sc[gi])


def _hgrn(zq, zf, zi, zg, s0_cat, lb, norm_g, L, nc):
    b, t, dm = zq.shape
    tb = L * nc
    consts = [lb.reshape(1, dm), norm_g.reshape(1, dm), _block_ones(GROUP, HEAD_DIM)]
    row_spec = pl.BlockSpec((1, tb, dm), lambda i, j: (i, j, 0))
    return pl.pallas_call(
        functools.partial(_hgrn_kernel, L=L, nc=nc),
        grid=(b, t // tb),
        in_specs=[row_spec] * 4 + [_state_spec()] + [_const_spec(c.shape) for c in consts],
        out_specs=[row_spec, _state_spec()],
        out_shape=[jax.ShapeDtypeStruct((b, t, dm), F32),
                   jax.ShapeDtypeStruct((b, N_GROUPS, HEAD_DIM, GROUP), F32)],
        scratch_shapes=[pltpu.VMEM((N_GROUPS, GROUP, GROUP), F32)],
        compiler_params=_params(("parallel", "arbitrary")),
        name="hgrn2",
    )(zq, zf, zi, zg, s0_cat, *consts)


def _cumsum_kernel(x_ref, o_ref):
    x = x_ref[0]
    n, w = x.shape
    nblk = n // N_HEADS
    upper = (_iota((w, w), 0) <= _iota((w, w), 1)).astype(F32)
    c = _dot(x, upper, HI)
    tot = jnp.broadcast_to(c[:, w - 1:w], (n, w))
    ri = _iota((n, n), 0)
    ci = _iota((n, n), 1)
    prior = ((ri // nblk == ci // nblk) & (ci < ri)).astype(F32)
    o_ref[0] = c + _dot(prior, tot, HI)


def _cumsum_time(logf_bht):
    b, h, t = logf_bht.shape
    w = 128
    n = h * (t // w)
    x = logf_bht.reshape(b, n, w)
    out = pl.pallas_call(
        _cumsum_kernel,
        grid=(b,),
        in_specs=[pl.BlockSpec((1, n, w), lambda i: (i, 0, 0))],
        out_specs=pl.BlockSpec((1, n, w), lambda i: (i, 0, 0)),
        out_shape=jax.ShapeDtypeStruct((b, n, w), F32),
        compiler_params=_params(("parallel",)),
        name="cumsum",
    )(x)
    return out.reshape(b, h, t)


def _fox_kernel(q_ref, k_ref, v_ref, f_ref, o_ref, kb_sc, vb_sc, *, tq):
    i = pl.program_id(2)

    @pl.when(i == 0)
    def _():
        kb_sc[...] = k_ref[0].astype(BF16)
        vb_sc[...] = v_ref[0].astype(BF16)

    q = q_ref[0] * ATTN_SCALE
    lane = _iota((1, 2 * HEAD_DIM), 1)
    causal = _iota((tq, tq), 1) <= _iota((tq, tq), 0)
    heads = range(2)
    head_lane = [(lane // HEAD_DIM) == hh for hh in heads]
    qm = [jnp.where(head_lane[hh], q, 0.0).astype(BF16) for hh in heads]

    def scores(j, hh):
        return _dot_nt(qm[hh], kb_sc[pl.ds(j * tq, tq), :]) - f_ref[0, 0, j, hh:hh + 1, :]

    def update(j, hh, s, m, acc):
        m_new = jnp.maximum(m, jnp.max(s, axis=-1, keepdims=True))
        p = jnp.exp(s - m_new).astype(BF16)
        vx = jnp.where(head_lane[hh], vb_sc[pl.ds(j * tq, tq), :], 1.0)
        return m_new, jnp.exp(m - m_new) * acc + _dot(p, vx)

    def run_blocks(first, count, diagonal_last, carry):
        ss = {(n, hh): scores(first + n, hh) for n in range(count) for hh in heads}
        if diagonal_last:
            for hh in heads:
                ss[count - 1, hh] = jnp.where(causal, ss[count - 1, hh], NEG)
        carry = list(carry)
        for n in range(count):
            for hh in heads:
                carry[hh] = update(first + n, hh, ss[n, hh], *carry[hh])
        return tuple(carry)

    init = tuple((jnp.full((tq, 1), NEG, F32), jnp.zeros((tq, 2 * HEAD_DIM), F32)) for _ in heads)
    n_full = i // FOX_GROUP
    carry = lax.fori_loop(0, n_full, lambda g, cr: run_blocks(g * FOX_GROUP, FOX_GROUP, False, cr), init)
    tails = [functools.partial(run_blocks, n_full * FOX_GROUP, r + 1, True) for r in range(FOX_GROUP)]
    (_, acc0), (_, acc1) = lax.switch(i % FOX_GROUP, tails, carry)
    outs = [acc / pltpu.roll(acc, HEAD_DIM, 1) for acc in (acc0, acc1)]
    o_ref[0] = jnp.where(head_lane[0], outs[0], outs[1])


def _fox_prompt(q, k, v, cum_bht, tq):
    b, t, dm = q.shape
    hp = dm // (2 * HEAD_DIM)
    nk = t // tq
    f = cum_bht.reshape(b, hp, 2, nk, tq).transpose(0, 1, 3, 2, 4)
    kv_spec = pl.BlockSpec((1, t, 2 * HEAD_DIM), lambda i, h, j: (i, 0, h))
    return pl.pallas_call(
        functools.partial(_fox_kernel, tq=tq),
        scratch_shapes=[pltpu.VMEM((t, 2 * HEAD_DIM), BF16), pltpu.VMEM((t, 2 * HEAD_DIM), BF16)],
        grid=(b, hp, nk),
        in_specs=[pl.BlockSpec((1, tq, 2 * HEAD_DIM), lambda i, h, j: (i, j, h)), kv_spec, kv_spec,
                  pl.BlockSpec((1, 1, nk, 2, tq), lambda i, h, j: (i, h, 0, 0, 0))],
        out_specs=pl.BlockSpec((1, tq, 2 * HEAD_DIM), lambda i, h, j: (i, j, h)),
        out_shape=jax.ShapeDtypeStruct((b, t, dm), F32),
        compiler_params=_params(("parallel", "parallel", "arbitrary")),
        name="fox_prompt",
    )(q, k, v, f)


def _band_kernel(q_ref, k_ref, v_ref, bias_ref, o_ref, *, tq):
    i = pl.program_id(2)
    q = q_ref[0] * ATTN_SCALE
    lane = _iota((1, 2 * HEAD_DIM), 1)
    n_piece = BAND // tq + 1
    kbs, vs, valid = [], [], []
    for p in range(n_piece):
        blk = i - (n_piece - 1) + p
        start = jnp.maximum(blk, 0) * tq
        kbs.append(k_ref[0, pl.ds(start, tq), :].astype(BF16))
        vs.append(v_ref[0, pl.ds(start, tq), :].astype(BF16))
        valid.append(blk >= 0)
    scores = {}
    for hh in range(2):
        qm = jnp.where((lane // HEAD_DIM) == hh, q, 0.0).astype(BF16)
        for p in range(n_piece):
            s = _dot_nt(qm, kbs[p]) + bias_ref[0, hh, :, p * tq:(p + 1) * tq]
            scores[hh, p] = jnp.where(valid[p], s, NEG)
    outs = []
    for hh in range(2):
        ss = [scores[hh, p] for p in range(n_piece)]
        m = jnp.max(ss[0], axis=-1, keepdims=True)
        for s in ss[1:]:
            m = jnp.maximum(m, jnp.max(s, axis=-1, keepdims=True))
        l = jnp.zeros((tq, 1), F32)
        acc = jnp.zeros((tq, 2 * HEAD_DIM), F32)
        for s, vb in zip(ss, vs):
            p_ = jnp.exp(s - m)
            l = l + jnp.sum(p_, axis=-1, keepdims=True)
            acc = acc + _dot(p_.astype(BF16), vb)
        outs.append(acc / l)
    o_ref[0] = jnp.where((lane // HEAD_DIM) == 0, outs[0], outs[1])


def _toeplitz_bias(rel_bias, nq, nk, offset):
    h = rel_bias.shape[0]
    m = jnp.arange(nq + nk - 1)
    u = rel_bias[:, jnp.clip(offset + nq - 1 - m, -REL_CLIP, REL_CLIP) + REL_CLIP].astype(F32)
    period = nq + nk
    up = jnp.pad(u, ((0, 0), (0, 1)))
    skew = jnp.tile(up, (1, nq))[:, :nq * (period - 1)].reshape(h, nq, period - 1)
    return skew[:, :, nq - 1:]


def _band_bias_prompt(rel_bias, tq):
    nk = BAND + tq
    bias = _toeplitz_bias(rel_bias, tq, nk, BAND)
    qc = jnp.arange(tq)[:, None] // CHUNK
    kc = jnp.arange(nk)[None, :] // CHUNK
    valid = (kc >= qc) & (kc <= qc + BAND // CHUNK)
    return jnp.where(valid[None], bias, NEG)


def _band_prompt(q, k, v, rel_bias, tq):
    b, t, dm = q.shape
    hp = dm // (2 * HEAD_DIM)
    bias = _band_bias_prompt(rel_bias, tq).reshape(hp, 2, tq, BAND + tq)
    kv_spec = pl.BlockSpec((1, t, 2 * HEAD_DIM), lambda i, h, j: (i, 0, h))
    return pl.pallas_call(
        functools.partial(_band_kernel, tq=tq),
        grid=(b, hp, t // tq),
        in_specs=[pl.BlockSpec((1, tq, 2 * HEAD_DIM), lambda i, h, j: (i, j, h)), kv_spec, kv_spec,
                  pl.BlockSpec((1, 2, tq, BAND + tq), lambda i, h, j: (h, 0, 0, 0))],
        out_specs=pl.BlockSpec((1, tq, 2 * HEAD_DIM), lambda i, h, j: (i, j, h)),
        out_shape=jax.ShapeDtypeStruct((b, t, dm), F32),
        compiler_params=_params(("parallel", "parallel", "arbitrary")),
        name="band_prompt",
    )(q, k, v, bias)


def _cached_attn_kernel(*refs, fox):
    if fox:
        q_ref, kn_ref, vn_ref, ck_ref, cv_ref, fc_ref, fn_ref, o_ref = refs
    else:
        q_ref, kn_ref, vn_ref, ck_ref, cv_ref, bc_ref, bn_ref, o_ref = refs
    tn, dm = q_ref.shape[1], q_ref.shape[2]
    nst = N_HEADS * tn
    stackmask = (_iota((nst, dm), 0) // tn) == (_iota((nst, dm), 1) // HEAD_DIM)
    q = q_ref[0] * ATTN_SCALE
    qst = jnp.where(stackmask, jnp.concatenate([q] * N_HEADS, axis=0), 0.0).astype(BF16)
    s_c = _dot_nt(qst, ck_ref[0].astype(BF16))
    s_n = _dot_nt(qst, kn_ref[0].astype(BF16))
    if fox:
        expand = ((_iota((nst, N_HEADS), 0) // tn) == _iota((nst, N_HEADS), 1)).astype(F32)
        fc = fc_ref[0]
        upper = (_iota((tn, tn), 0) <= _iota((tn, tn), 1)).astype(F32)
        fnew = fc[:, fc.shape[1] - 1:] + _dot(fn_ref[0], upper, HI)
        s_c = s_c - _dot(expand, fc, HI)
        s_n = s_n - _dot(expand, fnew, HI)
        tq = _iota((nst, tn), 0) % tn
        s_n = jnp.where(_iota((nst, tn), 1) <= tq, s_n, NEG)
    else:
        s_c = s_c + bc_ref[...]
        s_n = s_n + bn_ref[...]
    m = jnp.maximum(jnp.max(s_c, axis=-1, keepdims=True), jnp.max(s_n, axis=-1, keepdims=True))
    p_c = jnp.exp(s_c - m)
    p_n = jnp.exp(s_n - m)
    l = jnp.sum(p_c, axis=-1, keepdims=True) + jnp.sum(p_n, axis=-1, keepdims=True)
    ov = _dot(p_c.astype(BF16), cv_ref[0].astype(BF16)) + _dot(p_n.astype(BF16), vn_ref[0].astype(BF16))
    ov = jnp.where(stackmask, ov / l, 0.0)
    out = ov[0:tn]
    for hh in range(1, N_HEADS):
        out = out + ov[hh * tn:(hh + 1) * tn]
    o_ref[0] = out


def _cached_attn(q, kn, vn, ck, cv, extra_c, extra_n, fox):
    b, tn, dm = q.shape
    p = ck.shape[1]
    new_spec = pl.BlockSpec((1, tn, dm), lambda i: (i, 0, 0))
    cache_spec = pl.BlockSpec((1, p, dm), lambda i: (i, 0, 0))
    if fox:
        ex_specs = [pl.BlockSpec((1, N_HEADS, p), lambda i: (i, 0, 0)),
                    pl.BlockSpec((1, N_HEADS, tn), lambda i: (i, 0, 0))]
    else:
        ex_specs = [_const_spec(extra_c.shape), _const_spec(extra_n.shape)]
    return pl.pallas_call(
        functools.partial(_cached_attn_kernel, fox=fox),
        grid=(b,),
        in_specs=[new_spec, new_spec, new_spec, cache_spec, cache_spec] + ex_specs,
        out_specs=new_spec,
        out_shape=jax.ShapeDtypeStruct((b, tn, dm), F32),
        compiler_params=_params(("parallel",)),
        name="fox_sample" if fox else "band_sample",
    )(q, kn, vn, ck, cv, extra_c, extra_n)


def _to_cat(s):
    b = s.shape[0]
    s5 = s.reshape(b, N_GROUPS, HEADS_PER_GROUP, HEAD_DIM, HEAD_DIM)
    return s5.transpose(0, 1, 3, 2, 4).reshape(b, N_GROUPS, HEAD_DIM, GROUP)


def _from_cat(c):
    b = c.shape[0]
    c5 = c.reshape(b, N_GROUPS, HEAD_DIM, HEADS_PER_GROUP, HEAD_DIM)
    return c5.transpose(0, 1, 3, 2, 4).reshape(b, N_HEADS, HEAD_DIM, HEAD_DIM)


def _trunk(x, mods, P, cache, tm, tq, L, nc):
    b, t, d = x.shape
    dm = D_MIX
    sample = cache is not None
    new = {}
    za, q, k, v, logf = _proj(x, mods[0], P["norm_mix_g"][0], P["ab_w"], (P["a_cols"], dm, dm, dm), tm,
                              wf=P["ab_wf"], bf=P["fox_bf"])
    ac = P["a_cols"]
    if sample:
        nb, tn = cache["nb"], cache["tn"]
        za_b = za.reshape(nb, tn, ac)
        shift0 = cache["rwkv_shift"][0]
        s0 = _to_cat(cache["rwkv"][0])
    else:
        nb, tn = b, t
        za_b = za
        shift0 = jnp.zeros((nb, ac), F32)
        s0 = jnp.zeros((nb, N_GROUPS, HEAD_DIM, GROUP), F32)
    ya, st = _rwkv(za_b, shift0, s0, P["rwkv"], L, nc)
    new["rwkv"] = _from_cat(st)[None]
    new["rwkv_shift"] = za_b[:, -1][None]
    qb, kb, vb = (a.reshape(nb, tn, dm) for a in (q, k, v))
    logf_b = logf.reshape(nb, tn, N_HEADS)
    logf_t = jnp.swapaxes(logf_b, 1, 2)
    if sample:
        ck = cache["fox_k"][0].reshape(nb, -1, dm)
        cv = cache["fox_v"][0].reshape(nb, -1, dm)
        fc = _cumsum_time(jnp.swapaxes(cache["fox_logf"][0], 1, 2))
        yb = _cached_attn(qb, kb, vb, ck, cv, fc, logf_t, fox=True)
    else:
        yb = _fox_prompt(qb, kb, vb, _cumsum_time(logf_t), tq)
    new["fox_k"] = kb.reshape(1, nb, tn, N_HEADS, HEAD_DIM)
    new["fox_v"] = vb.reshape(1, nb, tn, N_HEADS, HEAD_DIM)
    new["fox_logf"] = logf_b[None]
    fix = None
    if sample:
        buf = cache["ffn_conv"][0]
        zero = jnp.zeros((nb, tn - 2, buf.shape[-1]), F32)
        fix = (jnp.concatenate([buf[:, 1:2], buf[:, 0:1] * 0, zero], axis=1).reshape(1, nb * tn, -1),
               jnp.concatenate([buf, zero], axis=1).reshape(1, nb * tn, -1))
    res = _tail(x, ya.reshape(b, t, dm), yb.reshape(b, t, dm), mods[0], mods[1], P["ab_wo"],
                P["norm_ffn_g"][0], P["wup"][0], P["conv_w"][0], P["conv_b"][0], P["wdn"][0],
                P["final_g"], False, tm, fix=fix, period=tn)
    x = res[0]
    conv0 = res[1].reshape(nb, tn, -1)[:, tn - 2:] if sample else res[1]
    q, k, v, zq, zf, zi, zg = _proj(x, mods[2], P["norm_mix_g"][1], P["cd_w"], (dm,) * 7, tm)
    qb, kb, vb, zq, zf, zi, zg = (a.reshape(nb, tn, dm) for a in (q, k, v, zq, zf, zi, zg))
    if sample:
        ck = cache["chunk_k"][0].reshape(nb, -1, dm)
        cv = cache["chunk_v"][0].reshape(nb, -1, dm)
        yc = _cached_attn(qb, kb, vb, ck, cv, P["band_bias_c"], P["band_bias_n"], fox=False)
        new["chunk_k"] = kb.reshape(1, nb, tn, N_HEADS, HEAD_DIM)
        new["chunk_v"] = vb.reshape(1, nb, tn, N_HEADS, HEAD_DIM)
        s0 = _to_cat(jnp.swapaxes(cache["hgrn"][0], -1, -2))
    else:
        yc = _band_prompt(qb, kb, vb, P["rel_bias"], 4 * CHUNK)
        keep = min(BAND, tn)
        new["chunk_k"] = kb[:, tn - keep:].reshape(1, nb, keep, N_HEADS, HEAD_DIM)
        new["chunk_v"] = vb[:, tn - keep:].reshape(1, nb, keep, N_HEADS, HEAD_DIM)
        s0 = jnp.zeros((nb, N_GROUPS, HEAD_DIM, GROUP), F32)
    yd, st = _hgrn(zq, zf, zi, zg, s0, P["hgrn_lb"], P["hgrn_norm_g"], L, nc)
    new["hgrn"] = jnp.swapaxes(_from_cat(st), -1, -2)[None]
    if sample:
        buf = cache["ffn_conv"][1]
        fix = (jnp.concatenate([buf[:, 1:2], buf[:, 0:1] * 0, zero], axis=1).reshape(1, nb * tn, -1),
               jnp.concatenate([buf, zero], axis=1).reshape(1, nb * tn, -1))
    res = _tail(x, yc.reshape(b, t, dm), yd.reshape(b, t, dm), mods[2], mods[3], P["cd_wo"],
                P["norm_ffn_g"][1], P["wup"][1], P["conv_w"][1], P["conv_b"][1], P["wdn"][1],
                P["final_g"], True, tm, fix=fix, period=tn)
    conv1 = res[1].reshape(nb, tn, -1)[:, tn - 2:] if sample else res[1]
    new["ffn_conv"] = jnp.stack([conv0, conv1])
    y = res[0].reshape(nb, tn, d)
    return y, new


def kernel(x_prompt, x_sample, c_prompt, c_sample, cache_fox_k, cache_fox_v, cache_fox_logf, state_rwkv,
           state_rwkv_shift, cache_chunk_k, cache_chunk_v, state_hgrn, state_ffn_conv, ada_w, ada_b,
           norm_mix_g, norm_ffn_g, ab_w_in, rwkv_mu, rwkv_w0, rwkv_w2, rwkv_a0, rwkv_a2, rwkv_g2,
           rwkv_k_k, rwkv_k_a, rwkv_r_k, rwkv_lnx_g, rwkv_lnx_b, fox_b_f, ab_w_out, cd_w_in,
           chunk_rel_bias, hgrn_lb_table, hgrn_norm_g, cd_w_out, ffn_w_up, ffn_conv_w, ffn_conv_b,
           ffn_w_down, final_norm_g):
    bp, tp, d = x_prompt.shape
    bs, ts, _ = x_sample.shape
    depth = ada_w.shape[0]
    dm = D_MIX
    a_cols = rwkv_mu.shape[1]
    n_lw, n_la, n_lg = rwkv_w2.shape[1], rwkv_a2.shape[1], rwkv_g2.shape[1]

    c_all = jnp.concatenate([c_prompt, c_sample], axis=0)
    mods = _adaln(c_all, ada_w.reshape(depth * 2, d, 3 * d), ada_b.reshape(depth * 2, 3 * d))
    mods_p = [mods[i, :bp].reshape(bp, 1, 3 * d) for i in range(depth * 2)]
    mods_s = [jnp.repeat(mods[i, bp:], ts, axis=0).reshape(1, bs * ts, 3 * d) for i in range(depth * 2)]

    wl = jnp.zeros((n_lw + n_la + n_lg, 3 * dm), F32)
    wl = wl.at[:n_lw, 0:dm].set(rwkv_w2[0])
    wl = wl.at[n_lw:n_lw + n_la, dm:2 * dm].set(rwkv_a2[0])
    wl = wl.at[n_lw + n_la:, 2 * dm:].set(rwkv_g2[0])
    sm = jax.nn.softmax(hgrn_lb_table.astype(F32), axis=0)
    lb = (jnp.cumsum(sm, axis=0) - sm[0])[1]
    wf = jnp.zeros((d, 128), F32).at[:, :N_HEADS].set(ab_w_in[0][:, a_cols + 3 * dm:]).astype(BF16)
    bf = jnp.zeros((1, 128), F32).at[0, :N_HEADS].set(fox_b_f[0])
    p_c = cache_chunk_k.shape[2]
    bias_s = _toeplitz_bias(chunk_rel_bias[0], ts, p_c + ts, p_c).reshape(N_HEADS * ts, p_c + ts)

    P = {
        "a_cols": a_cols,
        "norm_mix_g": norm_mix_g, "norm_ffn_g": norm_ffn_g, "final_g": final_norm_g,
        "ab_w": ab_w_in[0][:, :a_cols + 3 * dm].astype(BF16), "ab_wf": wf, "fox_bf": bf,
        "ab_wo": ab_w_out[0].astype(BF16),
        "cd_w": cd_w_in[0].astype(BF16), "cd_wo": cd_w_out[0].astype(BF16),
        "wup": [ffn_w_up[i].astype(BF16) for i in range(depth)],
        "wdn": [ffn_w_down[i].astype(BF16) for i in range(depth)],
        "conv_w": ffn_conv_w, "conv_b": ffn_conv_b,
        "rwkv": {"mu": rwkv_mu[0], "w0": rwkv_w0[0], "a0": rwkv_a0[0], "wl": wl.astype(BF16), "k_k": rwkv_k_k[0],
                 "k_a": rwkv_k_a[0], "r_k": rwkv_r_k[0], "lnx_g": rwkv_lnx_g[0], "lnx_b": rwkv_lnx_b[0]},
        "rel_bias": chunk_rel_bias[0], "band_bias_c": bias_s[:, :p_c], "band_bias_n": bias_s[:, p_c:],
        "hgrn_lb": lb, "hgrn_norm_g": hgrn_norm_g[0],
    }
    cache = {"nb": bs, "tn": ts, "fox_k": cache_fox_k, "fox_v": cache_fox_v, "fox_logf": cache_fox_logf,
             "rwkv": state_rwkv, "rwkv_shift": state_rwkv_shift, "chunk_k": cache_chunk_k,
             "chunk_v": cache_chunk_v, "hgrn": state_hgrn, "ffn_conv": state_ffn_conv}

    y_p, sp = _trunk(x_prompt, mods_p, P, None, tm=512, tq=512, L=CHUNK, nc=4)
    y_s, ss = _trunk(x_sample.reshape(1, bs * ts, d), mods_s, P, cache, tm=bs * ts, tq=None, L=ts, nc=1)
    names = ("fox_k", "fox_v", "fox_logf", "rwkv", "rwkv_shift", "chunk_k", "chunk_v", "hgrn", "ffn_conv")
    return (y_p, y_s) + tuple(sp[n] for n in names) + tuple(ss[n] for n in names)
```

```python
import functools
import math

import jax
import jax.numpy as jnp
from jax import lax
from jax.experimental import pallas as pl
from jax.experimental.pallas import tpu as pltpu

F32 = jnp.float32
BF16 = jnp.bfloat16

HEAD_DIM = 64
N_HEADS = 8
D_MIX = N_HEADS * HEAD_DIM
GROUP = 256
N_GROUPS = D_MIX // GROUP
HEADS_PER_GROUP = GROUP // HEAD_DIM
SUB = 16
CHUNK = 64
BAND = 512
REL_CLIP = 128
RMS_EPS = 1e-6
GN_EPS = 64e-5
ATTN_SCALE = HEAD_DIM ** -0.5
NEG = -1e30
FACTORED_MAX_DECAY = 60.0
FOX_GROUP = 4
UP_LOOKAHEAD = 4
VMEM_LIMIT = 56 * 1024 * 1024


def _dot(a, b):
    return jnp.dot(a, b, preferred_element_type=F32)


def _dot_nt(a, b):
    return lax.dot_general(a, b, (((1,), (1,)), ((), ())), preferred_element_type=F32)


_NN = ((1,), (0,))
_NT = ((1,), (1,))
_TN = ((0,), (0,))


def _sp(x, n):
    hi = x.astype(BF16)
    if n == 1:
        return (hi,)
    r = x - hi.astype(F32)
    mid = r.astype(BF16)
    if n == 2:
        return (hi, mid)
    return (hi, mid, (r - mid.astype(F32)).astype(BF16))


def bf(x):
    return (x.astype(BF16),)


def _mm(a, b, dims=_NN):
    n = max(len(a), len(b))
    out = None
    for i, ai in enumerate(a):
        for j, bj in enumerate(b):
            if i + j < n:
                d = lax.dot_general(ai, bj, (dims, ((), ())), preferred_element_type=F32)
                out = d if out is None else out + d
    return out


def _head_sum(x, bo_group, terms=2):
    return jnp.concatenate(
        [_mm(_sp(x[:, gi * GROUP:(gi + 1) * GROUP], terms), (bo_group,)) for gi in range(N_GROUPS)], axis=1)


def _sigmoid(x):
    return jax.nn.sigmoid(x)


def _silu(x):
    return x * jax.nn.sigmoid(x)


def _softplus(x):
    return jnp.maximum(x, 0.0) + jnp.log1p(jnp.exp(-jnp.abs(x)))


def _log_sigmoid(x):
    return -_softplus(-x)


def _iota(shape, dim):
    return lax.broadcasted_iota(jnp.int32, shape, dim)


def _const_spec(shape):
    nd = len(shape)
    return pl.BlockSpec(shape, lambda *_: (0,) * nd, pipeline_mode=pl.Buffered(1))


def _params(sem):
    return pltpu.CompilerParams(dimension_semantics=sem, vmem_limit_bytes=VMEM_LIMIT)


def _norm_mod(x, g, shift, scale):
    xn = x * lax.rsqrt(jnp.mean(x * x, axis=-1, keepdims=True) + RMS_EPS) * g
    return xn * (1.0 + scale) + shift


def _block_ones(n, blk):
    i = jnp.arange(n) // blk
    return (i[:, None] == i[None, :]).astype(BF16)


def _bd(x, rows_per_head, cols_per_head):
    n = HEADS_PER_GROUP
    t = jnp.concatenate([x] * n, axis=0)
    r = _iota(t.shape, 0) // rows_per_head
    c = _iota(t.shape, 1) // cols_per_head
    return jnp.where(r == c, t, 0.0)


def _unbd(m):
    out = m[0:HEAD_DIM]
    for h in range(1, HEADS_PER_GROUP):
        out = out + m[h * HEAD_DIM:(h + 1) * HEAD_DIM]
    return out


def _adaln_kernel(c_ref, w_ref, b_ref, o_ref):
    o_ref[0] = _mm(_sp(_silu(c_ref[...]), 2), bf(w_ref[0])) + b_ref[0]


def _adaln(c_all, ada_w, ada_b):
    n, d, d3 = ada_w.shape
    bt = c_all.shape[0]
    nt = d3 // d
    return pl.pallas_call(
        _adaln_kernel,
        grid=(n, nt),
        in_specs=[pl.BlockSpec((bt, d), lambda i, j: (0, 0)),
                  pl.BlockSpec((1, d, d), lambda i, j: (i, 0, j)),
                  pl.BlockSpec((1, 1, d), lambda i, j: (i, 0, j))],
        out_specs=pl.BlockSpec((1, bt, d), lambda i, j: (i, 0, j)),
        out_shape=jax.ShapeDtypeStruct((n, bt, d3), F32),
        compiler_params=_params(("parallel", "parallel")),
        name="adaln",
    )(c_all, ada_w, ada_b.reshape(n, 1, d3))


def _proj_kernel(*refs, splits, fox):
    x_ref, mod_ref, g_ref, w_ref = refs[:4]
    pos = 4
    if fox:
        wf_ref, bf_ref = refs[4:6]
        pos = 6
    outs = refs[pos:]
    d = x_ref.shape[-1]
    mod = mod_ref[0]
    h = _norm_mod(x_ref[0], g_ref[...], mod[:, :d], mod[:, d:2 * d]).astype(BF16)
    off = 0
    for o_ref, n in zip(outs, splits):
        o_ref[0] = _dot(h, w_ref[:, off:off + n])
        off += n
    if fox:
        zf = _dot(h, wf_ref[...]) + bf_ref[...]
        outs[len(splits)][0] = _log_sigmoid(zf)[:, :N_HEADS]


def _proj(x, mod, g, w_bf, splits, tm, wf=None, bf=None):
    b, t, d = x.shape
    r = mod.shape[1]
    fox = wf is not None
    mod_spec = (pl.BlockSpec((1, 1, 3 * d), lambda i, j: (i, 0, 0)) if r == 1
                else pl.BlockSpec((1, tm, 3 * d), lambda i, j: (i, j, 0)))
    in_specs = [pl.BlockSpec((1, tm, d), lambda i, j: (i, j, 0)), mod_spec,
                _const_spec((1, d)), _const_spec(w_bf.shape)]
    args = [x, mod, g.reshape(1, d), w_bf]
    widths = list(splits)
    if fox:
        in_specs += [_const_spec(wf.shape), _const_spec(bf.shape)]
        args += [wf, bf]
        widths.append(N_HEADS)
    return pl.pallas_call(
        functools.partial(_proj_kernel, splits=tuple(splits), fox=fox),
        grid=(b, t // tm),
        in_specs=in_specs,
        out_specs=[pl.BlockSpec((1, tm, n), lambda i, j: (i, j, 0)) for n in widths],
        out_shape=[jax.ShapeDtypeStruct((b, t, n), F32) for n in widths],
        compiler_params=_params(("parallel", "parallel")),
        name="proj",
    )(*args)


def _tail_kernel(*refs, sample, final, period, cb, down_group):
    (x_ref, ya_ref, yb_ref, m1_ref, m2_ref, wo_ref, g_ref, wup_ref, cw_ref, cbias_ref,
     wdn_ref, fg_ref) = refs[:12]
    if sample:
        fix1_ref, fix2_ref, o_ref, u_ref, g_sc = refs[12:]
    else:
        o_ref, cs_ref, g_sc, tail_sc = refs[12:]
    tm, d = x_ref.shape[1], x_ref.shape[2]
    ff = wdn_ref.shape[0]
    dm = ya_ref.shape[2]
    m1 = m1_ref[0]
    m2 = m2_ref[0]
    ymix = (_dot(ya_ref[0].astype(BF16), wo_ref[0:dm, :])
            + _dot(yb_ref[0].astype(BF16), wo_ref[dm:2 * dm, :]))
    x1 = x_ref[0] + m1[:, 2 * d:3 * d] * ymix
    h = _norm_mod(x1, g_ref[...], m2[:, :d], m2[:, d:2 * d]).astype(BF16)
    rows = _iota((tm, 1), 0)
    if sample:
        tpos = rows & (period - 1)
    else:
        t = pl.program_id(1)

        @pl.when(t == 0)
        def _():
            tail_sc[...] = jnp.zeros_like(tail_sc)

    def up(j):
        return [_dot(h, wup_ref[:, half * ff + j * cb:half * ff + (j + 1) * cb]) for half in range(2)]

    def conv(u, u1, u2, c0):
        cw = cw_ref[:, c0:c0 + cb]
        return cbias_ref[:, c0:c0 + cb] + cw[0:1] * u2 + cw[1:2] * u1 + cw[2:3] * u

    def gated(j, us):
        halves = []
        for half, u in enumerate(us):
            c0 = half * ff + j * cb
            if sample:
                u1 = jnp.where(tpos == 0, fix1_ref[0, :, c0:c0 + cb], pltpu.roll(u, 1, 0))
                u2 = jnp.where(tpos < 2, fix2_ref[0, :, c0:c0 + cb], pltpu.roll(u, 2, 0))
                u_ref[0, :, c0:c0 + cb] = u
                halves.append(conv(u, u1, u2, c0))
            else:
                edge = jnp.concatenate([tail_sc[:, c0:c0 + cb], u[0:8]], axis=0)
                first = conv(edge, pltpu.roll(edge, 1, 0), pltpu.roll(edge, 2, 0), c0)[8:16]
                tail_sc[:, c0:c0 + cb] = u[tm - 8:tm, :]
                main = conv(u, pltpu.roll(u, 1, 0), pltpu.roll(u, 2, 0), c0)
                halves.append(jnp.concatenate([first, main[8:]], axis=0))
        return (_silu(halves[0]) * halves[1]).astype(BF16)

    acc = None
    n_blk = ff // cb
    pending = [up(j) for j in range(min(UP_LOOKAHEAD, n_blk))]
    j0 = 0
    for j in range(n_blk):
        if j + UP_LOOKAHEAD < n_blk:
            pending.append(up(j + UP_LOOKAHEAD))
        g_sc[:, j * cb:(j + 1) * cb] = gated(j, pending.pop(0))
        if (j + 1 - j0) == down_group or j + 1 == n_blk:
            part = _dot(g_sc[:, j0 * cb:(j + 1) * cb], wdn_ref[j0 * cb:(j + 1) * cb, :])
            acc = part if acc is None else acc + part
            j0 = j + 1
    out = x1 + m2[:, 2 * d:3 * d] * acc
    if final:
        out = out * lax.rsqrt(jnp.mean(out * out, axis=-1, keepdims=True) + RMS_EPS) * fg_ref[...]
    o_ref[0] = out
    if not sample:
        @pl.when(t == pl.num_programs(1) - 1)
        def _():
            cs_ref[0] = tail_sc[6:8, :]


def _tail(x, ya, yb, mod1, mod2, wo_bf, g, wup_bf, conv_w, conv_b, wdn_bf, fg, final, tm,
          fix=None, period=None):
    b, t, d = x.shape
    ff = wdn_bf.shape[0]
    dm = ya.shape[2]
    sample = fix is not None
    r = mod1.shape[1]
    mod_spec = (pl.BlockSpec((1, 1, 3 * d), lambda i, j: (i, 0, 0)) if r == 1
                else pl.BlockSpec((1, tm, 3 * d), lambda i, j: (i, j, 0)))
    row_spec = lambda n: pl.BlockSpec((1, tm, n), lambda i, j: (i, j, 0))
    in_specs = [row_spec(d), row_spec(dm), row_spec(dm), mod_spec, mod_spec,
                _const_spec(wo_bf.shape), _const_spec((1, d)), _const_spec(wup_bf.shape),
                _const_spec(conv_w.shape), _const_spec((1, 2 * ff)), _const_spec(wdn_bf.shape),
                _const_spec((1, d))]
    args = [x, ya, yb, mod1, mod2, wo_bf, g.reshape(1, d), wup_bf, conv_w, conv_b.reshape(1, 2 * ff),
            wdn_bf, fg.reshape(1, d)]
    if sample:
        in_specs += [row_spec(2 * ff), row_spec(2 * ff)]
        args += list(fix)
        out_specs = [row_spec(d), row_spec(2 * ff)]
        out_shape = [jax.ShapeDtypeStruct((b, t, d), F32), jax.ShapeDtypeStruct((b, t, 2 * ff), F32)]
        scratch = [pltpu.VMEM((tm, ff), BF16)]
    else:
        out_specs = [row_spec(d), pl.BlockSpec((1, 2, 2 * ff), lambda i, j: (i, 0, 0))]
        out_shape = [jax.ShapeDtypeStruct((b, t, d), F32), jax.ShapeDtypeStruct((b, 2, 2 * ff), F32)]
        scratch = [pltpu.VMEM((tm, ff), BF16), pltpu.VMEM((8, 2 * ff), F32)]
    return pl.pallas_call(
        functools.partial(_tail_kernel, sample=sample, final=final, period=period, cb=256, down_group=4),
        grid=(b, t // tm),
        in_specs=in_specs,
        out_specs=out_specs,
        out_shape=out_shape,
        scratch_shapes=scratch,
        compiler_params=_params(("parallel", "arbitrary")),
        name="tail",
    )(*args)


def _rwkv_kernel(z_ref, shift0_ref, s0_ref, mu_ref, w0_ref, a0_ref, wl_ref, kk_ref, ka_ref, rk_ref,
                 lng_ref, lnb_ref, bo_ref, y_ref, st_ref, prev_sc, st_sc, *, L, nc):
    t = pl.program_id(1)
    tb = nc * L

    @pl.when(t == 0)
    def _():
        prev_sc[...] = shift0_ref[0]
        for gi in range(N_GROUPS):
            st_sc[gi] = _bd(s0_ref[0, gi], HEAD_DIM, HEAD_DIM)

    z = z_ref[0]
    rows = _iota((tb, 1), 0)
    z_prev = jnp.where(rows == 0, prev_sc[...], pltpu.roll(z, 1, 0))
    prev_sc[...] = z[tb - 1:tb, :]
    zs = z + mu_ref[...] * (z_prev - z)
    dm = D_MIX
    r = zs[:, 0:dm]
    k = zs[:, dm:2 * dm]
    v = zs[:, 2 * dm:3 * dm]
    lr = zs[:, 3 * dm:]
    nl = lr.shape[1]
    lane = _iota((1, nl), 1)
    act = jnp.where(lane < 64, jnp.tanh(lr), jnp.where(lane < 128, lr, _sigmoid(lr)))
    lo = _mm(bf(act), (wl_ref[...],))
    w = -_softplus(-(w0_ref[...] + lo[:, 0:dm])) - 0.5
    lw = -jnp.exp(w)
    a = _sigmoid(a0_ref[...] + lo[:, dm:2 * dm])
    g = lo[:, 2 * dm:3 * dm]
    bo = bo_ref[...]
    kk = k * kk_ref[...]
    k2 = k * (1.0 + (a - 1.0) * ka_ref[...])
    sums = _head_sum(jnp.concatenate([kk * kk, r * k2 * rk_ref[...]], axis=0), bo)
    kk = kk / jnp.maximum(jnp.sqrt(sums[:tb]), 1e-12)
    bonus = sums[tb:]
    am = -kk
    bm = kk * a

    ri = _iota((tb, tb), 0)
    ci_ = _iota((tb, tb), 1)
    tri = ((ci_ <= ri) & (ci_ // L == ri // L)).astype(BF16)
    c = _mm((tri,), _sp(lw, 3))
    c_last = jnp.concatenate(
        [jnp.broadcast_to(c[(j + 1) * L - 1:(j + 1) * L, :], (L, c.shape[1])) for j in range(nc)], axis=0)
    a_t = am * jnp.exp(c - lw)
    r_t = r * jnp.exp(c)
    g_inv = jnp.exp(-c)
    k_t = k2 * g_inv
    b_t = bm * g_inv
    g_end = jnp.exp(c_last - c)
    k_end = k2 * g_end
    b_end = bm * g_end
    g_last = jnp.exp(c_last)

    cw = HEADS_PER_GROUP * L
    e_tile = ((_iota((L, cw), 1) % L) == _iota((L, cw), 0)).astype(BF16)
    tmask = (_iota((GROUP, cw), 0) // HEAD_DIM) == (_iota((GROUP, cw), 1) // L)
    col_s = _iota((L, cw), 1) % L
    row_t = _iota((L, cw), 0)
    strict = col_s < row_t
    incl = col_s <= row_t
    bdmask = (_iota((GROUP, GROUP), 0) // HEAD_DIM) == (_iota((GROUP, GROUP), 1) // HEAD_DIM)
    n_dbl = int(math.log2(L))

    bd_masks = {c: (_iota((cw, 4 * c), 0) // L) == (_iota((cw, 4 * c), 1) // c) for c in {HEAD_DIM, L}}

    def bd(x, n, cols_per_head=HEAD_DIM):
        return tuple(jnp.where(bd_masks[cols_per_head], jnp.concatenate([p] * HEADS_PER_GROUP, axis=0), 0.0)
                     for p in _sp(x, n))

    eye_cat = (col_s == row_t).astype(F32)

    insts = [(cj, gi) for cj in range(nc) for gi in range(N_GROUPS)]
    rsl = {cj: slice(cj * L, (cj + 1) * L) for cj in range(nc)}
    gsl = {gi: slice(gi * GROUP, (gi + 1) * GROUP) for gi in range(N_GROUPS)}
    bk_t = {cj: _mm(bf(jnp.concatenate([b_t[rsl[cj]], k_t[rsl[cj]]], axis=1)), (e_tile,), _TN) for cj in range(nc)}
    ar = {(cj, gi): bf(jnp.concatenate([a_t[rsl[cj], gsl[gi]], r_t[rsl[cj], gsl[gi]]], axis=0)) for cj, gi in insts}
    arb = {(cj, gi): _mm(ar[cj, gi], bf(jnp.where(tmask, bk_t[cj][gsl[gi]], 0.0))) for cj, gi in insts}
    ark = {(cj, gi): _mm(ar[cj, gi], bf(jnp.where(tmask, bk_t[cj][dm + gi * GROUP:dm + (gi + 1) * GROUP], 0.0)))
           for cj, gi in insts}
    p = {k_: jnp.where(strict, arb[k_][:L], 0.0) for k_ in insts}
    m_rb = {k_: bf(jnp.where(incl, arb[k_][L:], 0.0)) for k_ in insts}
    nmv = {(cj, gi): _mm(bf(jnp.concatenate([jnp.where(strict, ark[cj, gi][:L], 0.0),
                                              jnp.where(incl, ark[cj, gi][L:], 0.0)], axis=0)),
                         bd(v[rsl[cj], gsl[gi]], 1)) for cj, gi in insts}
    tm = {k_: eye_cat + p[k_] for k_ in insts}
    for i in range(n_dbl - 1):
        for k_ in insts:
            w = bd(p[k_], 1, L)
            if i == 0:
                p[k_] = _mm(bf(p[k_]), w)
            else:
                both = _mm(bf(jnp.concatenate([p[k_], tm[k_]], axis=0)), w)
                p[k_] = both[:L]
                tm[k_] = tm[k_] + both[L:]
    tm = {k_: bf(tm[k_] + _mm(bf(tm[k_]), bd(p[k_], 1, L))) for k_ in insts}

    groups = range(N_GROUPS)
    state = [st_sc[gi] for gi in groups]
    y_rows = []
    for cj in range(nc):
        rs = rsl[cj]
        ars = [_mm(ar[cj, gi], bf(state[gi]), _NT) for gi in groups]
        u = [_mm(tm[cj, gi], bd(ars[gi][:L] + nmv[cj, gi][:L], 1)) for gi in groups]
        ys = [ars[gi][L:] + nmv[cj, gi][L:] + _mm(m_rb[cj, gi], bd(u[gi], 1)) for gi in groups]
        upd = [_mm(bf(jnp.concatenate([v[rs, gsl[gi]], u[gi]], axis=0)),
                   bf(jnp.concatenate([k_end[rs, gsl[gi]], b_end[rs, gsl[gi]]], axis=0)), _TN) for gi in groups]
        state = [state[gi] * g_last[rs, gsl[gi]][0:1] + jnp.where(bdmask, upd[gi], 0.0) for gi in groups]
        y_rows.append(jnp.concatenate(ys, axis=1))
    for gi in groups:
        st_sc[gi] = state[gi]

    y = jnp.concatenate(y_rows, axis=0) if nc > 1 else y_rows[0]
    inv = 1.0 / HEAD_DIM
    mean = _head_sum(y, bo) * inv
    dlt = y - mean
    var = _head_sum(dlt * dlt, bo) * inv
    yn = dlt * lax.rsqrt(var + GN_EPS) * lng_ref[...] + lnb_ref[...]
    y_ref[0] = (yn + bonus * v) * g

    @pl.when(t == pl.num_programs(1) - 1)
    def _():
        for gi in range(N_GROUPS):
            st_ref[0, gi] = _unbd(st_sc[gi])


def _state_spec():
    return pl.BlockSpec((1, N_GROUPS, HEAD_DIM, GROUP), lambda i, j: (i, 0, 0, 0))


def _rwkv(za, shift0, s0_cat, pr, L, nc):
    b, t, ac = za.shape
    dm = D_MIX
    tb = L * nc
    row = lambda a: a.reshape(1, -1)
    consts = [row(pr["mu"]), row(pr["w0"]), row(pr["a0"]), pr["wl"], row(pr["k_k"]), row(pr["k_a"]),
              row(pr["r_k"]), row(pr["lnx_g"]), row(pr["lnx_b"]), _block_ones(GROUP, HEAD_DIM)]
    return pl.pallas_call(
        functools.partial(_rwkv_kernel, L=L, nc=nc),
        grid=(b, t // tb),
        in_specs=[pl.BlockSpec((1, tb, ac), lambda i, j: (i, j, 0)),
                  pl.BlockSpec((1, 1, ac), lambda i, j: (i, 0, 0)), _state_spec()]
                 + [_const_spec(c.shape) for c in consts],
        out_specs=[pl.BlockSpec((1, tb, dm), lambda i, j: (i, j, 0)), _state_spec()],
        out_shape=[jax.ShapeDtypeStruct((b, t, dm), F32),
                   jax.ShapeDtypeStruct((b, N_GROUPS, HEAD_DIM, GROUP), F32)],
        scratch_shapes=[pltpu.VMEM((1, ac), F32), pltpu.VMEM((N_GROUPS, GROUP, GROUP), F32)],
        compiler_params=_params(("parallel", "arbitrary")),
        name="rwkv7",
    )(za, shift0.reshape(b, 1, ac), s0_cat, *consts)


def _hgrn_kernel(zq_ref, zf_ref, zi_ref, zg_ref, s0_ref, lb_ref, ng_ref, bo_ref, y_ref, st_ref, st_sc, *, L, nc):
    t = pl.program_id(1)
    tb = nc * L

    @pl.when(t == 0)
    def _():
        for gi in range(N_GROUPS):
            st_sc[gi] = _bd(s0_ref[0, gi], HEAD_DIM, HEAD_DIM)

    dm = D_MIX
    q = _silu(zq_ref[0])
    xf = zf_ref[0]
    v = zi_ref[0]
    lb = lb_ref[...]
    la = jnp.log(lb)
    lc = jnp.log1p(-lb) + _log_sigmoid(xf)
    logf = jnp.maximum(la, lc) + jnp.log1p(jnp.exp(-jnp.abs(la - lc)))
    k = (1.0 - lb) * _sigmoid(-xf)
    bo = bo_ref[...]

    ri = _iota((tb, tb), 0)
    ci = _iota((tb, tb), 1)
    tri = ((ci <= ri) & (ci // L == ri // L)).astype(BF16)
    b = _mm((tri,), _sp(logf, 3))
    b_last = jnp.concatenate(
        [jnp.broadcast_to(b[(j + 1) * L - 1:(j + 1) * L, :], (L, dm)) for j in range(nc)], axis=0)
    qe = q * jnp.exp(b)
    k_end = k * jnp.exp(b_last - b)
    g_last = jnp.exp(b_last)

    n_sub = L // SUB
    nst = N_HEADS * SUB
    stackmask = (_iota((nst, dm), 0) // SUB) == (_iota((nst, dm), 1) // HEAD_DIM)
    sub_rows = _iota((SUB, 1), 0)
    subs = [(cj * L, cj * L + i * SUB, i) for cj in range(nc) for i in range(n_sub)]
    rows_of = {sb: slice(sb[1], sb[1] + SUB) for sb in subs}
    brefs = {sb: jnp.zeros((1, dm), F32) if sb[2] == 0 else b[sb[1] - 1:sb[1]] for sb in subs}

    def fold_heads(ov):
        out = ov[0:SUB]
        for hh in range(1, N_HEADS):
            out = out + ov[hh * SUB:(hh + 1) * SUB]
        return out

    def stacked_q(sb):
        qh = q[rows_of[sb]] * jnp.exp(b[rows_of[sb]] - brefs[sb])
        return bf(jnp.where(stackmask, jnp.concatenate([qh] * N_HEADS, axis=0), 0.0))

    def intra_exact():
        xs = {sb: [] for sb in subs}
        for s in range(SUB):
            for sb in subs:
                bi = b[rows_of[sb]]
                e = jnp.exp(jnp.where(sub_rows >= s, bi - bi[s:s + 1], NEG))
                xs[sb].append(q[rows_of[sb]] * e * k[rows_of[sb]][s:s + 1])
        att = {sb: _head_sum(jnp.concatenate(xs[sb], axis=0), bo, terms=1) for sb in subs}
        later = [sb for sb in subs if sb[2] > 0]
        qst = {sb: stacked_q(sb) for sb in later}
        kh = {sb: bf(k[sb[0]:sb[1]] * jnp.exp(brefs[sb] - b[sb[0]:sb[1]])) for sb in later}
        att2 = {sb: _mm(qst[sb], kh[sb], _NT) for sb in later}
        ov = {sb: jnp.where(stackmask, _mm(bf(att2[sb]), bf(v[sb[0]:sb[1]])), 0.0) for sb in later}
        o_subs = []
        for sb in subs:
            vi = v[rows_of[sb]]
            oi = att[sb][0:SUB] * vi[0:1]
            for s in range(1, SUB):
                oi = oi + att[sb][s * SUB:(s + 1) * SUB] * vi[s:s + 1]
            if sb[2] > 0:
                oi = oi + fold_heads(ov[sb])
            o_subs.append(oi)
        return jnp.concatenate(o_subs, axis=0) if len(o_subs) > 1 else o_subs[0]

    def intra_factored():
        qst = {sb: stacked_q(sb) for sb in subs}
        kh = {sb: bf(k[sb[0]:sb[1] + SUB] * jnp.exp(brefs[sb] - b[sb[0]:sb[1] + SUB])) for sb in subs}
        att2 = {}
        for sb in subs:
            n_keys = sb[1] + SUB - sb[0]
            visible = _iota((nst, n_keys), 1) <= sb[2] * SUB + _iota((nst, n_keys), 0) % SUB
            att2[sb] = jnp.where(visible, _mm(qst[sb], kh[sb], _NT), 0.0)
        ov = {sb: jnp.where(stackmask, _mm(bf(att2[sb]), bf(v[sb[0]:sb[1] + SUB])), 0.0) for sb in subs}
        o_subs = [fold_heads(ov[sb]) for sb in subs]
        return jnp.concatenate(o_subs, axis=0) if len(o_subs) > 1 else o_subs[0]

    drop = jnp.concatenate([b[sb[1] + SUB - 1:sb[1] + SUB] - brefs[sb] for sb in subs], axis=0)
    o_intra = lax.cond(jnp.min(drop) >= -FACTORED_MAX_DECAY, intra_factored, intra_exact)

    groups = range(N_GROUPS)
    gsl = {gi: slice(gi * GROUP, (gi + 1) * GROUP) for gi in groups}
    bdmask = (_iota((GROUP, GROUP), 0) // HEAD_DIM) == (_iota((GROUP, GROUP), 1) // HEAD_DIM)
    upd = {(cj, gi): jnp.where(bdmask, _mm(bf(v[cj * L:(cj + 1) * L, gsl[gi]]),
                                          bf(k_end[cj * L:(cj + 1) * L, gsl[gi]]), _TN), 0.0)
           for cj in range(nc) for gi in groups}
    state = [st_sc[gi] for gi in groups]
    o_rows = []
    for cj in range(nc):
        rs = slice(cj * L, (cj + 1) * L)
        o_rows.append(jnp.concatenate([_mm(bf(qe[rs, gsl[gi]]), bf(state[gi]), _NT) for gi in groups], axis=1))
        state = [state[gi] * g_last[rs, gsl[gi]][0:1] + upd[cj, gi] for gi in groups]
    for gi in groups:
        st_sc[gi] = state[gi]
    o = o_intra + (jnp.concatenate(o_rows, axis=0) if nc > 1 else o_rows[0])

    ms = _head_sum(o * o, bo) * (1.0 / HEAD_DIM)
    y_ref[0] = o * lax.rsqrt(ms + RMS_EPS) * ng_ref[...] * _silu(zg_ref[0])

    @pl.when(t == pl.num_programs(1) - 1)
    def _():
        for gi in range(N_GROUPS):
            st_ref[0, gi] = _unbd(st_sc[gi])


def _hgrn(zq, zf, zi, zg, s0_cat, lb, norm_g, L, nc):
    b, t, dm = zq.shape
    tb = L * nc
    consts = [lb.reshape(1, dm), norm_g.reshape(1, dm), _block_ones(GROUP, HEAD_DIM)]
    row_spec = pl.BlockSpec((1, tb, dm), lambda i, j: (i, j, 0))
    return pl.pallas_call(
        functools.partial(_hgrn_kernel, L=L, nc=nc),
        grid=(b, t // tb),
        in_specs=[row_spec] * 4 + [_state_spec()] + [_const_spec(c.shape) for c in consts],
        out_specs=[row_spec, _state_spec()],
        out_shape=[jax.ShapeDtypeStruct((b, t, dm), F32),
                   jax.ShapeDtypeStruct((b, N_GROUPS, HEAD_DIM, GROUP), F32)],
        scratch_shapes=[pltpu.VMEM((N_GROUPS, GROUP, GROUP), F32)],
        compiler_params=_params(("parallel", "arbitrary")),
        name="hgrn2",
    )(zq, zf, zi, zg, s0_cat, *consts)


def _cumsum_kernel(x_ref, o_ref):
    x = x_ref[0]
    n, w = x.shape
    nblk = n // N_HEADS
    upper = (_iota((w, w), 0) <= _iota((w, w), 1)).astype(BF16)
    c = _mm(_sp(x, 3), (upper,))
    tot = jnp.broadcast_to(c[:, w - 1:w], (n, w))
    ri = _iota((n, n), 0)
    ci = _iota((n, n), 1)
    prior = ((ri // nblk == ci // nblk) & (ci < ri)).astype(BF16)
    o_ref[0] = c + _mm((prior,), _sp(tot, 3))


def _cumsum_time(logf_bht):
    b, h, t = logf_bht.shape
    w = 128
    n = h * (t // w)
    x = logf_bht.reshape(b, n, w)
    out = pl.pallas_call(
        _cumsum_kernel,
        grid=(b,),
        in_specs=[pl.BlockSpec((1, n, w), lambda i: (i, 0, 0))],
        out_specs=pl.BlockSpec((1, n, w), lambda i: (i, 0, 0)),
        out_shape=jax.ShapeDtypeStruct((b, n, w), F32),
        compiler_params=_params(("parallel",)),
        name="cumsum",
    )(x)
    return out.reshape(b, h, t)


def _fox_kernel(q_ref, k_ref, v_ref, f_ref, o_ref, kb_sc, vb_sc, *, tq):
    i = pl.program_id(2)

    @pl.when(i == 0)
    def _():
        kb_sc[...] = k_ref[0].astype(BF16)
        vb_sc[...] = v_ref[0].astype(BF16)

    q = q_ref[0] * ATTN_SCALE
    lane = _iota((1, 2 * HEAD_DIM), 1)
    causal = _iota((tq, tq), 1) <= _iota((tq, tq), 0)
    heads = range(2)
    head_lane = [(lane // HEAD_DIM) == hh for hh in heads]
    qm = [jnp.where(head_lane[hh], q, 0.0).astype(BF16) for hh in heads]

    def scores(j, hh):
        return _dot_nt(qm[hh], kb_sc[pl.ds(j * tq, tq), :]) - f_ref[0, 0, j, hh:hh + 1, :]

    def update(j, hh, s, m, acc):
        m_new = jnp.maximum(m, jnp.max(s, axis=-1, keepdims=True))
        p = jnp.exp(s - m_new).astype(BF16)
        vx = jnp.where(head_lane[hh], vb_sc[pl.ds(j * tq, tq), :], 1.0)
        return m_new, jnp.exp(m - m_new) * acc + _dot(p, vx)

    def run_blocks(first, count, diagonal_last, carry):
        ss = {(n, hh): scores(first + n, hh) for n in range(count) for hh in heads}
        if diagonal_last:
            for hh in heads:
                ss[count - 1, hh] = jnp.where(causal, ss[count - 1, hh], NEG)
        carry = list(carry)
        for n in range(count):
            for hh in heads:
                carry[hh] = update(first + n, hh, ss[n, hh], *carry[hh])
        return tuple(carry)

    init = tuple((jnp.full((tq, 1), NEG, F32), jnp.zeros((tq, 2 * HEAD_DIM), F32)) for _ in heads)
    n_full = i // FOX_GROUP
    carry = lax.fori_loop(0, n_full, lambda g, cr: run_blocks(g * FOX_GROUP, FOX_GROUP, False, cr), init)
    tails = [functools.partial(run_blocks, n_full * FOX_GROUP, r + 1, True) for r in range(FOX_GROUP)]
    (_, acc0), (_, acc1) = lax.switch(i % FOX_GROUP, tails, carry)
    outs = [acc / pltpu.roll(acc, HEAD_DIM, 1) for acc in (acc0, acc1)]
    o_ref[0] = jnp.where(head_lane[0], outs[0], outs[1])


def _fox_prompt(q, k, v, cum_bht, tq):
    b, t, dm = q.shape
    hp = dm // (2 * HEAD_DIM)
    nk = t // tq
    f = cum_bht.reshape(b, hp, 2, nk, tq).transpose(0, 1, 3, 2, 4)
    kv_spec = pl.BlockSpec((1, t, 2 * HEAD_DIM), lambda i, h, j: (i, 0, h))
    return pl.pallas_call(
        functools.partial(_fox_kernel, tq=tq),
        scratch_shapes=[pltpu.VMEM((t, 2 * HEAD_DIM), BF16), pltpu.VMEM((t, 2 * HEAD_DIM), BF16)],
        grid=(b, hp, nk),
        in_specs=[pl.BlockSpec((1, tq, 2 * HEAD_DIM), lambda i, h, j: (i, j, h)), kv_spec, kv_spec,
                  pl.BlockSpec((1, 1, nk, 2, tq), lambda i, h, j: (i, h, 0, 0, 0))],
        out_specs=pl.BlockSpec((1, tq, 2 * HEAD_DIM), lambda i, h, j: (i, j, h)),
        out_shape=jax.ShapeDtypeStruct((b, t, dm), F32),
        compiler_params=_params(("parallel", "parallel", "arbitrary")),
        name="fox_prompt",
    )(q, k, v, f)


def _band_kernel(q_ref, k_ref, v_ref, bias_ref, o_ref, *, tq):
    i = pl.program_id(2)
    q = q_ref[0] * ATTN_SCALE
    lane = _iota((1, 2 * HEAD_DIM), 1)
    n_piece = BAND // tq + 1
    kbs, vs, valid = [], [], []
    for p in range(n_piece):
        blk = i - (n_piece - 1) + p
        start = jnp.maximum(blk, 0) * tq
        kbs.append(k_ref[0, pl.ds(start, tq), :].astype(BF16))
        vs.append(v_ref[0, pl.ds(start, tq), :].astype(BF16))
        valid.append(blk >= 0)
    scores = {}
    for hh in range(2):
        qm = jnp.where((lane // HEAD_DIM) == hh, q, 0.0).astype(BF16)
        for p in range(n_piece):
            s = _dot_nt(qm, kbs[p]) + bias_ref[0, hh, :, p * tq:(p + 1) * tq]
            scores[hh, p] = jnp.where(valid[p], s, NEG)
    outs = []
    for hh in range(2):
        ss = [scores[hh, p] for p in range(n_piece)]
        m = jnp.max(ss[0], axis=-1, keepdims=True)
        for s in ss[1:]:
            m = jnp.maximum(m, jnp.max(s, axis=-1, keepdims=True))
        l = jnp.zeros((tq, 1), F32)
        acc = jnp.zeros((tq, 2 * HEAD_DIM), F32)
        for s, vb in zip(ss, vs):
            p_ = jnp.exp(s - m)
            l = l + jnp.sum(p_, axis=-1, keepdims=True)
            acc = acc + _dot(p_.astype(BF16), vb)
        outs.append(acc / l)
    o_ref[0] = jnp.where((lane // HEAD_DIM) == 0, outs[0], outs[1])


def _toeplitz_bias(rel_bias, nq, nk, offset):
    h = rel_bias.shape[0]
    m = jnp.arange(nq + nk - 1)
    u = rel_bias[:, jnp.clip(offset + nq - 1 - m, -REL_CLIP, REL_CLIP) + REL_CLIP].astype(F32)
    period = nq + nk
    up = jnp.pad(u, ((0, 0), (0, 1)))
    skew = jnp.tile(up, (1, nq))[:, :nq * (period - 1)].reshape(h, nq, period - 1)
    return skew[:, :, nq - 1:]


def _band_bias_prompt(rel_bias, tq):
    nk = BAND + tq
    bias = _toeplitz_bias(rel_bias, tq, nk, BAND)
    qc = jnp.arange(tq)[:, None] // CHUNK
    kc = jnp.arange(nk)[None, :] // CHUNK
    valid = (kc >= qc) & (kc <= qc + BAND // CHUNK)
    return jnp.where(valid[None], bias, NEG)


def _band_prompt(q, k, v, rel_bias, tq):
    b, t, dm = q.shape
    hp = dm // (2 * HEAD_DIM)
    bias = _band_bias_prompt(rel_bias, tq).reshape(hp, 2, tq, BAND + tq)
    kv_spec = pl.BlockSpec((1, t, 2 * HEAD_DIM), lambda i, h, j: (i, 0, h))
    return pl.pallas_call(
        functools.partial(_band_kernel, tq=tq),
        grid=(b, hp, t // tq),
        in_specs=[pl.BlockSpec((1, tq, 2 * HEAD_DIM), lambda i, h, j: (i, j, h)), kv_spec, kv_spec,
                  pl.BlockSpec((1, 2, tq, BAND + tq), lambda i, h, j: (h, 0, 0, 0))],
        out_specs=pl.BlockSpec((1, tq, 2 * HEAD_DIM), lambda i, h, j: (i, j, h)),
        out_shape=jax.ShapeDtypeStruct((b, t, dm), F32),
        compiler_params=_params(("parallel", "parallel", "arbitrary")),
        name="band_prompt",
    )(q, k, v, bias)


def _cached_attn_kernel(*refs, fox):
    if fox:
        q_ref, kn_ref, vn_ref, ck_ref, cv_ref, fc_ref, fn_ref, o_ref = refs
    else:
        q_ref, kn_ref, vn_ref, ck_ref, cv_ref, bc_ref, bn_ref, o_ref = refs
    tn, dm = q_ref.shape[1], q_ref.shape[2]
    nst = N_HEADS * tn
    stackmask = (_iota((nst, dm), 0) // tn) == (_iota((nst, dm), 1) // HEAD_DIM)
    q = q_ref[0] * ATTN_SCALE
    qst = jnp.where(stackmask, jnp.concatenate([q] * N_HEADS, axis=0), 0.0).astype(BF16)
    s_c = _dot_nt(qst, ck_ref[0].astype(BF16))
    s_n = _dot_nt(qst, kn_ref[0].astype(BF16))
    if fox:
        expand = ((_iota((nst, N_HEADS), 0) // tn) == _iota((nst, N_HEADS), 1)).astype(BF16)
        fc = fc_ref[0]
        upper = (_iota((tn, tn), 0) <= _iota((tn, tn), 1)).astype(BF16)
        fnew = fc[:, fc.shape[1] - 1:] + _mm(_sp(fn_ref[0], 3), (upper,))
        s_c = s_c - _mm((expand,), _sp(fc, 3))
        s_n = s_n - _mm((expand,), _sp(fnew, 3))
        tq = _iota((nst, tn), 0) % tn
        s_n = jnp.where(_iota((nst, tn), 1) <= tq, s_n, NEG)
    else:
        s_c = s_c + bc_ref[...]
        s_n = s_n + bn_ref[...]
    m = jnp.maximum(jnp.max(s_c, axis=-1, keepdims=True), jnp.max(s_n, axis=-1, keepdims=True))
    p_c = jnp.exp(s_c - m)
    p_n = jnp.exp(s_n - m)
    l = jnp.sum(p_c, axis=-1, keepdims=True) + jnp.sum(p_n, axis=-1, keepdims=True)
    ov = _dot(p_c.astype(BF16), cv_ref[0].astype(BF16)) + _dot(p_n.astype(BF16), vn_ref[0].astype(BF16))
    ov = jnp.where(stackmask, ov / l, 0.0)
    out = ov[0:tn]
    for hh in range(1, N_HEADS):
        out = out + ov[hh * tn:(hh + 1) * tn]
    o_ref[0] = out


def _cached_attn(q, kn, vn, ck, cv, extra_c, extra_n, fox):
    b, tn, dm = q.shape
    p = ck.shape[1]
    new_spec = pl.BlockSpec((1, tn, dm), lambda i: (i, 0, 0))
    cache_spec = pl.BlockSpec((1, p, dm), lambda i: (i, 0, 0))
    if fox:
        ex_specs = [pl.BlockSpec((1, N_HEADS, p), lambda i: (i, 0, 0)),
                    pl.BlockSpec((1, N_HEADS, tn), lambda i: (i, 0, 0))]
    else:
        ex_specs = [_const_spec(extra_c.shape), _const_spec(extra_n.shape)]
    return pl.pallas_call(
        functools.partial(_cached_attn_kernel, fox=fox),
        grid=(b,),
        in_specs=[new_spec, new_spec, new_spec, cache_spec, cache_spec] + ex_specs,
        out_specs=new_spec,
        out_shape=jax.ShapeDtypeStruct((b, tn, dm), F32),
        compiler_params=_params(("parallel",)),
        name="fox_sample" if fox else "band_sample",
    )(q, kn, vn, ck, cv, extra_c, extra_n)


def _to_cat(s):
    b = s.shape[0]
    s5 = s.reshape(b, N_GROUPS, HEADS_PER_GROUP, HEAD_DIM, HEAD_DIM)
    return s5.transpose(0, 1, 3, 2, 4).reshape(b, N_GROUPS, HEAD_DIM, GROUP)


def _from_cat(c):
    b = c.shape[0]
    c5 = c.reshape(b, N_GROUPS, HEAD_DIM, HEADS_PER_GROUP, HEAD_DIM)
    return c5.transpose(0, 1, 3, 2, 4).reshape(b, N_HEADS, HEAD_DIM, HEAD_DIM)


def _trunk(x, mods, P, cache, tm, tq, L, nc):
    b, t, d = x.shape
    dm = D_MIX
    sample = cache is not None
    new = {}
    za, q, k, v, logf = _proj(x, mods[0], P["norm_mix_g"][0], P["ab_w"], (P["a_cols"], dm, dm, dm), tm,
                              wf=P["ab_wf"], bf=P["fox_bf"])
    ac = P["a_cols"]
    if sample:
        nb, tn = cache["nb"], cache["tn"]
        za_b = za.reshape(nb, tn, ac)
        shift0 = cache["rwkv_shift"][0]
        s0 = _to_cat(cache["rwkv"][0])
    else:
        nb, tn = b, t
        za_b = za
        shift0 = jnp.zeros((nb, ac), F32)
        s0 = jnp.zeros((nb, N_GROUPS, HEAD_DIM, GROUP), F32)
    ya, st = _rwkv(za_b, shift0, s0, P["rwkv"], L, nc)
    new["rwkv"] = _from_cat(st)[None]
    new["rwkv_shift"] = za_b[:, -1][None]
    qb, kb, vb = (a.reshape(nb, tn, dm) for a in (q, k, v))
    logf_b = logf.reshape(nb, tn, N_HEADS)
    logf_t = jnp.swapaxes(logf_b, 1, 2)
    if sample:
        ck = cache["fox_k"][0].reshape(nb, -1, dm)
        cv = cache["fox_v"][0].reshape(nb, -1, dm)
        fc = _cumsum_time(jnp.swapaxes(cache["fox_logf"][0], 1, 2))
        yb = _cached_attn(qb, kb, vb, ck, cv, fc, logf_t, fox=True)
    else:
        yb = _fox_prompt(qb, kb, vb, _cumsum_time(logf_t), tq)
    new["fox_k"] = kb.reshape(1, nb, tn, N_HEADS, HEAD_DIM)
    new["fox_v"] = vb.reshape(1, nb, tn, N_HEADS, HEAD_DIM)
    new["fox_logf"] = logf_b[None]
    fix = None
    if sample:
        buf = cache["ffn_conv"][0]
        zero = jnp.zeros((nb, tn - 2, buf.shape[-1]), F32)
        fix = (jnp.concatenate([buf[:, 1:2], buf[:, 0:1] * 0, zero], axis=1).reshape(1, nb * tn, -1),
               jnp.concatenate([buf, zero], axis=1).reshape(1, nb * tn, -1))
    res = _tail(x, ya.reshape(b, t, dm), yb.reshape(b, t, dm), mods[0], mods[1], P["ab_wo"],
                P["norm_ffn_g"][0], P["wup"][0], P["conv_w"][0], P["conv_b"][0], P["wdn"][0],
                P["final_g"], False, tm, fix=fix, period=tn)
    x = res[0]
    conv0 = res[1].reshape(nb, tn, -1)[:, tn - 2:] if sample else res[1]
    q, k, v, zq, zf, zi, zg = _proj(x, mods[2], P["norm_mix_g"][1], P["cd_w"], (dm,) * 7, tm)
    qb, kb, vb, zq, zf, zi, zg = (a.reshape(nb, tn, dm) for a in (q, k, v, zq, zf, zi, zg))
    if sample:
        ck = cache["chunk_k"][0].reshape(nb, -1, dm)
        cv = cache["chunk_v"][0].reshape(nb, -1, dm)
        yc = _cached_attn(qb, kb, vb, ck, cv, P["band_bias_c"], P["band_bias_n"], fox=False)
        new["chunk_k"] = kb.reshape(1, nb, tn, N_HEADS, HEAD_DIM)
        new["chunk_v"] = vb.reshape(1, nb, tn, N_HEADS, HEAD_DIM)
        s0 = _to_cat(jnp.swapaxes(cache["hgrn"][0], -1, -2))
    else:
        yc = _band_prompt(qb, kb, vb, P["rel_bias"], 4 * CHUNK)
        keep = min(BAND, tn)
        new["chunk_k"] = kb[:, tn - keep:].reshape(1, nb, keep, N_HEADS, HEAD_DIM)
        new["chunk_v"] = vb[:, tn - keep:].reshape(1, nb, keep, N_HEADS, HEAD_DIM)
        s0 = jnp.zeros((nb, N_GROUPS, HEAD_DIM, GROUP), F32)
    yd, st = _hgrn(zq, zf, zi, zg, s0, P["hgrn_lb"], P["hgrn_norm_g"], L, nc)
    new["hgrn"] = jnp.swapaxes(_from_cat(st), -1, -2)[None]
    if sample:
        buf = cache["ffn_conv"][1]
        fix = (jnp.concatenate([buf[:, 1:2], buf[:, 0:1] * 0, zero], axis=1).reshape(1, nb * tn, -1),
               jnp.concatenate([buf, zero], axis=1).reshape(1, nb * tn, -1))
    res = _tail(x, yc.reshape(b, t, dm), yd.reshape(b, t, dm), mods[2], mods[3], P["cd_wo"],
                P["norm_ffn_g"][1], P["wup"][1], P["conv_w"][1], P["conv_b"][1], P["wdn"][1],
                P["final_g"], True, tm, fix=fix, period=tn)
    conv1 = res[1].reshape(nb, tn, -1)[:, tn - 2:] if sample else res[1]
    new["ffn_conv"] = jnp.stack([conv0, conv1])
    y = res[0].reshape(nb, tn, d)
    return y, new


def kernel(x_prompt, x_sample, c_prompt, c_sample, cache_fox_k, cache_fox_v, cache_fox_logf, state_rwkv,
           state_rwkv_shift, cache_chunk_k, cache_chunk_v, state_hgrn, state_ffn_conv, ada_w, ada_b,
           norm_mix_g, norm_ffn_g, ab_w_in, rwkv_mu, rwkv_w0, rwkv_w2, rwkv_a0, rwkv_a2, rwkv_g2,
           rwkv_k_k, rwkv_k_a, rwkv_r_k, rwkv_lnx_g, rwkv_lnx_b, fox_b_f, ab_w_out, cd_w_in,
           chunk_rel_bias, hgrn_lb_table, hgrn_norm_g, cd_w_out, ffn_w_up, ffn_conv_w, ffn_conv_b,
           ffn_w_down, final_norm_g):
    bp, tp, d = x_prompt.shape
    bs, ts, _ = x_sample.shape
    depth = ada_w.shape[0]
    dm = D_MIX
    a_cols = rwkv_mu.shape[1]
    n_lw, n_la, n_lg = rwkv_w2.shape[1], rwkv_a2.shape[1], rwkv_g2.shape[1]

    c_all = jnp.concatenate([c_prompt, c_sample], axis=0)
    mods = _adaln(c_all, ada_w.reshape(depth * 2, d, 3 * d), ada_b.reshape(depth * 2, 3 * d))
    mods_p = [mods[i, :bp].reshape(bp, 1, 3 * d) for i in range(depth * 2)]
    mods_s = [jnp.repeat(mods[i, bp:], ts, axis=0).reshape(1, bs * ts, 3 * d) for i in range(depth * 2)]

    wl = jnp.zeros((n_lw + n_la + n_lg, 3 * dm), F32)
    wl = wl.at[:n_lw, 0:dm].set(rwkv_w2[0])
    wl = wl.at[n_lw:n_lw + n_la, dm:2 * dm].set(rwkv_a2[0])
    wl = wl.at[n_lw + n_la:, 2 * dm:].set(rwkv_g2[0])
    sm = jax.nn.softmax(hgrn_lb_table.astype(F32), axis=0)
    lb = (jnp.cumsum(sm, axis=0) - sm[0])[1]
    wf = jnp.zeros((d, 128), F32).at[:, :N_HEADS].set(ab_w_in[0][:, a_cols + 3 * dm:]).astype(BF16)
    bf = jnp.zeros((1, 128), F32).at[0, :N_HEADS].set(fox_b_f[0])
    p_c = cache_chunk_k.shape[2]
    bias_s = _toeplitz_bias(chunk_rel_bias[0], ts, p_c + ts, p_c).reshape(N_HEADS * ts, p_c + ts)

    P = {
        "a_cols": a_cols,
        "norm_mix_g": norm_mix_g, "norm_ffn_g": norm_ffn_g, "final_g": final_norm_g,
        "ab_w": ab_w_in[0][:, :a_cols + 3 * dm].astype(BF16), "ab_wf": wf, "fox_bf": bf,
        "ab_wo": ab_w_out[0].astype(BF16),
        "cd_w": cd_w_in[0].astype(BF16), "cd_wo": cd_w_out[0].astype(BF16),
        "wup": [ffn_w_up[i].astype(BF16) for i in range(depth)],
        "wdn": [ffn_w_down[i].astype(BF16) for i in range(depth)],
        "conv_w": ffn_conv_w, "conv_b": ffn_conv_b,
        "rwkv": {"mu": rwkv_mu[0], "w0": rwkv_w0[0], "a0": rwkv_a0[0], "wl": wl.astype(BF16), "k_k": rwkv_k_k[0],
                 "k_a": rwkv_k_a[0], "r_k": rwkv_r_k[0], "lnx_g": rwkv_lnx_g[0], "lnx_b": rwkv_lnx_b[0]},
        "rel_bias": chunk_rel_bias[0], "band_bias_c": bias_s[:, :p_c], "band_bias_n": bias_s[:, p_c:],
        "hgrn_lb": lb, "hgrn_norm_g": hgrn_norm_g[0],
    }
    cache = {"nb": bs, "tn": ts, "fox_k": cache_fox_k, "fox_v": cache_fox_v, "fox_logf": cache_fox_logf,
             "rwkv": state_rwkv, "rwkv_shift": state_rwkv_shift, "chunk_k": cache_chunk_k,
             "chunk_v": cache_chunk_v, "hgrn": state_hgrn, "ffn_conv": state_ffn_conv}

    y_p, sp = _trunk(x_prompt, mods_p, P, None, tm=512, tq=512, L=CHUNK, nc=4)
    y_s, ss = _trunk(x_sample.reshape(1, bs * ts, d), mods_s, P, cache, tm=bs * ts, tq=None, L=ts, nc=1)
    names = ("fox_k", "fox_v", "fox_logf", "rwkv", "rwkv_shift", "chunk_k", "chunk_v", "hgrn", "ffn_conv")
    return (y_p, y_s) + tuple(sp[n] for n in names) + tuple(ss[n] for n in names)
```

```python
import functools
import math

import jax
import jax.numpy as jnp
from jax import lax
from jax.experimental import pallas as pl
from jax.experimental.pallas import tpu as pltpu

F32 = jnp.float32
BF16 = jnp.bfloat16

HEAD_DIM = 64
N_HEADS = 8
D_MIX = N_HEADS * HEAD_DIM
GROUP = 256
N_GROUPS = D_MIX // GROUP
HEADS_PER_GROUP = GROUP // HEAD_DIM
SUB = 16
CHUNK = 64
BAND = 512
REL_CLIP = 128
RMS_EPS = 1e-6
GN_EPS = 64e-5
ATTN_SCALE = HEAD_DIM ** -0.5
NEG = -1e30
FACTORED_MAX_DECAY = 60.0
FOX_GROUP = 4
UP_LOOKAHEAD = 4
VMEM_LIMIT = 56 * 1024 * 1024


def _dot(a, b):
    return jnp.dot(a, b, preferred_element_type=F32)


def _dot_nt(a, b):
    return lax.dot_general(a, b, (((1,), (1,)), ((), ())), preferred_element_type=F32)


_NN = ((1,), (0,))
_NT = ((1,), (1,))
_TN = ((0,), (0,))


def _sp(x, n):
    hi = x.astype(BF16)
    if n == 1:
        return (hi,)
    r = x - hi.astype(F32)
    mid = r.astype(BF16)
    if n == 2:
        return (hi, mid)
    return (hi, mid, (r - mid.astype(F32)).astype(BF16))


def bf(x):
    return (x.astype(BF16),)


def _mm(a, b, dims=_NN):
    n = max(len(a), len(b))
    out = None
    for i, ai in enumerate(a):
        for j, bj in enumerate(b):
            if i + j < n:
                d = lax.dot_general(ai, bj, (dims, ((), ())), preferred_element_type=F32)
                out = d if out is None else out + d
    return out


def _head_sum(x, bo_group, terms=2):
    return jnp.concatenate(
        [_mm(_sp(x[:, gi * GROUP:(gi + 1) * GROUP], terms), (bo_group,)) for gi in range(N_GROUPS)], axis=1)


def _sigmoid(x):
    return jax.nn.sigmoid(x)


def _silu(x):
    return x * jax.nn.sigmoid(x)


def _softplus(x):
    return jnp.maximum(x, 0.0) + jnp.log1p(jnp.exp(-jnp.abs(x)))


def _log_sigmoid(x):
    return -_softplus(-x)


def _iota(shape, dim):
    return lax.broadcasted_iota(jnp.int32, shape, dim)


def _const_spec(shape):
    nd = len(shape)
    return pl.BlockSpec(shape, lambda *_: (0,) * nd, pipeline_mode=pl.Buffered(1))


def _params(sem):
    return pltpu.CompilerParams(dimension_semantics=sem, vmem_limit_bytes=VMEM_LIMIT)


def _norm_mod(x, g, shift, scale):
    xn = x * lax.rsqrt(jnp.mean(x * x, axis=-1, keepdims=True) + RMS_EPS) * g
    return xn * (1.0 + scale) + shift


def _block_ones(n, blk):
    i = jnp.arange(n) // blk
    return (i[:, None] == i[None, :]).astype(BF16)


def _bd(x, rows_per_head, cols_per_head):
    n = HEADS_PER_GROUP
    t = jnp.concatenate([x] * n, axis=0)
    r = _iota(t.shape, 0) // rows_per_head
    c = _iota(t.shape, 1) // cols_per_head
    return jnp.where(r == c, t, 0.0)


def _unbd(m):
    out = m[0:HEAD_DIM]
    for h in range(1, HEADS_PER_GROUP):
        out = out + m[h * HEAD_DIM:(h + 1) * HEAD_DIM]
    return out


def _adaln_kernel(c_ref, w_ref, b_ref, o_ref):
    o_ref[0] = _mm(_sp(_silu(c_ref[...]), 2), bf(w_ref[0])) + b_ref[0]


def _adaln(c_all, ada_w, ada_b):
    n, d, d3 = ada_w.shape
    bt = c_all.shape[0]
    nt = d3 // d
    return pl.pallas_call(
        _adaln_kernel,
        grid=(n, nt),
        in_specs=[pl.BlockSpec((bt, d), lambda i, j: (0, 0)),
                  pl.BlockSpec((1, d, d), lambda i, j: (i, 0, j)),
                  pl.BlockSpec((1, 1, d), lambda i, j: (i, 0, j))],
        out_specs=pl.BlockSpec((1, bt, d), lambda i, j: (i, 0, j)),
        out_shape=jax.ShapeDtypeStruct((n, bt, d3), F32),
        compiler_params=_params(("parallel", "parallel")),
        name="adaln",
    )(c_all, ada_w, ada_b.reshape(n, 1, d3))


def _proj_kernel(*refs, splits, fox):
    x_ref, mod_ref, g_ref, w_ref = refs[:4]
    pos = 4
    if fox:
        wf_ref, bf_ref = refs[4:6]
        pos = 6
    outs = refs[pos:]
    d = x_ref.shape[-1]
    mod = mod_ref[0]
    h = _norm_mod(x_ref[0], g_ref[...], mod[:, :d], mod[:, d:2 * d]).astype(BF16)
    off = 0
    for o_ref, n in zip(outs, splits):
        o_ref[0] = _dot(h, w_ref[:, off:off + n])
        off += n
    if fox:
        zf = _dot(h, wf_ref[...]) + bf_ref[...]
        outs[len(splits)][0] = _log_sigmoid(zf)[:, :N_HEADS]


def _proj(x, mod, g, w_bf, splits, tm, wf=None, bf=None):
    b, t, d = x.shape
    r = mod.shape[1]
    fox = wf is not None
    mod_spec = (pl.BlockSpec((1, 1, 3 * d), lambda i, j: (i, 0, 0)) if r == 1
                else pl.BlockSpec((1, tm, 3 * d), lambda i, j: (i, j, 0)))
    in_specs = [pl.BlockSpec((1, tm, d), lambda i, j: (i, j, 0)), mod_spec,
                _const_spec((1, d)), _const_spec(w_bf.shape)]
    args = [x, mod, g.reshape(1, d), w_bf]
    widths = list(splits)
    if fox:
        in_specs += [_const_spec(wf.shape), _const_spec(bf.shape)]
        args += [wf, bf]
        widths.append(N_HEADS)
    return pl.pallas_call(
        functools.partial(_proj_kernel, splits=tuple(splits), fox=fox),
        grid=(b, t // tm),
        in_specs=in_specs,
        out_specs=[pl.BlockSpec((1, tm, n), lambda i, j: (i, j, 0)) for n in widths],
        out_shape=[jax.ShapeDtypeStruct((b, t, n), F32) for n in widths],
        compiler_params=_params(("parallel", "parallel")),
        name="proj",
    )(*args)


def _tail_kernel(*refs, sample, final, period, cb, down_group):
    (x_ref, ya_ref, yb_ref, m1_ref, m2_ref, wo_ref, g_ref, wup_ref, cw_ref, cbias_ref,
     wdn_ref, fg_ref) = refs[:12]
    if sample:
        fix1_ref, fix2_ref, o_ref, u_ref, g_sc = refs[12:]
    else:
        o_ref, cs_ref, g_sc, tail_sc = refs[12:]
    tm, d = x_ref.shape[1], x_ref.shape[2]
    ff = wdn_ref.shape[0]
    dm = ya_ref.shape[2]
    m1 = m1_ref[0]
    m2 = m2_ref[0]
    ymix = (_dot(ya_ref[0].astype(BF16), wo_ref[0:dm, :])
            + _dot(yb_ref[0].astype(BF16), wo_ref[dm:2 * dm, :]))
    x1 = x_ref[0] + m1[:, 2 * d:3 * d] * ymix
    h = _norm_mod(x1, g_ref[...], m2[:, :d], m2[:, d:2 * d]).astype(BF16)
    rows = _iota((tm, 1), 0)
    if sample:
        tpos = rows & (period - 1)
    else:
        t = pl.program_id(1)

        @pl.when(t == 0)
        def _():
            tail_sc[...] = jnp.zeros_like(tail_sc)

    def up(j):
        return [_dot(h, wup_ref[:, half * ff + j * cb:half * ff + (j + 1) * cb]) for half in range(2)]

    def conv(u, u1, u2, c0):
        cw = cw_ref[:, c0:c0 + cb]
        return cbias_ref[:, c0:c0 + cb] + cw[0:1] * u2 + cw[1:2] * u1 + cw[2:3] * u

    def gated(j, us):
        halves = []
        for half, u in enumerate(us):
            c0 = half * ff + j * cb
            if sample:
                u1 = jnp.where(tpos == 0, fix1_ref[0, :, c0:c0 + cb], pltpu.roll(u, 1, 0))
                u2 = jnp.where(tpos < 2, fix2_ref[0, :, c0:c0 + cb], pltpu.roll(u, 2, 0))
                u_ref[0, :, c0:c0 + cb] = u
                halves.append(conv(u, u1, u2, c0))
            else:
                edge = jnp.concatenate([tail_sc[:, c0:c0 + cb], u[0:8]], axis=0)
                first = conv(edge, pltpu.roll(edge, 1, 0), pltpu.roll(edge, 2, 0), c0)[8:16]
                tail_sc[:, c0:c0 + cb] = u[tm - 8:tm, :]
                main = conv(u, pltpu.roll(u, 1, 0), pltpu.roll(u, 2, 0), c0)
                halves.append(jnp.concatenate([first, main[8:]], axis=0))
        return (_silu(halves[0]) * halves[1]).astype(BF16)

    acc = None
    n_blk = ff // cb
    pending = [up(j) for j in range(min(UP_LOOKAHEAD, n_blk))]
    j0 = 0
    for j in range(n_blk):
        if j + UP_LOOKAHEAD < n_blk:
            pending.append(up(j + UP_LOOKAHEAD))
        g_sc[:, j * cb:(j + 1) * cb] = gated(j, pending.pop(0))
        if (j + 1 - j0) == down_group or j + 1 == n_blk:
            part = _dot(g_sc[:, j0 * cb:(j + 1) * cb], wdn_ref[j0 * cb:(j + 1) * cb, :])
            acc = part if acc is None else acc + part
            j0 = j + 1
    out = x1 + m2[:, 2 * d:3 * d] * acc
    if final:
        out = out * lax.rsqrt(jnp.mean(out * out, axis=-1, keepdims=True) + RMS_EPS) * fg_ref[...]
    o_ref[0] = out
    if not sample:
        @pl.when(t == pl.num_programs(1) - 1)
        def _():
            cs_ref[0] = tail_sc[6:8, :]


def _tail(x, ya, yb, mod1, mod2, wo_bf, g, wup_bf, conv_w, conv_b, wdn_bf, fg, final, tm,
          fix=None, period=None):
    b, t, d = x.shape
    ff = wdn_bf.shape[0]
    dm = ya.shape[2]
    sample = fix is not None
    r = mod1.shape[1]
    mod_spec = (pl.BlockSpec((1, 1, 3 * d), lambda i, j: (i, 0, 0)) if r == 1
                else pl.BlockSpec((1, tm, 3 * d), lambda i, j: (i, j, 0)))
    row_spec = lambda n: pl.BlockSpec((1, tm, n), lambda i, j: (i, j, 0))
    in_specs = [row_spec(d), row_spec(dm), row_spec(dm), mod_spec, mod_spec,
                _const_spec(wo_bf.shape), _const_spec((1, d)), _const_spec(wup_bf.shape),
                _const_spec(conv_w.shape), _const_spec((1, 2 * ff)), _const_spec(wdn_bf.shape),
                _const_spec((1, d))]
    args = [x, ya, yb, mod1, mod2, wo_bf, g.reshape(1, d), wup_bf, conv_w, conv_b.reshape(1, 2 * ff),
            wdn_bf, fg.reshape(1, d)]
    if sample:
        in_specs += [row_spec(2 * ff), row_spec(2 * ff)]
        args += list(fix)
        out_specs = [row_spec(d), row_spec(2 * ff)]
        out_shape = [jax.ShapeDtypeStruct((b, t, d), F32), jax.ShapeDtypeStruct((b, t, 2 * ff), F32)]
        scratch = [pltpu.VMEM((tm, ff), BF16)]
    else:
        out_specs = [row_spec(d), pl.BlockSpec((1, 2, 2 * ff), lambda i, j: (i, 0, 0))]
        out_shape = [jax.ShapeDtypeStruct((b, t, d), F32), jax.ShapeDtypeStruct((b, 2, 2 * ff), F32)]
        scratch = [pltpu.VMEM((tm, ff), BF16), pltpu.VMEM((8, 2 * ff), F32)]
    return pl.pallas_call(
        functools.partial(_tail_kernel, sample=sample, final=final, period=period, cb=256, down_group=4),
        grid=(b, t // tm),
        in_specs=in_specs,
        out_specs=out_specs,
        out_shape=out_shape,
        scratch_shapes=scratch,
        compiler_params=_params(("parallel", "arbitrary")),
        name="tail",
    )(*args)


def _rwkv_kernel(z_ref, shift0_ref, s0_ref, mu_ref, w0_ref, a0_ref, wl_ref, kk_ref, ka_ref, rk_ref,
                 lng_ref, lnb_ref, bo_ref, y_ref, st_ref, prev_sc, st_sc, *, L, nc):
    t = pl.program_id(1)
    tb = nc * L

    @pl.when(t == 0)
    def _():
        prev_sc[...] = shift0_ref[0]
        for gi in range(N_GROUPS):
            st_sc[gi] = _bd(s0_ref[0, gi], HEAD_DIM, HEAD_DIM)

    z = z_ref[0]
    rows = _iota((tb, 1), 0)
    z_prev = jnp.where(rows == 0, prev_sc[...], pltpu.roll(z, 1, 0))
    prev_sc[...] = z[tb - 1:tb, :]
    zs = z + mu_ref[...] * (z_prev - z)
    dm = D_MIX
    r = zs[:, 0:dm]
    k = zs[:, dm:2 * dm]
    v = zs[:, 2 * dm:3 * dm]
    lr = zs[:, 3 * dm:]
    nl = lr.shape[1]
    lane = _iota((1, nl), 1)
    act = jnp.where(lane < 64, jnp.tanh(lr), jnp.where(lane < 128, lr, _sigmoid(lr)))
    lo = _mm(bf(act), (wl_ref[...],))
    w = -_softplus(-(w0_ref[...] + lo[:, 0:dm])) - 0.5
    lw = -jnp.exp(w)
    a = _sigmoid(a0_ref[...] + lo[:, dm:2 * dm])
    g = lo[:, 2 * dm:3 * dm]
    bo = bo_ref[...]
    kk = k * kk_ref[...]
    k2 = k * (1.0 + (a - 1.0) * ka_ref[...])
    sums = _head_sum(jnp.concatenate([kk * kk, r * k2 * rk_ref[...]], axis=0), bo)
    kk = kk / jnp.maximum(jnp.sqrt(sums[:tb]), 1e-12)
    bonus = sums[tb:]
    am = -kk
    bm = kk * a

    ri = _iota((tb, tb), 0)
    ci_ = _iota((tb, tb), 1)
    tri = ((ci_ <= ri) & (ci_ // L == ri // L)).astype(BF16)
    c = _mm((tri,), _sp(lw, 3))
    c_last = jnp.concatenate(
        [jnp.broadcast_to(c[(j + 1) * L - 1:(j + 1) * L, :], (L, c.shape[1])) for j in range(nc)], axis=0)
    a_t = am * jnp.exp(c - lw)
    r_t = r * jnp.exp(c)
    g_inv = jnp.exp(-c)
    k_t = k2 * g_inv
    b_t = bm * g_inv
    g_end = jnp.exp(c_last - c)
    k_end = k2 * g_end
    b_end = bm * g_end
    g_last = jnp.exp(c_last)

    cw = HEADS_PER_GROUP * L
    e_tile = ((_iota((L, cw), 1) % L) == _iota((L, cw), 0)).astype(BF16)
    tmask = (_iota((GROUP, cw), 0) // HEAD_DIM) == (_iota((GROUP, cw), 1) // L)
    col_s = _iota((L, cw), 1) % L
    row_t = _iota((L, cw), 0)
    strict = col_s < row_t
    incl = col_s <= row_t
    bdmask = (_iota((GROUP, GROUP), 0) // HEAD_DIM) == (_iota((GROUP, GROUP), 1) // HEAD_DIM)
    n_dbl = int(math.log2(L))

    bd_masks = {c: (_iota((cw, 4 * c), 0) // L) == (_iota((cw, 4 * c), 1) // c) for c in {HEAD_DIM, L}}

    def bd(x, n, cols_per_head=HEAD_DIM):
        return tuple(jnp.where(bd_masks[cols_per_head], jnp.concatenate([p] * HEADS_PER_GROUP, axis=0), 0.0)
                     for p in _sp(x, n))

    eye_cat = (col_s == row_t).astype(F32)

    insts = [(cj, gi) for cj in range(nc) for gi in range(N_GROUPS)]
    rsl = {cj: slice(cj * L, (cj + 1) * L) for cj in range(nc)}
    gsl = {gi: slice(gi * GROUP, (gi + 1) * GROUP) for gi in range(N_GROUPS)}
    bk_t = {cj: _mm(bf(jnp.concatenate([b_t[rsl[cj]], k_t[rsl[cj]]], axis=1)), (e_tile,), _TN) for cj in range(nc)}
    ar = {(cj, gi): bf(jnp.concatenate([a_t[rsl[cj], gsl[gi]], r_t[rsl[cj], gsl[gi]]], axis=0)) for cj, gi in insts}
    arb = {(cj, gi): _mm(ar[cj, gi], bf(jnp.where(tmask, bk_t[cj][gsl[gi]], 0.0))) for cj, gi in insts}
    ark = {(cj, gi): _mm(ar[cj, gi], bf(jnp.where(tmask, bk_t[cj][dm + gi * GROUP:dm + (gi + 1) * GROUP], 0.0)))
           for cj, gi in insts}
    p = {k_: jnp.where(strict, arb[k_][:L], 0.0) for k_ in insts}
    m_rb = {k_: bf(jnp.where(incl, arb[k_][L:], 0.0)) for k_ in insts}
    nmv = {(cj, gi): _mm(bf(jnp.concatenate([jnp.where(strict, ark[cj, gi][:L], 0.0),
                                              jnp.where(incl, ark[cj, gi][L:], 0.0)], axis=0)),
                         bd(v[rsl[cj], gsl[gi]], 1)) for cj, gi in insts}
    tm = {k_: eye_cat + p[k_] for k_ in insts}
    for i in range(n_dbl - 1):
        for k_ in insts:
            w = bd(p[k_], 1, L)
            if i == 0:
                p[k_] = _mm(bf(p[k_]), w)
            else:
                both = _mm(bf(jnp.concatenate([p[k_], tm[k_]], axis=0)), w)
                p[k_] = both[:L]
                tm[k_] = tm[k_] + both[L:]
    tm = {k_: bf(tm[k_] + _mm(bf(tm[k_]), bd(p[k_], 1, L))) for k_ in insts}

    groups = range(N_GROUPS)
    state = [st_sc[gi] for gi in groups]
    y_rows = []
    for cj in range(nc):
        rs = rsl[cj]
        ars = [_mm(ar[cj, gi], bf(state[gi]), _NT) for gi in groups]
        u = [_mm(tm[cj, gi], bd(ars[gi][:L] + nmv[cj, gi][:L], 1)) for gi in groups]
        ys = [ars[gi][L:] + nmv[cj, gi][L:] + _mm(m_rb[cj, gi], bd(u[gi], 1)) for gi in groups]
        upd = [_mm(bf(jnp.concatenate([v[rs, gsl[gi]], u[gi]], axis=0)),
                   bf(jnp.concatenate([k_end[rs, gsl[gi]], b_end[rs, gsl[gi]]], axis=0)), _TN) for gi in groups]
        state = [state[gi] * g_last[rs, gsl[gi]][0:1] + jnp.where(bdmask, upd[gi], 0.0) for gi in groups]
        y_rows.append(jnp.concatenate(ys, axis=1))
    for gi in groups:
        st_sc[gi] = state[gi]

    y = jnp.concatenate(y_rows, axis=0) if nc > 1 else y_rows[0]
    inv = 1.0 / HEAD_DIM
    mean = _head_sum(y, bo) * inv
    dlt = y - mean
    var = _head_sum(dlt * dlt, bo) * inv
    yn = dlt * lax.rsqrt(var + GN_EPS) * lng_ref[...] + lnb_ref[...]
    y_ref[0] = (yn + bonus * v) * g

    @pl.when(t == pl.num_programs(1) - 1)
    def _():
        for gi in range(N_GROUPS):
            st_ref[0, gi] = _unbd(st_sc[gi])


def _state_spec():
    return pl.BlockSpec((1, N_GROUPS, HEAD_DIM, GROUP), lambda i, j: (i, 0, 0, 0))


def _rwkv(za, shift0, s0_cat, pr, L, nc):
    b, t, ac = za.shape
    dm = D_MIX
    tb = L * nc
    row = lambda a: a.reshape(1, -1)
    consts = [row(pr["mu"]), row(pr["w0"]), row(pr["a0"]), pr["wl"], row(pr["k_k"]), row(pr["k_a"]),
              row(pr["r_k"]), row(pr["lnx_g"]), row(pr["lnx_b"]), _block_ones(GROUP, HEAD_DIM)]
    return pl.pallas_call(
        functools.partial(_rwkv_kernel, L=L, nc=nc),
        grid=(b, t // tb),
        in_specs=[pl.BlockSpec((1, tb, ac), lambda i, j: (i, j, 0)),
                  pl.BlockSpec((1, 1, ac), lambda i, j: (i, 0, 0)), _state_spec()]
                 + [_const_spec(c.shape) for c in consts],
        out_specs=[pl.BlockSpec((1, tb, dm), lambda i, j: (i, j, 0)), _state_spec()],
        out_shape=[jax.ShapeDtypeStruct((b, t, dm), F32),
                   jax.ShapeDtypeStruct((b, N_GROUPS, HEAD_DIM, GROUP), F32)],
        scratch_shapes=[pltpu.VMEM((1, ac), F32), pltpu.VMEM((N_GROUPS, GROUP, GROUP), F32)],
        compiler_params=_params(("parallel", "arbitrary")),
        name="rwkv7",
    )(za, shift0.reshape(b, 1, ac), s0_cat, *consts)


def _hgrn_kernel(zq_ref, zf_ref, zi_ref, zg_ref, s0_ref, lb_ref, ng_ref, bo_ref, y_ref, st_ref, st_sc, *, L, nc):
    t = pl.program_id(1)
    tb = nc * L

    @pl.when(t == 0)
    def _():
        for gi in range(N_GROUPS):
            st_sc[gi] = _bd(s0_ref[0, gi], HEAD_DIM, HEAD_DIM)

    dm = D_MIX
    q = _silu(zq_ref[0])
    xf = zf_ref[0]
    v = zi_ref[0]
    lb = lb_ref[...]
    la = jnp.log(lb)
    lc = jnp.log1p(-lb) + _log_sigmoid(xf)
    logf = jnp.maximum(la, lc) + jnp.log1p(jnp.exp(-jnp.abs(la - lc)))
    k = (1.0 - lb) * _sigmoid(-xf)
    bo = bo_ref[...]

    ri = _iota((tb, tb), 0)
    ci = _iota((tb, tb), 1)
    tri = ((ci <= ri) & (ci // L == ri // L)).astype(BF16)
    b = _mm((tri,), _sp(logf, 3))
    b_last = jnp.concatenate(
        [jnp.broadcast_to(b[(j + 1) * L - 1:(j + 1) * L, :], (L, dm)) for j in range(nc)], axis=0)
    qe = q * jnp.exp(b)
    k_end = k * jnp.exp(b_last - b)
    g_last = jnp.exp(b_last)

    n_sub = L // SUB
    nst = N_HEADS * SUB
    stackmask = (_iota((nst, dm), 0) // SUB) == (_iota((nst, dm), 1) // HEAD_DIM)
    sub_rows = _iota((SUB, 1), 0)
    subs = [(cj * L, cj * L + i * SUB, i) for cj in range(nc) for i in range(n_sub)]
    rows_of = {sb: slice(sb[1], sb[1] + SUB) for sb in subs}
    brefs = {sb: jnp.zeros((1, dm), F32) if sb[2] == 0 else b[sb[1] - 1:sb[1]] for sb in subs}

    def fold_heads(ov):
        out = ov[0:SUB]
        for hh in range(1, N_HEADS):
            out = out + ov[hh * SUB:(hh + 1) * SUB]
        return out

    def stacked_q(sb):
        qh = q[rows_of[sb]] * jnp.exp(b[rows_of[sb]] - brefs[sb])
        return bf(jnp.where(stackmask, jnp.concatenate([qh] * N_HEADS, axis=0), 0.0))

    def intra_exact():
        xs = {sb: [] for sb in subs}
        for s in range(SUB):
            for sb in subs:
                bi = b[rows_of[sb]]
                e = jnp.exp(jnp.where(sub_rows >= s, bi - bi[s:s + 1], NEG))
                xs[sb].append(q[rows_of[sb]] * e * k[rows_of[sb]][s:s + 1])
        att = {sb: _head_sum(jnp.concatenate(xs[sb], axis=0), bo, terms=1) for sb in subs}
        later = [sb for sb in subs if sb[2] > 0]
        qst = {sb: stacked_q(sb) for sb in later}
        kh = {sb: bf(k[sb[0]:sb[1]] * jnp.exp(brefs[sb] - b[sb[0]:sb[1]])) for sb in later}
        att2 = {sb: _mm(qst[sb], kh[sb], _NT) for sb in later}
        ov = {sb: jnp.where(stackmask, _mm(bf(att2[sb]), bf(v[sb[0]:sb[1]])), 0.0) for sb in later}
        o_subs = []
        for sb in subs:
            vi = v[rows_of[sb]]
            oi = att[sb][0:SUB] * vi[0:1]
            for s in range(1, SUB):
                oi = oi + att[sb][s * SUB:(s + 1) * SUB] * vi[s:s + 1]
            if sb[2] > 0:
                oi = oi + fold_heads(ov[sb])
            o_subs.append(oi)
        return jnp.concatenate(o_subs, axis=0) if len(o_subs) > 1 else o_subs[0]

    def intra_factored():
        qst = {sb: stacked_q(sb) for sb in subs}
        kh = {sb: bf(k[sb[0]:sb[1] + SUB] * jnp.exp(brefs[sb] - b[sb[0]:sb[1] + SUB])) for sb in subs}
        att2 = {}
        for sb in subs:
            n_keys = sb[1] + SUB - sb[0]
            visible = _iota((nst, n_keys), 1) <= sb[2] * SUB + _iota((nst, n_keys), 0) % SUB
            att2[sb] = jnp.where(visible, _mm(qst[sb], kh[sb], _NT), 0.0)
        ov = {sb: jnp.where(stackmask, _mm(bf(att2[sb]), bf(v[sb[0]:sb[1] + SUB])), 0.0) for sb in subs}
        o_subs = [fold_heads(ov[sb]) for sb in subs]
        return jnp.concatenate(o_subs, axis=0) if len(o_subs) > 1 else o_subs[0]

    drop = jnp.concatenate([b[sb[1] + SUB - 1:sb[1] + SUB] - brefs[sb] for sb in subs], axis=0)
    o_intra = lax.cond(jnp.min(drop) >= -FACTORED_MAX_DECAY, intra_factored, intra_exact)

    groups = range(N_GROUPS)
    gsl = {gi: slice(gi * GROUP, (gi + 1) * GROUP) for gi in groups}
    bdmask = (_iota((GROUP, GROUP), 0) // HEAD_DIM) == (_iota((GROUP, GROUP), 1) // HEAD_DIM)
    upd = {(cj, gi): jnp.where(bdmask, _mm(bf(v[cj * L:(cj + 1) * L, gsl[gi]]),
                                          bf(k_end[cj * L:(cj + 1) * L, gsl[gi]]), _TN), 0.0)
           for cj in range(nc) for gi in groups}
    state = [st_sc[gi] for gi in groups]
    o_rows = []
    for cj in range(nc):
        rs = slice(cj * L, (cj + 1) * L)
        o_rows.append(jnp.concatenate([_mm(bf(qe[rs, gsl[gi]]), bf(state[gi]), _NT) for gi in groups], axis=1))
        state = [state[gi] * g_last[rs, gsl[gi]][0:1] + upd[cj, gi] for gi in groups]
    for gi in groups:
        st_sc[gi] = state[gi]
    o = o_intra + (jnp.concatenate(o_rows, axis=0) if nc > 1 else o_rows[0])

    ms = _head_sum(o * o, bo) * (1.0 / HEAD_DIM)
    y_ref[0] = o * lax.rsqrt(ms + RMS_EPS) * ng_ref[...] * _silu(zg_ref[0])

    @pl.when(t == pl.num_programs(1) - 1)
    def _():
        for gi in range(N_GROUPS):
            st_ref[0, gi] = _unbd(st_sc[gi])


def _hgrn(zq, zf, zi, zg, s0_cat, lb, norm_g, L, nc):
    b, t, dm = zq.shape
    tb = L * nc
    consts = [lb.reshape(1, dm), norm_g.reshape(1, dm), _block_ones(GROUP, HEAD_DIM)]
    row_spec = pl.BlockSpec((1, tb, dm), lambda i, j: (i, j, 0))
    return pl.pallas_call(
        functools.partial(_hgrn_kernel, L=L, nc=nc),
        grid=(b, t // tb),
        in_specs=[row_spec] * 4 + [_state_spec()] + [_const_spec(c.shape) for c in consts],
        out_specs=[row_spec, _state_spec()],
        out_shape=[jax.ShapeDtypeStruct((b, t, dm), F32),
                   jax.ShapeDtypeStruct((b, N_GROUPS, HEAD_DIM, GROUP), F32)],
        scratch_shapes=[pltpu.VMEM((N_GROUPS, GROUP, GROUP), F32)],
        compiler_params=_params(("parallel", "arbitrary")),
        name="hgrn2",
    )(zq, zf, zi, zg, s0_cat, *consts)


def _cumsum_kernel(x_ref, o_ref):
    x = x_ref[0]
    n, w = x.shape
    nblk = n // N_HEADS
    upper = (_iota((w, w), 0) <= _iota((w, w), 1)).astype(BF16)
    c = _mm(_sp(x, 3), (upper,))
    tot = jnp.broadcast_to(c[:, w - 1:w], (n, w))
    ri = _iota((n, n), 0)
    ci = _iota((n, n), 1)
    prior = ((ri // nblk == ci // nblk) & (ci < ri)).astype(BF16)
    o_ref[0] = c + _mm((prior,), _sp(tot, 3))


def _cumsum_time(logf_bht):
    b, h, t = logf_bht.shape
    w = 128
    n = h * (t // w)
    x = logf_bht.reshape(b, n, w)
    out = pl.pallas_call(
        _cumsum_kernel,
        grid=(b,),
        in_specs=[pl.BlockSpec((1, n, w), lambda i: (i, 0, 0))],
        out_specs=pl.BlockSpec((1, n, w), lambda i: (i, 0, 0)),
        out_shape=jax.ShapeDtypeStruct((b, n, w), F32),
        compiler_params=_params(("parallel",)),
        name="cumsum",
    )(x)
    return out.reshape(b, h, t)


def _fox_kernel(q_ref, k_ref, v_ref, f_ref, o_ref, kb_sc, vb_sc, *, tq):
    i = pl.program_id(2)

    @pl.when(i == 0)
    def _():
        kb_sc[...] = k_ref[0].astype(BF16)
        vb_sc[...] = v_ref[0].astype(BF16)

    q = q_ref[0] * ATTN_SCALE
    lane = _iota((1, 2 * HEAD_DIM), 1)
    causal = _iota((tq, tq), 1) <= _iota((tq, tq), 0)
    heads = range(2)
    head_lane = [(lane // HEAD_DIM) == hh for hh in heads]
    qm = [jnp.where(head_lane[hh], q, 0.0).astype(BF16) for hh in heads]

    def scores(j, hh):
        return _dot_nt(qm[hh], kb_sc[pl.ds(j * tq, tq), :]) - f_ref[0, 0, j, hh:hh + 1, :]

    def update(j, hh, s, m, acc):
        m_new = jnp.maximum(m, jnp.max(s, axis=-1, keepdims=True))
        p = jnp.exp(s - m_new).astype(BF16)
        vx = jnp.where(head_lane[hh], vb_sc[pl.ds(j * tq, tq), :], 1.0)
        return m_new, jnp.exp(m - m_new) * acc + _dot(p, vx)

    def run_blocks(first, count, diagonal_last, carry):
        ss = {(n, hh): scores(first + n, hh) for n in range(count) for hh in heads}
        if diagonal_last:
            for hh in heads:
                ss[count - 1, hh] = jnp.where(causal, ss[count - 1, hh], NEG)
        carry = list(carry)
        for n in range(count):
            for hh in heads:
                carry[hh] = update(first + n, hh, ss[n, hh], *carry[hh])
        return tuple(carry)

    init = tuple((jnp.full((tq, 1), NEG, F32), jnp.zeros((tq, 2 * HEAD_DIM), F32)) for _ in heads)
    n_full = i // FOX_GROUP
    carry = lax.fori_loop(0, n_full, lambda g, cr: run_blocks(g * FOX_GROUP, FOX_GROUP, False, cr), init)
    tails = [functools.partial(run_blocks, n_full * FOX_GROUP, r + 1, True) for r in range(FOX_GROUP)]
    (_, acc0), (_, acc1) = lax.switch(i % FOX_GROUP, tails, carry)
    outs = [acc / pltpu.roll(acc, HEAD_DIM, 1) for acc in (acc0, acc1)]
    o_ref[0] = jnp.where(head_lane[0], outs[0], outs[1])


def _fox_prompt(q, k, v, cum_bht, tq):
    b, t, dm = q.shape
    hp = dm // (2 * HEAD_DIM)
    nk = t // tq
    f = cum_bht.reshape(b, hp, 2, nk, tq).transpose(0, 1, 3, 2, 4)
    kv_spec = pl.BlockSpec((1, t, 2 * HEAD_DIM), lambda i, h, j: (i, 0, h))
    return pl.pallas_call(
        functools.partial(_fox_kernel, tq=tq),
        scratch_shapes=[pltpu.VMEM((t, 2 * HEAD_DIM), BF16), pltpu.VMEM((t, 2 * HEAD_DIM), BF16)],
        grid=(b, hp, nk),
        in_specs=[pl.BlockSpec((1, tq, 2 * HEAD_DIM), lambda i, h, j: (i, j, h)), kv_spec, kv_spec,
                  pl.BlockSpec((1, 1, nk, 2, tq), lambda i, h, j: (i, h, 0, 0, 0))],
        out_specs=pl.BlockSpec((1, tq, 2 * HEAD_DIM), lambda i, h, j: (i, j, h)),
        out_shape=jax.ShapeDtypeStruct((b, t, dm), F32),
        compiler_params=_params(("parallel", "parallel", "arbitrary")),
        name="fox_prompt",
    )(q, k, v, f)


def _band_kernel(q_ref, k_ref, v_ref, bias_ref, o_ref, *, tq):
    i = pl.program_id(2)
    q = q_ref[0] * ATTN_SCALE
    lane = _iota((1, 2 * HEAD_DIM), 1)
    n_piece = BAND // tq + 1
    kbs, vs, valid = [], [], []
    for p in range(n_piece):
        blk = i - (n_piece - 1) + p
        start = jnp.maximum(blk, 0) * tq
        kbs.append(k_ref[0, pl.ds(start, tq), :].astype(BF16))
        vs.append(v_ref[0, pl.ds(start, tq), :].astype(BF16))
        valid.append(blk >= 0)
    scores = {}
    for hh in range(2):
        qm = jnp.where((lane // HEAD_DIM) == hh, q, 0.0).astype(BF16)
        for p in range(n_piece):
            s = _dot_nt(qm, kbs[p]) + bias_ref[0, hh, :, p * tq:(p + 1) * tq]
            scores[hh, p] = jnp.where(valid[p], s, NEG)
    outs = []
    for hh in range(2):
        ss = [scores[hh, p] for p in range(n_piece)]
        m = jnp.max(ss[0], axis=-1, keepdims=True)
        for s in ss[1:]:
            m = jnp.maximum(m, jnp.max(s, axis=-1, keepdims=True))
        l = jnp.zeros((tq, 1), F32)
        acc = jnp.zeros((tq, 2 * HEAD_DIM), F32)
        for s, vb in zip(ss, vs):
            p_ = jnp.exp(s - m)
            l = l + jnp.sum(p_, axis=-1, keepdims=True)
            acc = acc + _dot(p_.astype(BF16), vb)
        outs.append(acc / l)
    o_ref[0] = jnp.where((lane // HEAD_DIM) == 0, outs[0], outs[1])


def _toeplitz_bias(rel_bias, nq, nk, offset):
    h = rel_bias.shape[0]
    m = jnp.arange(nq + nk - 1)
    u = rel_bias[:, jnp.clip(offset + nq - 1 - m, -REL_CLIP, REL_CLIP) + REL_CLIP].astype(F32)
    period = nq + nk
    up = jnp.pad(u, ((0, 0), (0, 1)))
    skew = jnp.tile(up, (1, nq))[:, :nq * (period - 1)].reshape(h, nq, period - 1)
    return skew[:, :, nq - 1:]


def _band_bias_prompt(rel_bias, tq):
    nk = BAND + tq
    bias = _toeplitz_bias(rel_bias, tq, nk, BAND)
    qc = jnp.arange(tq)[:, None] // CHUNK
    kc = jnp.arange(nk)[None, :] // CHUNK
    valid = (kc >= qc) & (kc <= qc + BAND // CHUNK)
    return jnp.where(valid[None], bias, NEG)


def _band_prompt(q, k, v, rel_bias, tq):
    b, t, dm = q.shape
    hp = dm // (2 * HEAD_DIM)
    bias = _band_bias_prompt(rel_bias, tq).reshape(hp, 2, tq, BAND + tq)
    kv_spec = pl.BlockSpec((1, t, 2 * HEAD_DIM), lambda i, h, j: (i, 0, h))
    return pl.pallas_call(
        functools.partial(_band_kernel, tq=tq),
        grid=(b, hp, t // tq),
        in_specs=[pl.BlockSpec((1, tq, 2 * HEAD_DIM), lambda i, h, j: (i, j, h)), kv_spec, kv_spec,
                  pl.BlockSpec((1, 2, tq, BAND + tq), lambda i, h, j: (h, 0, 0, 0))],
        out_specs=pl.BlockSpec((1, tq, 2 * HEAD_DIM), lambda i, h, j: (i, j, h)),
        out_shape=jax.ShapeDtypeStruct((b, t, dm), F32),
        compiler_params=_params(("parallel", "parallel", "arbitrary")),
        name="band_prompt",
    )(q, k, v, bias)


def _cached_attn_kernel(*refs, fox):
    if fox:
        q_ref, kn_ref, vn_ref, ck_ref, cv_ref, fc_ref, fn_ref, o_ref = refs
    else:
        q_ref, kn_ref, vn_ref, ck_ref, cv_ref, bc_ref, bn_ref, o_ref = refs
    tn, dm = q_ref.shape[1], q_ref.shape[2]
    nst = N_HEADS * tn
    stackmask = (_iota((nst, dm), 0) // tn) == (_iota((nst, dm), 1) // HEAD_DIM)
    q = q_ref[0] * ATTN_SCALE
    qst = jnp.where(stackmask, jnp.concatenate([q] * N_HEADS, axis=0), 0.0).astype(BF16)
    s_c = _dot_nt(qst, ck_ref[0].astype(BF16))
    s_n = _dot_nt(qst, kn_ref[0].astype(BF16))
    if fox:
        expand = ((_iota((nst, N_HEADS), 0) // tn) == _iota((nst, N_HEADS), 1)).astype(BF16)
        fc = fc_ref[0]
        upper = (_iota((tn, tn), 0) <= _iota((tn, tn), 1)).astype(BF16)
        fnew = fc[:, fc.shape[1] - 1:] + _mm(_sp(fn_ref[0], 3), (upper,))
        s_c = s_c - _mm((expand,), _sp(fc, 3))
        s_n = s_n - _mm((expand,), _sp(fnew, 3))
        tq = _iota((nst, tn), 0) % tn
        s_n = jnp.where(_iota((nst, tn), 1) <= tq, s_n, NEG)
    else:
        s_c = s_c + bc_ref[...]
        s_n = s_n + bn_ref[...]
    m = jnp.maximum(jnp.max(s_c, axis=-1, keepdims=True), jnp.max(s_n, axis=-1, keepdims=True))
    p_c = jnp.exp(s_c - m)
    p_n = jnp.exp(s_n - m)
    l = jnp.sum(p_c, axis=-1, keepdims=True) + jnp.sum(p_n, axis=-1, keepdims=True)
    ov = _dot(p_c.astype(BF16), cv_ref[0].astype(BF16)) + _dot(p_n.astype(BF16), vn_ref[0].astype(BF16))
    ov = jnp.where(stackmask, ov / l, 0.0)
    out = ov[0:tn]
    for hh in range(1, N_HEADS):
        out = out + ov[hh * tn:(hh + 1) * tn]
    o_ref[0] = out


def _cached_attn(q, kn, vn, ck, cv, extra_c, extra_n, fox):
    b, tn, dm = q.shape
    p = ck.shape[1]
    new_spec = pl.BlockSpec((1, tn, dm), lambda i: (i, 0, 0))
    cache_spec = pl.BlockSpec((1, p, dm), lambda i: (i, 0, 0))
    if fox:
        ex_specs = [pl.BlockSpec((1, N_HEADS, p), lambda i: (i, 0, 0)),
                    pl.BlockSpec((1, N_HEADS, tn), lambda i: (i, 0, 0))]
    else:
        ex_specs = [_const_spec(extra_c.shape), _const_spec(extra_n.shape)]
    return pl.pallas_call(
        functools.partial(_cached_attn_kernel, fox=fox),
        grid=(b,),
        in_specs=[new_spec, new_spec, new_spec, cache_spec, cache_spec] + ex_specs,
        out_specs=new_spec,
        out_shape=jax.ShapeDtypeStruct((b, tn, dm), F32),
        compiler_params=_params(("parallel",)),
        name="fox_sample" if fox else "band_sample",
    )(q, kn, vn, ck, cv, extra_c, extra_n)


def _to_cat(s):
    b = s.shape[0]
    s5 = s.reshape(b, N_GROUPS, HEADS_PER_GROUP, HEAD_DIM, HEAD_DIM)
    return s5.transpose(0, 1, 3, 2, 4).reshape(b, N_GROUPS, HEAD_DIM, GROUP)


def _from_cat(c):
    b = c.shape[0]
    c5 = c.reshape(b, N_GROUPS, HEAD_DIM, HEADS_PER_GROUP, HEAD_DIM)
    return c5.transpose(0, 1, 3, 2, 4).reshape(b, N_HEADS, HEAD_DIM, HEAD_DIM)


def _trunk(x, mods, P, cache, tm, tq, L, nc):
    b, t, d = x.shape
    dm = D_MIX
    sample = cache is not None
    new = {}
    za, q, k, v, logf = _proj(x, mods[0], P["norm_mix_g"][0], P["ab_w"], (P["a_cols"], dm, dm, dm), tm,
                              wf=P["ab_wf"], bf=P["fox_bf"])
    ac = P["a_cols"]
    if sample:
        nb, tn = cache["nb"], cache["tn"]
        za_b = za.reshape(nb, tn, ac)
        shift0 = cache["rwkv_shift"][0]
        s0 = _to_cat(cache["rwkv"][0])
    else:
        nb, tn = b, t
        za_b = za
        shift0 = jnp.zeros((nb, ac), F32)
        s0 = jnp.zeros((nb, N_GROUPS, HEAD_DIM, GROUP), F32)
    ya, st = _rwkv(za_b, shift0, s0, P["rwkv"], L, nc)
    new["rwkv"] = _from_cat(st)[None]
    new["rwkv_shift"] = za_b[:, -1][None]
    qb, kb, vb = (a.reshape(nb, tn, dm) for a in (q, k, v))
    logf_b = logf.reshape(nb, tn, N_HEADS)
    logf_t = jnp.swapaxes(logf_b, 1, 2)
    if sample:
        ck = cache["fox_k"][0].reshape(nb, -1, dm)
        cv = cache["fox_v"][0].reshape(nb, -1, dm)
        fc = _cumsum_time(jnp.swapaxes(cache["fox_logf"][0], 1, 2))
        yb = _cached_attn(qb, kb, vb, ck, cv, fc, logf_t, fox=True)
    else:
        yb = _fox_prompt(qb, kb, vb, _cumsum_time(logf_t), tq)
    new["fox_k"] = kb.reshape(1, nb, tn, N_HEADS, HEAD_DIM)
    new["fox_v"] = vb.reshape(1, nb, tn, N_HEADS, HEAD_DIM)
    new["fox_logf"] = logf_b[None]
    fix = None
    if sample:
        buf = cache["ffn_conv"][0]
        zero = jnp.zeros((nb, tn - 2, buf.shape[-1]), F32)
        fix = (jnp.concatenate([buf[:, 1:2], buf[:, 0:1] * 0, zero], axis=1).reshape(1, nb * tn, -1),
               jnp.concatenate([buf, zero], axis=1).reshape(1, nb * tn, -1))
    res = _tail(x, ya.reshape(b, t, dm), yb.reshape(b, t, dm), mods[0], mods[1], P["ab_wo"],
                P["norm_ffn_g"][0], P["wup"][0], P["conv_w"][0], P["conv_b"][0], P["wdn"][0],
                P["final_g"], False, tm, fix=fix, period=tn)
    x = res[0]
    conv0 = res[1].reshape(nb, tn, -1)[:, tn - 2:] if sample else res[1]
    q, k, v, zq, zf, zi, zg = _proj(x, mods[2], P["norm_mix_g"][1], P["cd_w"], (dm,) * 7, tm)
    qb, kb, vb, zq, zf, zi, zg = (a.reshape(nb, tn, dm) for a in (q, k, v, zq, zf, zi, zg))
    if sample:
        ck = cache["chunk_k"][0].reshape(nb, -1, dm)
        cv = cache["chunk_v"][0].reshape(nb, -1, dm)
        yc = _cached_attn(qb, kb, vb, ck, cv, P["band_bias_c"], P["band_bias_n"], fox=False)
        new["chunk_k"] = kb.reshape(1, nb, tn, N_HEADS, HEAD_DIM)
        new["chunk_v"] = vb.reshape(1, nb, tn, N_HEADS, HEAD_DIM)
        s0 = _to_cat(jnp.swapaxes(cache["hgrn"][0], -1, -2))
    else:
        yc = _band_prompt(qb, kb, vb, P["rel_bias"], 4 * CHUNK)
        keep = min(BAND, tn)
        new["chunk_k"] = kb[:, tn - keep:].reshape(1, nb, keep, N_HEADS, HEAD_DIM)
        new["chunk_v"] = vb[:, tn - keep:].reshape(1, nb, keep, N_HEADS, HEAD_DIM)
        s0 = jnp.zeros((nb, N_GROUPS, HEAD_DIM, GROUP), F32)
    yd, st = _hgrn(zq, zf, zi, zg, s0, P["hgrn_lb"], P["hgrn_norm_g"], L, nc)
    new["hgrn"] = jnp.swapaxes(_from_cat(st), -1, -2)[None]
    if sample:
        buf = cache["ffn_conv"][1]
        fix = (jnp.concatenate([buf[:, 1:2], buf[:, 0:1] * 0, zero], axis=1).reshape(1, nb * tn, -1),
               jnp.concatenate([buf, zero], axis=1).reshape(1, nb * tn, -1))
    res = _tail(x, yc.reshape(b, t, dm), yd.reshape(b, t, dm), mods[2], mods[3], P["cd_wo"],
                P["norm_ffn_g"][1], P["wup"][1], P["conv_w"][1], P["conv_b"][1], P["wdn"][1],
                P["final_g"], True, tm, fix=fix, period=tn)
    conv1 = res[1].reshape(nb, tn, -1)[:, tn - 2:] if sample else res[1]
    new["ffn_conv"] = jnp.stack([conv0, conv1])
    y = res[0].reshape(nb, tn, d)
    return y, new


def kernel(x_prompt, x_sample, c_prompt, c_sample, cache_fox_k, cache_fox_v, cache_fox_logf, state_rwkv,
           state_rwkv_shift, cache_chunk_k, cache_chunk_v, state_hgrn, state_ffn_conv, ada_w, ada_b,
           norm_mix_g, norm_ffn_g, ab_w_in, rwkv_mu, rwkv_w0, rwkv_w2, rwkv_a0, rwkv_a2, rwkv_g2,
           rwkv_k_k, rwkv_k_a, rwkv_r_k, rwkv_lnx_g, rwkv_lnx_b, fox_b_f, ab_w_out, cd_w_in,
           chunk_rel_bias, hgrn_lb_table, hgrn_norm_g, cd_w_out, ffn_w_up, ffn_conv_w, ffn_conv_b,
           ffn_w_down, final_norm_g):
    bp, tp, d = x_prompt.shape
    bs, ts, _ = x_sample.shape
    depth = ada_w.shape[0]
    dm = D_MIX
    a_cols = rwkv_mu.shape[1]
    n_lw, n_la, n_lg = rwkv_w2.shape[1], rwkv_a2.shape[1], rwkv_g2.shape[1]

    c_all = jnp.concatenate([c_prompt, c_sample], axis=0)
    mods = _adaln(c_all, ada_w.reshape(depth * 2, d, 3 * d), ada_b.reshape(depth * 2, 3 * d))
    mods_p = [mods[i, :bp].reshape(bp, 1, 3 * d) for i in range(depth * 2)]
    mods_s = [jnp.repeat(mods[i, bp:], ts, axis=0).reshape(1, bs * ts, 3 * d) for i in range(depth * 2)]

    wl = jnp.zeros((n_lw + n_la + n_lg, 3 * dm), F32)
    wl = wl.at[:n_lw, 0:dm].set(rwkv_w2[0])
    wl = wl.at[n_lw:n_lw + n_la, dm:2 * dm].set(rwkv_a2[0])
    wl = wl.at[n_lw + n_la:, 2 * dm:].set(rwkv_g2[0])
    sm = jax.nn.softmax(hgrn_lb_table.astype(F32), axis=0)
    lb = (jnp.cumsum(sm, axis=0) - sm[0])[1]
    wf = jnp.zeros((d, 128), F32).at[:, :N_HEADS].set(ab_w_in[0][:, a_cols + 3 * dm:]).astype(BF16)
    bf = jnp.zeros((1, 128), F32).at[0, :N_HEADS].set(fox_b_f[0])
    p_c = cache_chunk_k.shape[2]
    bias_s = _toeplitz_bias(chunk_rel_bias[0], ts, p_c + ts, p_c).reshape(N_HEADS * ts, p_c + ts)

    P = {
        "a_cols": a_cols,
        "norm_mix_g": norm_mix_g, "norm_ffn_g": norm_ffn_g, "final_g": final_norm_g,
        "ab_w": ab_w_in[0][:, :a_cols + 3 * dm].astype(BF16), "ab_wf": wf, "fox_bf": bf,
        "ab_wo": ab_w_out[0].astype(BF16),
        "cd_w": cd_w_in[0].astype(BF16), "cd_wo": cd_w_out[0].astype(BF16),
        "wup": [ffn_w_up[i].astype(BF16) for i in range(depth)],
        "wdn": [ffn_w_down[i].astype(BF16) for i in range(depth)],
        "conv_w": ffn_conv_w, "conv_b": ffn_conv_b,
        "rwkv": {"mu": rwkv_mu[0], "w0": rwkv_w0[0], "a0": rwkv_a0[0], "wl": wl.astype(BF16), "k_k": rwkv_k_k[0],
                 "k_a": rwkv_k_a[0], "r_k": rwkv_r_k[0], "lnx_g": rwkv_lnx_g[0], "lnx_b": rwkv_lnx_b[0]},
        "rel_bias": chunk_rel_bias[0], "band_bias_c": bias_s[:, :p_c], "band_bias_n": bias_s[:, p_c:],
        "hgrn_lb": lb, "hgrn_norm_g": hgrn_norm_g[0],
    }
    cache = {"nb": bs, "tn": ts, "fox_k": cache_fox_k, "fox_v": cache_fox_v, "fox_logf": cache_fox_logf,
             "rwkv": state_rwkv, "rwkv_shift": state_rwkv_shift, "chunk_k": cache_chunk_k,
             "chunk_v": cache_chunk_v, "hgrn": state_hgrn, "ffn_conv": state_ffn_conv}

    y_p, sp = _trunk(x_prompt, mods_p, P, None, tm=512, tq=512, L=CHUNK, nc=8)
    y_s, ss = _trunk(x_sample.reshape(1, bs * ts, d), mods_s, P, cache, tm=bs * ts, tq=None, L=ts, nc=1)
    names = ("fox_k", "fox_v", "fox_logf", "rwkv", "rwkv_shift", "chunk_k", "chunk_v", "hgrn", "ffn_conv")
    return (y_p, y_s) + tuple(sp[n] for n in names) + tuple(ss[n] for n in names)
```
